```python
import jax, jax.numpy as jnp
from jax import lax
import numpy as np

D_MODEL = 1024
BATCH = 2
SEQ = 8192
DEPTH = 2
DEC_BATCH = 32
DEC_SEQ = 16
PAST_LEN = 2048

CHUNK = 64
Q_BLOCK = 128
D_MIX = D_MODEL
FOX_HEADS = 8
FOX_HEAD_DIM = 64
FOX_WIDTH = FOX_HEADS * FOX_HEAD_DIM
HGRN_HEADS = 4
HGRN_HEAD_DIM = 128
HGRN_WIDTH = HGRN_HEADS * HGRN_HEAD_DIM
D_IN_PROJ = 3 * FOX_WIDTH + FOX_HEADS + 4 * HGRN_WIDTH
D_FF = 2816
RMS_EPS = 1e-6
FOX_SCALE = FOX_HEAD_DIM ** -0.5

kernel_name = "hymba_fox_hgrn2_macaron_step"


def rms_norm(x, g):
    xf = x.astype(jnp.float32)
    y = xf * lax.rsqrt(jnp.mean(xf * xf, axis=-1, keepdims=True) + RMS_EPS)
    return (y * g.astype(jnp.float32)).astype(x.dtype)


def swiglu_ffn(x, wi, wo):
    a, b = jnp.split(x @ wi, 2, axis=-1)
    return (jax.nn.silu(a) * b) @ wo


def hgrn_lower_bounds(lb_param):
    cs = jnp.cumsum(jax.nn.softmax(lb_param.astype(jnp.float32), axis=0), axis=0)
    return cs - cs[0:1]


def project_mixers(h, w_in, b_f, lb):
    B, L, _ = h.shape
    F, H, W = FOX_WIDTH, FOX_HEADS, HGRN_WIDTH
    proj = h @ w_in
    q, k, v, fg, hq, hf, hi, hg = jnp.split(
        proj, [F, 2 * F, 3 * F, 3 * F + H, 3 * F + H + W, 3 * F + H + 2 * W, 3 * F + H + 3 * W], axis=-1)
    fox_heads = lambda t: t.reshape(B, L, FOX_HEADS, FOX_HEAD_DIM).transpose(0, 2, 1, 3)
    fox_logf = jax.nn.log_sigmoid(fg.astype(jnp.float32) + b_f.astype(jnp.float32)).transpose(0, 2, 1)
    z = hf.astype(jnp.float32)
    h_logf = jnp.logaddexp(jnp.log(lb), jnp.log1p(-lb) + jax.nn.log_sigmoid(z))
    h_k = (1.0 - lb) * jax.nn.sigmoid(-z)
    hs = lambda t: t.reshape(B, L, HGRN_HEADS, HGRN_HEAD_DIM)
    return (fox_heads(q), fox_heads(k), fox_heads(v), fox_logf,
            hs(jax.nn.silu(hq)), hs(h_logf), hs(h_k), hs(hi), hs(hg))


def fox_attend(q, c_q, q_pos, k, v, c_k):
    s = jnp.einsum('bhqd,bhkd->bhqk', q, k).astype(jnp.float32) * FOX_SCALE
    s = s + (c_q[..., :, None] - c_k[..., None, :])
    mask = jnp.arange(k.shape[2])[None, :] <= q_pos[:, None]
    p = jax.nn.softmax(jnp.where(mask, s, -jnp.inf), axis=-1)
    return jnp.einsum('bhqk,bhkd->bhqd', p.astype(v.dtype), v)


def fox_prompt(q, k, v, logf):
    B, H, L, HD = q.shape
    nb = L // Q_BLOCK
    c = jnp.cumsum(logf, axis=-1)
    qb = q.reshape(B, H, nb, Q_BLOCK, HD).transpose(2, 0, 1, 3, 4)
    cb = c.reshape(B, H, nb, Q_BLOCK).transpose(2, 0, 1, 3)
    pos = jnp.arange(L, dtype=jnp.int32).reshape(nb, Q_BLOCK)
    o = lax.map(lambda a: fox_attend(a[0], a[1], a[2], k, v, c), (qb, cb, pos))
    return o.transpose(1, 2, 0, 3, 4).reshape(B, H, L, HD)


def fox_sample(q, k, v, logf, ck, cv, clogf):
    P, T = ck.shape[2], q.shape[2]
    k_all = jnp.concatenate([ck.astype(k.dtype), k], axis=2)
    v_all = jnp.concatenate([cv.astype(v.dtype), v], axis=2)
    c_all = jnp.cumsum(jnp.concatenate([clogf.astype(jnp.float32), logf], axis=-1), axis=-1)
    pos = P + jnp.arange(T, dtype=jnp.int32)
    return fox_attend(q, c_all[..., P:], pos, k_all, v_all, c_all)


def hgrn2_chunked(q, logf, k, v, s0):
    B, L, H, DK = q.shape
    DV = v.shape[-1]
    C = min(CHUNK, L)
    n = L // C
    blocks = lambda t: t.astype(jnp.float32).reshape(B, n, C, H, t.shape[-1]).transpose(1, 0, 3, 2, 4)
    causal = jnp.tril(jnp.ones((C, C), dtype=bool))

    def step(S, inp):
        qc, lfc, kc, vc = inp
        b = jnp.cumsum(lfc, axis=2)
        diff = b[:, :, :, None, :] - b[:, :, None, :, :]
        decay = jnp.exp(jnp.where(causal[:, :, None], diff, -jnp.inf))
        scores = jnp.einsum('bhtd,bhsd,bhtsd->bhts', qc, kc, decay)
        o = jnp.einsum('bhts,bhsv->bhtv', scores, vc) + jnp.einsum('bhtd,bhdv->bhtv', qc * jnp.exp(b), S)
        b_last = b[:, :, -1]
        S_new = jnp.exp(b_last)[..., None] * S + jnp.einsum(
            'bhsd,bhsv->bhdv', kc * jnp.exp(b_last[:, :, None] - b), vc)
        return S_new, o

    S, o = lax.scan(step, s0.astype(jnp.float32), (blocks(q), blocks(logf), blocks(k), blocks(v)))
    return o.transpose(1, 0, 3, 2, 4).reshape(B, L, H, DV), S


def trunk_layer(x, l, p, lb, fox_fn, s0):
    B, L, _ = x.shape
    x = x + 0.5 * swiglu_ffn(rms_norm(x, p['norm_ffn1'][l]), p['ffn1_wi'][l], p['ffn1_wo'][l])
    h = rms_norm(x, p['norm_mix'][l])
    fq, fk, fv, flogf, hq, hlogf, hk, hv, hg = project_mixers(h, p['w_in'][l], p['b_fgate'][l], lb)
    fox_o = fox_fn(fq, fk, fv, flogf).transpose(0, 2, 1, 3).reshape(B, L, FOX_WIDTH)
    ho, S = hgrn2_chunked(hq, hlogf, hk, hv, s0)
    hgrn_o = (rms_norm(ho, p['hgrn_gnorm'][l]) * jax.nn.silu(hg.astype(jnp.float32))).reshape(B, L, HGRN_WIDTH)
    mix = jnp.concatenate([fox_o, hgrn_o.astype(x.dtype)], axis=-1)
    x = x + mix @ p['w_out'][l]
    x = x + 0.5 * swiglu_ffn(rms_norm(x, p['norm_ffn2'][l]), p['ffn2_wi'][l], p['ffn2_wo'][l])
    return x, fk, fv, flogf, S


def setup_inputs(seed: int = 0) -> dict:
    key = jax.random.key(seed)
    ks = jax.random.split(key, 20)
    nrm = lambda k, shape, scale: jax.random.normal(k, shape, jnp.float32) * scale
    gain = lambda k, shape: 1.0 + nrm(k, shape, 0.05)
    return {
        'x_prompt': nrm(ks[0], (BATCH, SEQ, D_MODEL), 1.0),
        'x_sample': nrm(ks[1], (DEC_BATCH, DEC_SEQ, D_MODEL), 1.0),
        'cache_k': nrm(ks[2], (DEPTH, DEC_BATCH, FOX_HEADS, PAST_LEN, FOX_HEAD_DIM), 1.0),
        'cache_v': nrm(ks[3], (DEPTH, DEC_BATCH, FOX_HEADS, PAST_LEN, FOX_HEAD_DIM), 1.0),
        'cache_logf': jax.nn.log_sigmoid(1.0 + nrm(ks[4], (DEPTH, DEC_BATCH, FOX_HEADS, PAST_LEN), 1.0)),
        'state_hgrn': nrm(ks[5], (DEPTH, DEC_BATCH, HGRN_HEADS, HGRN_HEAD_DIM, HGRN_HEAD_DIM), 0.5),
        'norm_ffn1': gain(ks[6], (DEPTH, D_MODEL)),
        'ffn1_wi': nrm(ks[7], (DEPTH, D_MODEL, 2 * D_FF), D_MODEL ** -0.5),
        'ffn1_wo': nrm(ks[8], (DEPTH, D_FF, D_MODEL), D_FF ** -0.5),
        'norm_mix': gain(ks[9], (DEPTH, D_MODEL)),
        'w_in': nrm(ks[10], (DEPTH, D_MODEL, D_IN_PROJ), D_MODEL ** -0.5),
        'b_fgate': 1.0 + nrm(ks[11], (DEPTH, FOX_HEADS), 0.1),
        'hgrn_lb': nrm(ks[12], (DEPTH, HGRN_WIDTH), 0.5),
        'hgrn_gnorm': gain(ks[13], (DEPTH, HGRN_HEAD_DIM)),
        'w_out': nrm(ks[14], (DEPTH, D_MIX, D_MODEL), D_MIX ** -0.5),
        'norm_ffn2': gain(ks[15], (DEPTH, D_MODEL)),
        'ffn2_wi': nrm(ks[16], (DEPTH, D_MODEL, 2 * D_FF), D_MODEL ** -0.5),
        'ffn2_wo': nrm(ks[17], (DEPTH, D_FF, D_MODEL), D_FF ** -0.5),
        'norm_final': gain(ks[18], (D_MODEL,)),
    }


def reference(x_prompt, x_sample, cache_k, cache_v, cache_logf, state_hgrn,
              norm_ffn1, ffn1_wi, ffn1_wo, norm_mix, w_in, b_fgate, hgrn_lb, hgrn_gnorm,
              w_out, norm_ffn2, ffn2_wi, ffn2_wo, norm_final):
    p = {'norm_ffn1': norm_ffn1, 'ffn1_wi': ffn1_wi, 'ffn1_wo': ffn1_wo, 'norm_mix': norm_mix,
         'w_in': w_in, 'b_fgate': b_fgate, 'hgrn_gnorm': hgrn_gnorm, 'w_out': w_out,
         'norm_ffn2': norm_ffn2, 'ffn2_wi': ffn2_wi, 'ffn2_wo': ffn2_wo}
    lbs = hgrn_lower_bounds(hgrn_lb)
    xp, xs = x_prompt, x_sample
    s0_prompt = jnp.zeros((x_prompt.shape[0], HGRN_HEADS, HGRN_HEAD_DIM, HGRN_HEAD_DIM), jnp.float32)
    kp_l, vp_l, lfp_l, sp_l, ks_l, vs_l, lfs_l, ss_l = [], [], [], [], [], [], [], []
    for l in range(DEPTH):
        xp, kp, vp, lfp, sp = trunk_layer(xp, l, p, lbs[l], fox_prompt, s0_prompt)
        fox_fn = lambda q, k, v, lf, l=l: fox_sample(q, k, v, lf, cache_k[l], cache_v[l], cache_logf[l])
        xs, kss, vss, lfs, ss = trunk_layer(xs, l, p, lbs[l], fox_fn, state_hgrn[l])
        kp_l.append(kp); vp_l.append(vp); lfp_l.append(lfp); sp_l.append(sp)
        ks_l.append(kss); vs_l.append(vss); lfs_l.append(lfs); ss_l.append(ss)
    y_prompt = rms_norm(xp, norm_final)
    y_sample = rms_norm(xs, norm_final)
    return (y_prompt, y_sample,
            jnp.stack(kp_l), jnp.stack(vp_l), jnp.stack(lfp_l), jnp.stack(sp_l),
            jnp.stack(ks_l), jnp.stack(vs_l), jnp.stack(lfs_l), jnp.stack(ss_l))
```

```python
import functools

import jax
import jax.numpy as jnp
from jax import lax
from jax.experimental import pallas as pl
from jax.experimental.pallas import tpu as pltpu

F32 = jnp.float32
BF16 = jnp.bfloat16

D_MODEL = 1024
DEPTH = 2
FOX_HEADS = 8
FOX_HEAD_DIM = 64
FOX_WIDTH = FOX_HEADS * FOX_HEAD_DIM
HGRN_HEADS = 4
HGRN_HEAD_DIM = 128
HGRN_WIDTH = HGRN_HEADS * HGRN_HEAD_DIM
D_FF = 2816
RMS_EPS = 1e-6
FOX_SCALE = FOX_HEAD_DIM ** -0.5

LANES = 128
VMEM_LIMIT_BYTES = 56 * 1024 * 1024
HGRN_FACTOR_LIMIT = 50.0


def _cparams(*sem):
    return pltpu.CompilerParams(dimension_semantics=sem, vmem_limit_bytes=VMEM_LIMIT_BYTES)


def _rms(x, g):
    ms = jnp.mean(x * x, axis=-1, keepdims=True)
    return x * lax.rsqrt(ms + RMS_EPS) * g


def _log_sigmoid(z):
    return jnp.minimum(z, 0.0) - jnp.log1p(jnp.exp(-jnp.abs(z)))


def _dot(a, b):
    return jnp.dot(a, b, preferred_element_type=F32)


def _dot_nt(a, b):
    return lax.dot_general(a, b, (((1,), (1,)), ((), ())), preferred_element_type=F32)


def _dot_tn(a, b):
    return lax.dot_general(a, b, (((0,), (0,)), ((), ())), preferred_element_type=F32)


def _scan(x, axis):
    n = x.shape[axis]
    idx = lax.broadcasted_iota(jnp.int32, x.shape, axis)
    s = 1
    while s < n:
        x = x + jnp.where(idx >= s, pltpu.roll(x, s, axis), 0.0)
        s *= 2
    return x


def _ffn_body(x_ref, g_ref, wa_ref, wb_ref, wo_ref, gf_ref, o_ref, xn_ref, acc_ref, *, final_norm):
    j = pl.program_id(1)

    @pl.when(j == 0)
    def _():
        xn_ref[...] = _rms(x_ref[...], g_ref[...]).astype(BF16)
        acc_ref[...] = jnp.zeros_like(acc_ref)

    xn = xn_ref[...]
    a = _dot(xn, wa_ref[...])
    b = _dot(xn, wb_ref[...])
    h = (a * jax.nn.sigmoid(a)) * b
    acc_ref[...] += _dot(h.astype(BF16), wo_ref[...])

    @pl.when(j == pl.num_programs(1) - 1)
    def _():
        y = x_ref[...] + 0.5 * acc_ref[...]
        if final_norm:
            y = _rms(y, gf_ref[...])
        o_ref[...] = y


def _ffn(x, g, wi, wo, gf, *, final_norm, tm, tf):
    m, d = x.shape
    f = wo.shape[0]
    nf = f // tf
    return pl.pallas_call(
        functools.partial(_ffn_body, final_norm=final_norm),
        grid=(m // tm, nf),
        in_specs=[
            pl.BlockSpec((tm, d), lambda i, j: (i, 0)),
            pl.BlockSpec((1, d), lambda i, j: (0, 0)),
            pl.BlockSpec((d, tf), lambda i, j: (0, j)),
            pl.BlockSpec((d, tf), lambda i, j: (0, j + nf)),
            pl.BlockSpec((tf, d), lambda i, j: (j, 0)),
            pl.BlockSpec((1, d), lambda i, j: (0, 0)),
        ],
        out_specs=pl.BlockSpec((tm, d), lambda i, j: (i, 0)),
        out_shape=jax.ShapeDtypeStruct((m, d), F32),
        scratch_shapes=[pltpu.VMEM((tm, d), BF16), pltpu.VMEM((tm, d), F32)],
        compiler_params=_cparams("parallel", "arbitrary"),
        name="ffn",
    )(x, g, wi, wi, wo, gf)


def _inproj_body(x_ref, g_ref, wm_ref, wf_ref, bf_ref, lbp_ref,
                 q_ref, k_ref, v_ref, kf_ref, vf_ref, lft_ref, lfr_ref,
                 hq_ref, hlf_ref, hk_ref, hv_ref, hg_ref, *, layer):
    h = _rms(x_ref[...], g_ref[...]).astype(BF16)
    w = FOX_WIDTH

    def sec(i):
        return _dot(h, wm_ref[:, i * w:(i + 1) * w])

    q = sec(0)
    q_ref[...] = (q * FOX_SCALE).astype(BF16)
    k = sec(1)
    kf_ref[...] = k
    k_ref[...] = k.astype(BF16)
    v = sec(2)
    vf_ref[...] = v
    v_ref[...] = v.astype(BF16)

    lf = _log_sigmoid(_dot(h, wf_ref[...]) + bf_ref[...])
    lft_ref[...] = lf
    lfr_ref[...] = lf.T[:FOX_HEADS, :]

    hq = sec(3)
    hq_ref[...] = hq * jax.nn.sigmoid(hq)

    p = lbp_ref[...]
    e = jnp.exp(p - jnp.max(p, axis=0, keepdims=True))
    sm = e / jnp.sum(e, axis=0, keepdims=True)
    cs = sm[0:1]
    for r in range(1, layer + 1):
        cs = cs + sm[r:r + 1]
    lb = cs - sm[0:1]

    z = sec(4)
    a = jnp.log(lb)
    b = jnp.log1p(-lb) + _log_sigmoid(z)
    hlf_ref[...] = jnp.maximum(a, b) + jnp.log1p(jnp.exp(-jnp.abs(a - b)))
    hk_ref[...] = (1.0 - lb) * jax.nn.sigmoid(-z)
    hv_ref[...] = sec(5).astype(BF16)
    hg_ref[...] = sec(6)


def _inproj(x, g, wm, wf, bfp, lbp, *, layer, tm):
    m, d = x.shape
    w = FOX_WIDTH
    tok = lambda width: pl.BlockSpec((tm, width), lambda i: (i, 0))
    full = lambda a: pl.BlockSpec(a.shape, lambda i: (0,) * a.ndim)
    sds = jax.ShapeDtypeStruct
    return pl.pallas_call(
        functools.partial(_inproj_body, layer=layer),
        grid=(m // tm,),
        in_specs=[tok(d), full(g), full(wm), full(wf), full(bfp), full(lbp)],
        out_specs=[tok(w), tok(w), tok(w), tok(w), tok(w), tok(LANES),
                   pl.BlockSpec((FOX_HEADS, tm), lambda i: (0, i)),
                   tok(w), tok(w), tok(w), tok(w), tok(w)],
        out_shape=[sds((m, w), BF16), sds((m, w), BF16), sds((m, w), BF16),
                   sds((m, w), F32), sds((m, w), F32), sds((m, LANES), F32),
                   sds((FOX_HEADS, m), F32),
                   sds((m, w), F32), sds((m, w), F32), sds((m, w), F32),
                   sds((m, w), BF16), sds((m, w), F32)],
        compiler_params=_cparams("parallel"),
        name="inproj",
    )(x, g, wm, wf, bfp, lbp)


def _cumsum_body(lfr_ref, lft_ref, cr_ref, cc_ref):
    cr_ref[0] = _scan(lfr_ref[0], 1)
    cc_ref[0] = _scan(lft_ref[0], 0)


def _fox_cumsum(lf_row, lf_tm):
    b, h, l = lf_row.shape
    return pl.pallas_call(
        _cumsum_body,
        grid=(b,),
        in_specs=[pl.BlockSpec((1, h, l), lambda i: (i, 0, 0)),
                  pl.BlockSpec((1, l, LANES), lambda i: (i, 0, 0))],
        out_specs=[pl.BlockSpec((1, h, l), lambda i: (i, 0, 0)),
                   pl.BlockSpec((1, l, LANES), lambda i: (i, 0, 0))],
        out_shape=[jax.ShapeDtypeStruct((b, h, l), F32),
                   jax.ShapeDtypeStruct((b, l, LANES), F32)],
        compiler_params=_cparams("parallel"),
        name="fox_cumsum",
    )(lf_row, lf_tm)


def _fox_prompt_body(q_ref, k_ref, v_ref, cc_ref, cr_ref, o_ref, m_ref, l_ref, acc_ref, *, t):
    hp = pl.program_id(1)
    i = pl.program_id(2)
    lane = lax.broadcasted_iota(jnp.int32, (t, LANES), 1)
    row = lax.broadcasted_iota(jnp.int32, (t, t), 0)
    col = lax.broadcasted_iota(jnp.int32, (t, t), 1)
    qq = q_ref[0]
    cc = cc_ref[0]
    zero = jnp.zeros_like(qq)
    qm = [jnp.where(lane < FOX_HEAD_DIM, qq, zero), jnp.where(lane >= FOX_HEAD_DIM, qq, zero)]
    cq = [jnp.sum(jnp.where(lane == 2 * hp + h, cc, 0.0), axis=1, keepdims=True) for h in range(2)]

    m_ref[...] = jnp.full_like(m_ref, -jnp.inf)
    l_ref[...] = jnp.zeros_like(l_ref)
    acc_ref[...] = jnp.zeros_like(acc_ref)

    def block(j, diag):
        r0 = pl.multiple_of(j * t, t)
        kk = k_ref[0, pl.ds(r0, t), :]
        vv = v_ref[0, pl.ds(r0, t), :]
        for h in range(2):
            ck = cr_ref[0, j, pl.ds(2 * hp + h, 1), :]
            s = _dot_nt(qm[h], kk) + (cq[h] - ck)
            if diag:
                s = jnp.where(col <= row, s, -jnp.inf)
            m_prev = m_ref[h]
            m_next = jnp.maximum(m_prev, jnp.max(s, axis=1, keepdims=True))
            p = jnp.exp(s - pltpu.repeat(m_next, t // LANES, 1))
            alpha = jnp.exp(m_prev - m_next)
            l_ref[h] = alpha * l_ref[h] + jnp.sum(p, axis=1, keepdims=True)
            m_ref[h] = m_next
            acc_ref[h] = alpha * acc_ref[h] + _dot(p.astype(BF16), vv)

    def off_diag(j, carry):
        block(j, False)
        return carry

    lax.fori_loop(0, i, off_diag, 0)
    block(i, True)
    o0 = acc_ref[0] / l_ref[0]
    o1 = acc_ref[1] / l_ref[1]
    o_ref[0] = jnp.where(lane < FOX_HEAD_DIM, o0, o1).astype(BF16)


def _fox_prompt(q, k, v, c_col, c_rowb, *, t):
    b, l, w = q.shape
    nhp = w // LANES
    nk = l // t
    return pl.pallas_call(
        functools.partial(_fox_prompt_body, t=t),
        grid=(b, nhp, l // t),
        in_specs=[
            pl.BlockSpec((1, t, LANES), lambda bi, hp, i: (bi, i, hp)),
            pl.BlockSpec((1, l, LANES), lambda bi, hp, i: (bi, 0, hp)),
            pl.BlockSpec((1, l, LANES), lambda bi, hp, i: (bi, 0, hp)),
            pl.BlockSpec((1, t, LANES), lambda bi, hp, i: (bi, i, 0)),
            pl.BlockSpec((1, nk, FOX_HEADS, t), lambda bi, hp, i: (bi, 0, 0, 0)),
        ],
        out_specs=pl.BlockSpec((1, t, LANES), lambda bi, hp, i: (bi, i, hp)),
        out_shape=jax.ShapeDtypeStruct((b, l, w), BF16),
        scratch_shapes=[pltpu.VMEM((2, t, LANES), F32), pltpu.VMEM((2, t, LANES), F32),
                        pltpu.VMEM((2, t, LANES), F32)],
        compiler_params=_cparams("parallel", "parallel", "arbitrary"),
        name="fox_prompt",
    )(q, k, v, c_col, c_rowb)


def _fox_sample_body(q_ref, k_ref, v_ref, ck_ref, cv_ref, clf_ref, lfr_ref, lft_ref, o_ref):
    hp = pl.program_id(1)
    t = q_ref.shape[1]
    p_len = clf_ref.shape[3]
    lane = lax.broadcasted_iota(jnp.int32, (t, LANES), 1)
    row = lax.broadcasted_iota(jnp.int32, (t, t), 0)
    col = lax.broadcasted_iota(jnp.int32, (t, t), 1)
    qq = q_ref[0]
    kk = k_ref[0]
    vv = v_ref[0]
    zero = jnp.zeros_like(qq)
    c_cache = _scan(clf_ref[0, 0], 1)
    cn_row = _scan(lfr_ref[0, 0], 1)
    cn_col = _scan(lft_ref[0], 0)
    outs = []
    for h in range(2):
        lo = h * FOX_HEAD_DIM
        qm = jnp.where((lane >= lo) & (lane < lo + FOX_HEAD_DIM), qq, zero)
        q_h = qq[:, lo:lo + FOX_HEAD_DIM]
        kc = ck_ref[0, h].astype(BF16)
        vc = cv_ref[0, h].astype(BF16)
        c_h = c_cache[h:h + 1, :]
        tot = c_h[:, p_len - 1:p_len]
        cq = jnp.sum(jnp.where(lane == 2 * hp + h, cn_col, 0.0), axis=1, keepdims=True)
        s_c = _dot_nt(q_h, kc) + (cq + (tot - c_h))
        s_s = _dot_nt(qm, kk) + (cq - cn_row[h:h + 1, :t])
        s_s = jnp.where(col <= row, s_s, -jnp.inf)
        m = jnp.maximum(jnp.max(s_c, axis=1, keepdims=True), jnp.max(s_s, axis=1, keepdims=True))
        p_c = jnp.exp(s_c - m)
        p_s = jnp.exp(s_s - m)
        den = jnp.sum(p_c, axis=1, keepdims=True) + jnp.sum(p_s, axis=1, keepdims=True)
        o = _dot(p_c.astype(BF16), vc) + _dot(p_s.astype(BF16), vv)[:, lo:lo + FOX_HEAD_DIM]
        outs.append(o / den)
    o_ref[0] = jnp.concatenate(outs, axis=1).astype(BF16)


def _fox_sample(q, k, v, cache_k, cache_v, clf, lf_row, lf_tm):
    b, t, w = q.shape
    nhp = w // LANES
    p_len = cache_k.shape[2]
    hd = cache_k.shape[3]
    tokq = pl.BlockSpec((1, t, LANES), lambda bi, hp: (bi, 0, hp))
    cache = pl.BlockSpec((1, 2, p_len, hd), lambda bi, hp: (bi, hp, 0, 0))
    return pl.pallas_call(
        _fox_sample_body,
        grid=(b, nhp),
        in_specs=[tokq, tokq, tokq, cache, cache,
                  pl.BlockSpec((1, 1, 2, p_len), lambda bi, hp: (bi, hp, 0, 0)),
                  pl.BlockSpec((1, 1, 2, LANES), lambda bi, hp: (bi, hp, 0, 0)),
                  pl.BlockSpec((1, t, LANES), lambda bi, hp: (bi, 0, 0))],
        out_specs=tokq,
        out_shape=jax.ShapeDtypeStruct((b, t, w), BF16),
        compiler_params=_cparams("parallel", "parallel"),
        name="fox_sample",
    )(q, k, v, cache_k, cache_v, clf, lf_row, lf_tm)


def _hgrn_body(q_ref, lf_ref, k_ref, v_ref, s0_ref, o_ref, so_ref, st_ref, tmp_ref, *, c, n_chunks):
    ti = pl.program_id(2)

    @pl.when(ti == 0)
    def _():
        st_ref[...] = s0_ref[0, 0].T

    row = lax.broadcasted_iota(jnp.int32, (c, c), 0)
    col = lax.broadcasted_iota(jnp.int32, (c, c), 1)
    rowc = lax.broadcasted_iota(jnp.int32, (c, 1), 0)
    half = c // 2
    for ci in range(n_chunks):
        r0 = ci * c
        q = q_ref[0, r0:r0 + c, :]
        k = k_ref[0, r0:r0 + c, :]
        v = v_ref[0, r0:r0 + c, :]
        b = _scan(lf_ref[0, r0:r0 + c, :], 0)
        st = st_ref[...]
        o = _dot_nt((q * jnp.exp(b)).astype(BF16), st.astype(BF16))
        b_mid = b[half - 1:half, :]
        b_last = b[c - 1:c, :]
        span = jnp.maximum(jnp.max(b[0:1, :] - b_mid), jnp.max(b_mid - b_last))

        def factored(q=q, k=k, v=v, b=b, b_mid=b_mid):
            qt = (q * jnp.exp(b - b_mid)).astype(BF16)
            kt = (k * jnp.exp(b_mid - b)).astype(BF16)
            sc = jnp.where(col <= row, _dot_nt(qt, kt), 0.0)
            return _dot(sc.astype(BF16), v)

        def direct(q=q, k=k, v=v, b=b):
            tmp_ref[0] = b
            tmp_ref[1] = k
            tmp_ref[2] = v.astype(F32)

            def body(s, acc):
                bs = tmp_ref[0, pl.ds(s, 1), :]
                ks = tmp_ref[1, pl.ds(s, 1), :]
                vs = tmp_ref[2, pl.ds(s, 1), :]
                w = jnp.exp(jnp.minimum(b - bs, 0.0))
                a = jnp.sum(q * ks * w, axis=1, keepdims=True)
                return acc + jnp.where(rowc >= s, a, 0.0) * vs

            return lax.fori_loop(0, c, body, jnp.zeros((c, HGRN_HEAD_DIM), F32))

        o_ref[0, r0:r0 + c, :] = o + lax.cond(span <= HGRN_FACTOR_LIMIT, factored, direct)
        kh = (k * jnp.exp(b_last - b)).astype(BF16)
        st_ref[...] = st * jnp.exp(b_last) + _dot_tn(v, kh)

    @pl.when(ti == pl.num_programs(2) - 1)
    def _():
        so_ref[0, 0] = st_ref[...].T


def _hgrn(hq, hlf, hk, hv, s0, *, c, tl):
    b, l, w = hq.shape
    nh = w // HGRN_HEAD_DIM
    d = HGRN_HEAD_DIM
    tok = pl.BlockSpec((1, tl, d), lambda bi, h, ti: (bi, ti, h))
    st = pl.BlockSpec((1, 1, d, d), lambda bi, h, ti: (bi, h, 0, 0))
    return pl.pallas_call(
        functools.partial(_hgrn_body, c=c, n_chunks=tl // c),
        grid=(b, nh, l // tl),
        in_specs=[tok, tok, tok, tok, st],
        out_specs=[tok, st],
        out_shape=[jax.ShapeDtypeStruct((b, l, w), F32), jax.ShapeDtypeStruct((b, nh, d, d), F32)],
        scratch_shapes=[pltpu.VMEM((d, d), F32), pltpu.VMEM((3, c, d), F32)],
        compiler_params=_cparams("parallel", "parallel", "arbitrary"),
        name="hgrn",
    )(hq, hlf, hk, hv, s0)


def _outproj_body(x_ref, fo_ref, ho_ref, hg_ref, gn_ref, wf_ref, wh_ref, o_ref):
    acc = x_ref[...] + _dot(fo_ref[...], wf_ref[...])
    gn = gn_ref[...]
    d = HGRN_HEAD_DIM
    parts = []
    for h in range(HGRN_HEADS):
        ho = ho_ref[:, h * d:(h + 1) * d]
        g = hg_ref[:, h * d:(h + 1) * d]
        parts.append((_rms(ho, gn) * (g * jax.nn.sigmoid(g))).astype(BF16))
    o_ref[...] = acc + _dot(jnp.concatenate(parts, axis=1), wh_ref[...])


def _outproj(x, fo, ho, hg, gn, wf, wh, *, tm):
    m, d = x.shape
    tok = lambda width: pl.BlockSpec((tm, width), lambda i: (i, 0))
    full = lambda a: pl.BlockSpec(a.shape, lambda i: (0,) * a.ndim)
    return pl.pallas_call(
        _outproj_body,
        grid=(m // tm,),
        in_specs=[tok(d), tok(FOX_WIDTH), tok(HGRN_WIDTH), tok(HGRN_WIDTH), full(gn), full(wf), full(wh)],
        out_specs=tok(d),
        out_shape=jax.ShapeDtypeStruct((m, d), F32),
        compiler_params=_cparams("parallel"),
        name="outproj",
    )(x, fo, ho, hg, gn, wf, wh)


def _heads_first(a, b, l, heads):
    return a.reshape(b, l, heads, -1).transpose(0, 2, 1, 3)


def kernel(x_prompt, x_sample, cache_k, cache_v, cache_logf, state_hgrn, norm_ffn1, ffn1_wi, ffn1_wo,
           norm_mix, w_in, b_fgate, hgrn_lb, hgrn_gnorm, w_out, norm_ffn2, ffn2_wi, ffn2_wo, norm_final):
    bp, lp, d = x_prompt.shape
    bs, ls, _ = x_sample.shape
    mp, ms = bp * lp, bs * ls
    p_len = cache_k.shape[3]
    f, w, h8 = FOX_WIDTH, HGRN_WIDTH, FOX_HEADS
    tm_ffn, tf_ffn, tm_proj, t_attn, c_hgrn, tl_hgrn = 1536, 256, 512, 256, 64, 512

    x = jnp.concatenate([x_prompt.reshape(mp, d), x_sample.reshape(ms, d)], axis=0)
    gfin = norm_final.reshape(1, d)
    s0_prompt = jnp.zeros((bp, HGRN_HEADS, HGRN_HEAD_DIM, HGRN_HEAD_DIM), F32)

    outs = {n: [] for n in ("kp", "vp", "lfp", "sp", "ks", "vs", "lfs", "ss")}
    for l in range(DEPTH):
        wi1, wo1 = ffn1_wi[l].astype(BF16), ffn1_wo[l].astype(BF16)
        wi2, wo2 = ffn2_wi[l].astype(BF16), ffn2_wo[l].astype(BF16)
        wl = w_in[l]
        wm = jnp.concatenate([wl[:, :3 * f], wl[:, 3 * f + h8:]], axis=1).astype(BF16)
        wf = jnp.pad(wl[:, 3 * f:3 * f + h8], ((0, 0), (0, LANES - h8))).astype(BF16)
        bfp = jnp.pad(b_fgate[l].reshape(1, h8), ((0, 0), (0, LANES - h8)))
        wo_f, wo_h = w_out[l, :f].astype(BF16), w_out[l, f:].astype(BF16)

        x = _ffn(x, norm_ffn1[l].reshape(1, d), wi1, wo1, gfin, final_norm=False, tm=tm_ffn, tf=tf_ffn)
        (q, k, v, kf, vf, lft, lfr, hq, hlf, hk, hv, hg) = _inproj(
            x, norm_mix[l].reshape(1, d), wm, wf, bfp, hgrn_lb, layer=l, tm=tm_proj)

        lfr_p = lfr[:, :mp].reshape(h8, bp, lp).transpose(1, 0, 2)
        c_row, c_col = _fox_cumsum(lfr_p, lft[:mp].reshape(bp, lp, LANES))
        c_rowb = c_row.reshape(bp, h8, lp // t_attn, t_attn).transpose(0, 2, 1, 3)
        fo_p = _fox_prompt(q[:mp].reshape(bp, lp, f), k[:mp].reshape(bp, lp, f), v[:mp].reshape(bp, lp, f),
                           c_col, c_rowb, t=t_attn)
        lfr_s = lfr[:, mp:].reshape(h8, bs, ls).transpose(1, 0, 2)
        lfr_s_pad = jnp.pad(lfr_s, ((0, 0), (0, 0), (0, LANES - ls))).reshape(bs, h8 // 2, 2, LANES)
        fo_s = _fox_sample(q[mp:].reshape(bs, ls, f), k[mp:].reshape(bs, ls, f), v[mp:].reshape(bs, ls, f),
                           cache_k[l], cache_v[l], cache_logf[l].reshape(bs, h8 // 2, 2, p_len),
                           lfr_s_pad, lft[mp:].reshape(bs, ls, LANES))
        ho_p, st_p = _hgrn(hq[:mp].reshape(bp, lp, w), hlf[:mp].reshape(bp, lp, w), hk[:mp].reshape(bp, lp, w),
                           hv[:mp].reshape(bp, lp, w), s0_prompt, c=c_hgrn, tl=tl_hgrn)
        ho_s, st_s = _hgrn(hq[mp:].reshape(bs, ls, w), hlf[mp:].reshape(bs, ls, w), hk[mp:].reshape(bs, ls, w),
                           hv[mp:].reshape(bs, ls, w), state_hgrn[l], c=ls, tl=ls)

        fo = jnp.concatenate([fo_p.reshape(mp, f), fo_s.reshape(ms, f)], axis=0)
        ho = jnp.concatenate([ho_p.reshape(mp, w), ho_s.reshape(ms, w)], axis=0)
        x = _outproj(x, fo, ho, hg, hgrn_gnorm[l].reshape(1, HGRN_HEAD_DIM), wo_f, wo_h, tm=tm_proj)
        x = _ffn(x, norm_ffn2[l].reshape(1, d), wi2, wo2, gfin, final_norm=(l == DEPTH - 1), tm=tm_ffn, tf=tf_ffn)

        outs["kp"].append(_heads_first(kf[:mp], bp, lp, h8))
        outs["vp"].append(_heads_first(vf[:mp], bp, lp, h8))
        outs["lfp"].append(lfr_p)
        outs["sp"].append(st_p)
        outs["ks"].append(_heads_first(kf[mp:], bs, ls, h8))
        outs["vs"].append(_heads_first(vf[mp:], bs, ls, h8))
        outs["lfs"].append(lfr_s)
        outs["ss"].append(st_s)

    y_prompt = x[:mp].reshape(bp, lp, d)
    y_sample = x[mp:].reshape(bs, ls, d)
    st = lambda n: jnp.stack(outs[n])
    return (y_prompt, y_sample, st("kp"), st("vp"), st("lfp"), st("sp"), st("ks"), st("vs"), st("lfs"), st("ss"))
```

```python
import functools

import jax
import jax.numpy as jnp
from jax import lax
from jax.experimental import pallas as pl
from jax.experimental.pallas import tpu as pltpu

F32 = jnp.float32
BF16 = jnp.bfloat16

D_MODEL = 1024
DEPTH = 2
FOX_HEADS = 8
FOX_HEAD_DIM = 64
FOX_WIDTH = FOX_HEADS * FOX_HEAD_DIM
HGRN_HEADS = 4
HGRN_HEAD_DIM = 128
HGRN_WIDTH = HGRN_HEADS * HGRN_HEAD_DIM
D_FF = 2816
RMS_EPS = 1e-6
FOX_SCALE = FOX_HEAD_DIM ** -0.5

LANES = 128
VMEM_LIMIT_BYTES = 56 * 1024 * 1024
HGRN_FACTOR_LIMIT = 50.0
FOX_SKIP_GAP = 106.0
FOX_NORM_SLACK = 1.001


def _cparams(*sem):
    return pltpu.CompilerParams(dimension_semantics=sem, vmem_limit_bytes=VMEM_LIMIT_BYTES)


def _rms(x, g):
    ms = jnp.mean(x * x, axis=-1, keepdims=True)
    return x * lax.rsqrt(ms + RMS_EPS) * g


def _silu(x):
    return x * jax.nn.sigmoid(x)


def _log_sigmoid(z):
    return jnp.minimum(z, 0.0) - jnp.log1p(jnp.exp(-jnp.abs(z)))


def _dot(a, b):
    return jnp.dot(a, b, preferred_element_type=F32)


def _dot_nt(a, b):
    return lax.dot_general(a, b, (((1,), (1,)), ((), ())), preferred_element_type=F32)


def _dot_tn(a, b):
    return lax.dot_general(a, b, (((0,), (0,)), ((), ())), preferred_element_type=F32)


def _scan(x, axis, period=None):
    n = x.shape[axis] if period is None else period
    idx = lax.broadcasted_iota(jnp.int32, x.shape, axis)
    if period is not None:
        idx = idx & (period - 1)
    s = 1
    while s < n:
        x = x + jnp.where(idx >= s, pltpu.roll(x, s, axis), 0.0)
        s *= 2
    return x


def _drop_alias_refs(body, n_in, n_alias):
    def wrapped(*refs):
        return body(*refs[:n_in], *refs[n_in + n_alias:])
    return wrapped


_ANY = pl.BlockSpec(memory_space=pl.ANY)


def _ffn_body(x_ref, g_ref, wa_ref, wb_ref, wo_ref, gf_ref, o_ref, xn_ref, acc_ref, *, final_norm):
    j = pl.program_id(1)

    @pl.when(j == 0)
    def _():
        xn_ref[...] = _rms(x_ref[...], g_ref[0]).astype(BF16)
        acc_ref[...] = jnp.zeros_like(acc_ref)

    xn = xn_ref[...]
    a = _dot(xn, wa_ref[0].astype(BF16))
    b = _dot(xn, wb_ref[0].astype(BF16))
    acc_ref[...] += _dot((_silu(a) * b).astype(BF16), wo_ref[0].astype(BF16))

    @pl.when(j == pl.num_programs(1) - 1)
    def _():
        y = x_ref[...] + 0.5 * acc_ref[...]
        if final_norm:
            y = _rms(y, gf_ref[...])
        o_ref[...] = y


def _ffn(x, g, wi, wo, gf, *, layer, final_norm, tm, tf):
    m, d = x.shape
    f = wo.shape[1]
    nf = f // tf
    return pl.pallas_call(
        functools.partial(_ffn_body, final_norm=final_norm),
        grid=(m // tm, nf),
        in_specs=[
            pl.BlockSpec((tm, d), lambda i, j: (i, 0)),
            pl.BlockSpec((1, 1, d), lambda i, j: (layer, 0, 0)),
            pl.BlockSpec((1, d, tf), lambda i, j: (layer, 0, j)),
            pl.BlockSpec((1, d, tf), lambda i, j: (layer, 0, j + nf)),
            pl.BlockSpec((1, tf, d), lambda i, j: (layer, j, 0)),
            pl.BlockSpec((1, d), lambda i, j: (0, 0)),
        ],
        out_specs=pl.BlockSpec((tm, d), lambda i, j: (i, 0)),
        out_shape=jax.ShapeDtypeStruct((m, d), F32),
        scratch_shapes=[pltpu.VMEM((tm, d), BF16), pltpu.VMEM((tm, d), F32)],
        compiler_params=_cparams("parallel", "arbitrary"),
        name="ffn",
    )(x, g, wi, wi, wo, gf)


def _inproj_body(x_ref, g_ref, wm_ref, wf_ref, bf_ref, lbp_ref,
                 q_ref, k_ref, v_ref, kh_ref, vh_ref, lft_ref, lfr_ref,
                 hq_ref, hlf_ref, hk_ref, hv_ref, hg_ref, *, layer, sample):
    h = _rms(x_ref[...], g_ref[0]).astype(BF16)
    w, hd = FOX_WIDTH, FOX_HEAD_DIM

    def sec(i):
        return _dot(h, wm_ref[:, i * w:(i + 1) * w])

    def store_heads(ref, val):
        for hh in range(FOX_HEADS):
            piece = val[:, hh * hd:(hh + 1) * hd]
            if sample:
                ref[0, :, hh] = piece.reshape(ref.shape[1], ref.shape[3], hd)
            else:
                ref[0, 0, hh] = piece

    q = sec(0)
    q_ref[...] = (q * FOX_SCALE).astype(BF16)
    k = sec(1)
    store_heads(kh_ref, k)
    k_ref[...] = k.astype(BF16)
    v = sec(2)
    store_heads(vh_ref, v)
    v_ref[...] = v.astype(BF16)

    lf = _log_sigmoid(_dot(h, wf_ref[...]) + bf_ref[...])
    lft_ref[...] = lf
    lfr = lf.T[:FOX_HEADS, :]
    if sample:
        lfr_ref[...] = lfr
    else:
        lfr_ref[0, 0] = lfr

    hq_ref[...] = _silu(sec(3))

    p = lbp_ref[...]
    e = jnp.exp(p - jnp.max(p, axis=0, keepdims=True))
    sm = e / jnp.sum(e, axis=0, keepdims=True)
    cs = sm[0:1]
    for r in range(1, layer + 1):
        cs = cs + sm[r:r + 1]
    lb = cs - sm[0:1]

    z = sec(4)
    a = jnp.log(lb)
    b = jnp.log1p(-lb) + _log_sigmoid(z)
    hlf_ref[...] = jnp.maximum(a, b) + jnp.log1p(jnp.exp(-jnp.abs(a - b)))
    hk_ref[...] = (1.0 - lb) * jax.nn.sigmoid(-z)
    hv_ref[...] = sec(5).astype(BF16)
    hg_ref[...] = sec(6)


def _inproj(x, g, wm, wf, bfp, lbp, carried, *, layer, sample, batch, tm):
    m, d = x.shape
    w, hd, nh = FOX_WIDTH, FOX_HEAD_DIM, FOX_HEADS
    seq = m // batch
    tiles_per_seq = seq // tm if not sample else 1
    tok = lambda width: pl.BlockSpec((tm, width), lambda i: (i, 0))
    full = lambda a: pl.BlockSpec(a.shape, lambda i: (0,) * a.ndim)
    sds = jax.ShapeDtypeStruct
    if sample:
        heads_shape = (DEPTH, batch, nh, seq, hd)
        heads_spec = pl.BlockSpec((1, batch, nh, seq, hd), lambda i: (layer, 0, 0, 0, 0))
        lfr_shape, lfr_spec = (nh, m), pl.BlockSpec((nh, m), lambda i: (0, 0))
    else:
        heads_shape = (DEPTH, batch, nh, seq, hd)
        heads_spec = pl.BlockSpec((1, 1, nh, tm, hd),
                                  lambda i: (layer, i // tiles_per_seq, 0, i % tiles_per_seq, 0))
        lfr_shape = (DEPTH, batch, nh, seq)
        lfr_spec = pl.BlockSpec((1, 1, nh, tm), lambda i: (layer, i // tiles_per_seq, 0, i % tiles_per_seq))
    out_specs = [tok(w), tok(w), tok(w), heads_spec, heads_spec, tok(LANES), lfr_spec,
                 tok(w), tok(w), tok(w), tok(w), tok(w)]
    out_shape = [sds((m, w), BF16), sds((m, w), BF16), sds((m, w), BF16),
                 sds(heads_shape, F32), sds(heads_shape, F32), sds((m, LANES), F32), sds(lfr_shape, F32),
                 sds((m, w), F32), sds((m, w), F32), sds((m, w), F32), sds((m, w), BF16), sds((m, w), F32)]
    body = functools.partial(_inproj_body, layer=layer, sample=sample)
    args = [x, g, wm, wf, bfp, lbp]
    in_specs = [tok(d), pl.BlockSpec((1, 1, d), lambda i: (layer, 0, 0)), full(wm), full(wf), full(bfp), full(lbp)]
    aliases = {}
    if carried is not None:
        out_index = (3, 4, 6)
        for n, buf in enumerate(carried):
            aliases[len(args)] = out_index[n]
            args.append(buf)
            in_specs.append(_ANY)
        body = _drop_alias_refs(body, 6, len(carried))
    return pl.pallas_call(
        body, grid=(m // tm,), in_specs=in_specs, out_specs=out_specs, out_shape=out_shape,
        input_output_aliases=aliases, compiler_params=_cparams("arbitrary"), name="inproj",
    )(*args)


def _cumsum_body(lfr_ref, lft_ref, cr_ref, cc_ref):
    cr_ref[0] = _scan(lfr_ref[0, 0], 1)
    cc_ref[0] = _scan(lft_ref[0], 0)


def _fox_cumsum(lf_rows, lf_tm, *, layer):
    _, b, h, l = lf_rows.shape
    return pl.pallas_call(
        _cumsum_body,
        grid=(b,),
        in_specs=[pl.BlockSpec((1, 1, h, l), lambda i: (layer, i, 0, 0)),
                  pl.BlockSpec((1, l, LANES), lambda i: (i, 0, 0))],
        out_specs=[pl.BlockSpec((1, h, l), lambda i: (i, 0, 0)),
                   pl.BlockSpec((1, l, LANES), lambda i: (i, 0, 0))],
        out_shape=[jax.ShapeDtypeStruct((b, h, l), F32),
                   jax.ShapeDtypeStruct((b, l, LANES), F32)],
        compiler_params=_cparams("parallel"),
        name="fox_cumsum",
    )(lf_rows, lf_tm)


def _fox_prompt_body(q_ref, k_ref, v_ref, cc_ref, cr_ref, o_ref, m_ref, l_ref, acc_ref, kn_ref, *, t):
    hp = pl.program_id(1)
    i = pl.program_id(2)
    nk = cr_ref.shape[2]
    hd = FOX_HEAD_DIM
    lane = lax.broadcasted_iota(jnp.int32, (t, LANES), 1)
    row = lax.broadcasted_iota(jnp.int32, (t, t), 0)
    col = lax.broadcasted_iota(jnp.int32, (t, t), 1)
    head_lanes = [lane < hd, lane >= hd]

    @pl.when(i == 0)
    def _():
        step = 1024
        lane_s = lax.broadcasted_iota(jnp.int32, (step, LANES), 1)

        def chunk(ci, mx):
            kf = k_ref[0, pl.ds(pl.multiple_of(ci * step, step), step), :].astype(F32)
            k2 = kf * kf
            n0 = jnp.max(jnp.sum(jnp.where(lane_s < hd, k2, 0.0), axis=1, keepdims=True))
            n1 = jnp.max(jnp.sum(jnp.where(lane_s >= hd, k2, 0.0), axis=1, keepdims=True))
            return jnp.maximum(mx[0], n0), jnp.maximum(mx[1], n1)

        n0, n1 = lax.fori_loop(0, k_ref.shape[1] // step, chunk, (jnp.float32(0.0), jnp.float32(0.0)))
        kn_ref[0] = jnp.sqrt(n0)
        kn_ref[1] = jnp.sqrt(n1)

    qq = q_ref[0]
    cc = cc_ref[0]
    zero = jnp.zeros_like(qq)
    qm = [jnp.where(head_lanes[h], qq, zero) for h in range(2)]
    cq = [jnp.sum(jnp.where(lane == 2 * hp + h, cc, 0.0), axis=1, keepdims=True) for h in range(2)]
    qf = qq.astype(F32)
    q2 = qf * qf
    qn = [jnp.sqrt(jnp.sum(jnp.where(head_lanes[h], q2, 0.0), axis=1, keepdims=True)) for h in range(2)]

    m_ref[...] = jnp.full_like(m_ref, -jnp.inf)
    l_ref[...] = jnp.zeros_like(l_ref)
    acc_ref[...] = jnp.zeros_like(acc_ref)

    def block(j, diag):
        r0 = pl.multiple_of(j * t, t)
        kk = k_ref[0, pl.ds(r0, t), :]
        vv = v_ref[0, pl.ds(r0, t), :]
        for h in range(2):
            ck = cr_ref[0, h, pl.ds(j, 1), :]
            s = _dot_nt(qm[h], kk) + (cq[h] - ck)
            if diag:
                s = jnp.where(col <= row, s, -jnp.inf)
            m_prev = m_ref[h]
            m_next = jnp.maximum(m_prev, jnp.max(s, axis=1, keepdims=True))
            p = jnp.exp(s - pltpu.repeat(m_next, t // LANES, 1))
            alpha = jnp.exp(m_prev - m_next)
            l_ref[h] = alpha * l_ref[h] + jnp.sum(p, axis=1, keepdims=True)
            m_ref[h] = m_next
            acc_ref[h] = alpha * acc_ref[h] + _dot(p.astype(BF16), vv)

    block(i, True)

    jidx = lax.broadcasted_iota(jnp.int32, (nk, 1), 0)
    j_min = i
    for h in range(2):
        reach = jnp.max(qn[h] * (kn_ref[h] * FOX_NORM_SLACK) + cq[h] - m_ref[h][:, :1])
        c_min = jnp.min(cr_ref[0, h], axis=1, keepdims=True)
        needed = (reach - c_min >= -FOX_SKIP_GAP) & (jidx < i)
        j_min = jnp.minimum(j_min, jnp.min(jnp.where(needed, jidx, i)))

    def earlier(n, carry):
        block(i - 1 - n, False)
        return carry

    lax.fori_loop(0, i - j_min, earlier, 0)
    o0 = acc_ref[0] / l_ref[0]
    o1 = acc_ref[1] / l_ref[1]
    o_ref[0] = jnp.where(head_lanes[0], o0, o1).astype(BF16)


def _fox_prompt(q, k, v, c_col, c_rows, *, t):
    b, l, w = q.shape
    nhp = w // LANES
    nk = l // t
    return pl.pallas_call(
        functools.partial(_fox_prompt_body, t=t),
        grid=(b, nhp, l // t),
        in_specs=[
            pl.BlockSpec((1, t, LANES), lambda bi, hp, i: (bi, i, hp)),
            pl.BlockSpec((1, l, LANES), lambda bi, hp, i: (bi, 0, hp)),
            pl.BlockSpec((1, l, LANES), lambda bi, hp, i: (bi, 0, hp)),
            pl.BlockSpec((1, t, LANES), lambda bi, hp, i: (bi, i, 0)),
            pl.BlockSpec((1, 2, nk, t), lambda bi, hp, i: (bi, hp, 0, 0)),
        ],
        out_specs=pl.BlockSpec((1, t, LANES), lambda bi, hp, i: (bi, i, hp)),
        out_shape=jax.ShapeDtypeStruct((b, l, w), BF16),
        scratch_shapes=[pltpu.VMEM((2, t, LANES), F32), pltpu.VMEM((2, t, LANES), F32),
                        pltpu.VMEM((2, t, LANES), F32), pltpu.SMEM((2,), F32)],
        compiler_params=_cparams("parallel", "parallel", "arbitrary"),
        name="fox_prompt",
    )(q, k, v, c_col, c_rows)


def _fox_sample_body(q_ref, k_ref, v_ref, ck_ref, cv_ref, clf_ref, lfr_ref, lft_ref, o_ref):
    hp = pl.program_id(1)
    t = q_ref.shape[1]
    p_len = clf_ref.shape[4]
    hd = FOX_HEAD_DIM
    lane = lax.broadcasted_iota(jnp.int32, (t, LANES), 1)
    row = lax.broadcasted_iota(jnp.int32, (t, t), 0)
    col = lax.broadcasted_iota(jnp.int32, (t, t), 1)
    qq = q_ref[0]
    kk = k_ref[0]
    vv = v_ref[0]
    zero = jnp.zeros_like(qq)
    c_cache = _scan(clf_ref[0, 0, 0], 1)
    cn_row = _scan(lfr_ref[0, 0], 1)
    cn_col = _scan(lft_ref[0], 0)
    outs = []
    for h in range(2):
        lo = h * hd
        qm = jnp.where((lane >= lo) & (lane < lo + hd), qq, zero)
        q_h = qq[:, lo:lo + hd]
        kc = ck_ref[0, 0, h].astype(BF16)
        vc = cv_ref[0, 0, h].astype(BF16)
        c_h = c_cache[h:h + 1, :]
        tot = c_h[:, p_len - 1:p_len]
        cq = jnp.sum(jnp.where(lane == 2 * hp + h, cn_col, 0.0), axis=1, keepdims=True)
        s_c = _dot_nt(q_h, kc) + (cq + (tot - c_h))
        s_s = _dot_nt(qm, kk) + (cq - cn_row[h:h + 1, :t])
        s_s = jnp.where(col <= row, s_s, -jnp.inf)
        m = jnp.maximum(jnp.max(s_c, axis=1, keepdims=True), jnp.max(s_s, axis=1, keepdims=True))
        p_c = jnp.exp(s_c - m)
        p_s = jnp.exp(s_s - m)
        den = jnp.sum(p_c, axis=1, keepdims=True) + jnp.sum(p_s, axis=1, keepdims=True)
        o = _dot(p_c.astype(BF16), vc) + _dot(p_s.astype(BF16), vv)[:, lo:lo + hd]
        outs.append(o / den)
    o_ref[0] = jnp.concatenate(outs, axis=1).astype(BF16)


def _fox_sample(q, k, v, cache_k, cache_v, clf, lf_row, lf_tm, *, layer):
    b, t, w = q.shape
    nhp = w // LANES
    p_len, hd = cache_k.shape[3], cache_k.shape[4]
    tokq = pl.BlockSpec((1, t, LANES), lambda bi, hp: (bi, 0, hp))
    cache = pl.BlockSpec((1, 1, 2, p_len, hd), lambda bi, hp: (layer, bi, hp, 0, 0))
    return pl.pallas_call(
        _fox_sample_body,
        grid=(b, nhp),
        in_specs=[tokq, tokq, tokq, cache, cache,
                  pl.BlockSpec((1, 1, 1, 2, p_len), lambda bi, hp: (layer, bi, hp, 0, 0)),
                  pl.BlockSpec((1, 1, 2, LANES), lambda bi, hp: (bi, hp, 0, 0)),
                  pl.BlockSpec((1, t, LANES), lambda bi, hp: (bi, 0, 0))],
        out_specs=tokq,
        out_shape=jax.ShapeDtypeStruct((b, t, w), BF16),
        compiler_params=_cparams("parallel", "parallel"),
        name="fox_sample",
    )(q, k, v, cache_k, cache_v, clf, lf_row, lf_tm)


def _hgrn_body(q_ref, lf_ref, k_ref, v_ref, s0_ref, o_ref, so_ref, st_ref, tmp_ref, *, c, n_chunks, zero_init):
    ti = pl.program_id(2)
    d = HGRN_HEAD_DIM

    @pl.when(ti == 0)
    def _():
        st_ref[...] = jnp.zeros((d, d), F32) if zero_init else s0_ref[0, 0, 0].T

    row = lax.broadcasted_iota(jnp.int32, (c, c), 0)
    col = lax.broadcasted_iota(jnp.int32, (c, c), 1)
    rowc = lax.broadcasted_iota(jnp.int32, (c, 1), 0)
    half = c // 2
    b_all = _scan(lf_ref[0], 0, period=c)

    span = jnp.zeros((1, d), F32)
    for ci in range(n_chunks):
        b = b_all[ci * c:(ci + 1) * c]
        b_mid = b[half - 1:half]
        span = jnp.maximum(span, jnp.maximum(b[0:1] - b_mid, b_mid - b[c - 1:c]))
    factorable = jnp.max(span) <= HGRN_FACTOR_LIMIT

    def carry_state(st, k, v, b):
        b_last = b[c - 1:c]
        kh = (k * jnp.exp(b_last - b)).astype(BF16)
        return st * jnp.exp(b_last) + _dot_tn(v, kh)

    @pl.when(factorable)
    def _():
        st = st_ref[...]
        for ci in range(n_chunks):
            r0 = ci * c
            q = q_ref[0, r0:r0 + c, :]
            k = k_ref[0, r0:r0 + c, :]
            v = v_ref[0, r0:r0 + c, :]
            b = b_all[r0:r0 + c]
            b_mid = b[half - 1:half]
            o = _dot_nt((q * jnp.exp(b)).astype(BF16), st.astype(BF16))
            qt = (q * jnp.exp(b - b_mid)).astype(BF16)
            kt = (k * jnp.exp(b_mid - b)).astype(BF16)
            sc = jnp.where(col <= row, _dot_nt(qt, kt), 0.0)
            o_ref[0, r0:r0 + c, :] = o + _dot(sc.astype(BF16), v)
            st = carry_state(st, k, v, b)
        st_ref[...] = st

    @pl.when(jnp.logical_not(factorable))
    def _():
        def chunk(ci, carry):
            r0 = pl.multiple_of(ci * c, c)
            q = q_ref[0, pl.ds(r0, c), :]
            k = k_ref[0, pl.ds(r0, c), :]
            v = v_ref[0, pl.ds(r0, c), :]
            b = _scan(lf_ref[0, pl.ds(r0, c), :], 0)
            st = st_ref[...]
            tmp_ref[0] = b
            tmp_ref[1] = k
            tmp_ref[2] = v.astype(F32)

            def key_row(s, acc):
                bs = tmp_ref[0, pl.ds(s, 1), :]
                ks = tmp_ref[1, pl.ds(s, 1), :]
                vs = tmp_ref[2, pl.ds(s, 1), :]
                w = jnp.exp(jnp.minimum(b - bs, 0.0))
                a = jnp.sum(q * ks * w, axis=1, keepdims=True)
                return acc + jnp.where(rowc >= s, a, 0.0) * vs

            o = _dot_nt((q * jnp.exp(b)).astype(BF16), st.astype(BF16))
            o_ref[0, pl.ds(r0, c), :] = lax.fori_loop(0, c, key_row, o)
            st_ref[...] = carry_state(st, k, v, b)
            return carry

        lax.fori_loop(0, n_chunks, chunk, 0)

    @pl.when(ti == pl.num_programs(2) - 1)
    def _():
        so_ref[0, 0, 0] = st_ref[...].T


def _hgrn(hq, hlf, hk, hv, s0, carried, *, layer, c, tl):
    b, l, w = hq.shape
    d = HGRN_HEAD_DIM
    nh = w // d
    tok = pl.BlockSpec((1, tl, d), lambda bi, h, ti: (bi, ti, h))
    st = pl.BlockSpec((1, 1, 1, d, d), lambda bi, h, ti: (layer, bi, h, 0, 0))
    zero_init = s0 is None
    body = functools.partial(_hgrn_body, c=c, n_chunks=tl // c, zero_init=zero_init)
    args, in_specs = [hq, hlf, hk, hv], [tok, tok, tok, tok]
    if zero_init:
        core = body
        body = lambda q, lf, k, v, *rest: core(q, lf, k, v, None, *rest)
    else:
        args.append(s0)
        in_specs.append(st)
    aliases = {}
    if carried is not None:
        aliases[len(args)] = 1
        body = _drop_alias_refs(body, len(args), 1)
        args.append(carried)
        in_specs.append(_ANY)
    return pl.pallas_call(
        body,
        grid=(b, nh, l // tl),
        in_specs=in_specs,
        out_specs=[tok, st],
        out_shape=[jax.ShapeDtypeStruct((b, l, w), F32), jax.ShapeDtypeStruct((DEPTH, b, nh, d, d), F32)],
        scratch_shapes=[pltpu.VMEM((d, d), F32), pltpu.VMEM((3, c, d), F32)],
        input_output_aliases=aliases,
        compiler_params=_cparams("parallel", "parallel", "arbitrary"),
        name="hgrn",
    )(*args)


def _outproj_body(x_ref, fo_ref, ho_ref, hg_ref, gn_ref, w_ref, o_ref, wb_ref):
    @pl.when(pl.program_id(0) == 0)
    def _():
        wb_ref[...] = w_ref[0].astype(BF16)

    gn = gn_ref[0]
    d = HGRN_HEAD_DIM
    parts = [fo_ref[...]]
    for h in range(HGRN_HEADS):
        ho = ho_ref[:, h * d:(h + 1) * d]
        parts.append((_rms(ho, gn) * _silu(hg_ref[:, h * d:(h + 1) * d])).astype(BF16))
    o_ref[...] = x_ref[...] + _dot(jnp.concatenate(parts, axis=1), wb_ref[...])


def _outproj(x, fo, ho, hg, gn, w_out, *, layer, tm):
    m, d = x.shape
    dm = w_out.shape[1]
    tok = lambda width: pl.BlockSpec((tm, width), lambda i: (i, 0))
    return pl.pallas_call(
        _outproj_body,
        grid=(m // tm,),
        in_specs=[tok(d), tok(FOX_WIDTH), tok(HGRN_WIDTH), tok(HGRN_WIDTH),
                  pl.BlockSpec((1, 1, HGRN_HEAD_DIM), lambda i: (layer, 0, 0)),
                  pl.BlockSpec((1, dm, d), lambda i: (layer, 0, 0))],
        out_specs=tok(d),
        out_shape=jax.ShapeDtypeStruct((m, d), F32),
        scratch_shapes=[pltpu.VMEM((dm, d), BF16)],
        compiler_params=_cparams("arbitrary"),
        name="outproj",
    )(x, fo, ho, hg, gn, w_out)


def kernel(x_prompt, x_sample, cache_k, cache_v, cache_logf, state_hgrn, norm_ffn1, ffn1_wi, ffn1_wo,
           norm_mix, w_in, b_fgate, hgrn_lb, hgrn_gnorm, w_out, norm_ffn2, ffn2_wi, ffn2_wo, norm_final):
    bp, lp, d = x_prompt.shape
    bs, ls, _ = x_sample.shape
    mp, ms = bp * lp, bs * ls
    p_len = cache_k.shape[3]
    f, w, h8 = FOX_WIDTH, HGRN_WIDTH, FOX_HEADS
    tm_ffn, tf_ffn, tm_proj, t_attn, c_hgrn, tl_hgrn = 1024, 256, 512, 256, 64, 512

    xp = x_prompt.reshape(mp, d)
    xs = x_sample.reshape(ms, d)
    gfin = norm_final.reshape(1, d)
    g1, g2, gm = (a.reshape(DEPTH, 1, d) for a in (norm_ffn1, norm_ffn2, norm_mix))
    gn = hgrn_gnorm.reshape(DEPTH, 1, HGRN_HEAD_DIM)
    clf = cache_logf.reshape(DEPTH, bs, h8 // 2, 2, p_len)

    carry_p = carry_s = None
    st_p = st_s = None
    lfs = []
    for l in range(DEPTH):
        last = l == DEPTH - 1
        wl = w_in[l]
        wm = jnp.concatenate([wl[:, :3 * f], wl[:, 3 * f + h8:]], axis=1).astype(BF16)
        wf = jnp.pad(wl[:, 3 * f:3 * f + h8], ((0, 0), (0, LANES - h8))).astype(BF16)
        bfp = jnp.pad(b_fgate[l].reshape(1, h8), ((0, 0), (0, LANES - h8)))

        xp = _ffn(xp, g1, ffn1_wi, ffn1_wo, gfin, layer=l, final_norm=False, tm=tm_ffn, tf=tf_ffn)
        xs = _ffn(xs, g1, ffn1_wi, ffn1_wo, gfin, layer=l, final_norm=False, tm=ms, tf=tf_ffn)

        (q, k, v, kh, vh, lft, lfr, hq, hlf, hk, hv, hg) = _inproj(
            xp, gm, wm, wf, bfp, hgrn_lb, carry_p, layer=l, sample=False, batch=bp, tm=tm_proj)
        carry_p = (kh, vh, lfr)
        (q_s, k_s, v_s, kh_s, vh_s, lft_s, lfr_s, hq_s, hlf_s, hk_s, hv_s, hg_s) = _inproj(
            xs, gm, wm, wf, bfp, hgrn_lb, carry_s, layer=l, sample=True, batch=bs, tm=ms)
        carry_s = (kh_s, vh_s)

        c_row, c_col = _fox_cumsum(lfr, lft.reshape(bp, lp, LANES), layer=l)
        fo = _fox_prompt(q.reshape(bp, lp, f), k.reshape(bp, lp, f), v.reshape(bp, lp, f),
                         c_col, c_row.reshape(bp, h8, lp // t_attn, t_attn), t=t_attn)
        lfs_l = lfr_s.reshape(h8, bs, ls).transpose(1, 0, 2)
        lfs.append(lfs_l)
        lfs_pad = jnp.pad(lfs_l, ((0, 0), (0, 0), (0, LANES - ls))).reshape(bs, h8 // 2, 2, LANES)
        fo_s = _fox_sample(q_s.reshape(bs, ls, f), k_s.reshape(bs, ls, f), v_s.reshape(bs, ls, f),
                           cache_k, cache_v, clf, lfs_pad, lft_s.reshape(bs, ls, LANES), layer=l)
        to3 = lambda a, b_, l_: a.reshape(b_, l_, w)
        ho, st_p = _hgrn(to3(hq, bp, lp), to3(hlf, bp, lp), to3(hk, bp, lp), to3(hv, bp, lp), None, st_p,
                         layer=l, c=c_hgrn, tl=tl_hgrn)
        ho_s, st_s = _hgrn(to3(hq_s, bs, ls), to3(hlf_s, bs, ls), to3(hk_s, bs, ls), to3(hv_s, bs, ls),
                           state_hgrn, st_s, layer=l, c=ls, tl=ls)

        xp = _outproj(xp, fo.reshape(mp, f), ho.reshape(mp, w), hg, gn, w_out, layer=l, tm=tm_proj)
        xs = _outproj(xs, fo_s.reshape(ms, f), ho_s.reshape(ms, w), hg_s, gn, w_out, layer=l, tm=ms)
        xp = _ffn(xp, g2, ffn2_wi, ffn2_wo, gfin, layer=l, final_norm=last, tm=tm_ffn, tf=tf_ffn)
        xs = _ffn(xs, g2, ffn2_wi, ffn2_wo, gfin, layer=l, final_norm=last, tm=ms, tf=tf_ffn)

    return (xp.reshape(bp, lp, d), xs.reshape(bs, ls, d), carry_p[0], carry_p[1], carry_p[2], st_p,
            carry_s[0], carry_s[1], jnp.stack(lfs), st_s)
```

```python
import functools

import jax
import jax.numpy as jnp
from jax import lax
from jax.experimental import pallas as pl
from jax.experimental.pallas import tpu as pltpu

F32 = jnp.float32
BF16 = jnp.bfloat16

D_MODEL = 1024
DEPTH = 2
FOX_HEADS = 8
FOX_HEAD_DIM = 64
FOX_WIDTH = FOX_HEADS * FOX_HEAD_DIM
HGRN_HEADS = 4
HGRN_HEAD_DIM = 128
HGRN_WIDTH = HGRN_HEADS * HGRN_HEAD_DIM
D_FF = 2816
RMS_EPS = 1e-6
FOX_SCALE = FOX_HEAD_DIM ** -0.5

LANES = 128
VMEM_LIMIT_BYTES = 56 * 1024 * 1024
HGRN_FACTOR_LIMIT = 50.0
FOX_SKIP_GAP = 106.0
FOX_NORM_SLACK = 1.001
FOX_FIRST_ROUND_BLOCKS = 3


def _cparams(*sem):
    return pltpu.CompilerParams(dimension_semantics=sem, vmem_limit_bytes=VMEM_LIMIT_BYTES)


def _rms(x, g):
    ms = jnp.mean(x * x, axis=-1, keepdims=True)
    return x * lax.rsqrt(ms + RMS_EPS) * g


def _silu(x):
    return x * jax.nn.sigmoid(x)


def _log_sigmoid(z):
    return jnp.minimum(z, 0.0) - jnp.log1p(jnp.exp(-jnp.abs(z)))


def _dot(a, b):
    return jnp.dot(a, b, preferred_element_type=F32)


def _dot_nt(a, b):
    return lax.dot_general(a, b, (((1,), (1,)), ((), ())), preferred_element_type=F32)


def _dot_tn(a, b):
    return lax.dot_general(a, b, (((0,), (0,)), ((), ())), preferred_element_type=F32)


def _scan(x, axis, period=None):
    n = x.shape[axis] if period is None else period
    idx = lax.broadcasted_iota(jnp.int32, x.shape, axis)
    if period is not None:
        idx = idx & (period - 1)
    s = 1
    while s < n:
        x = x + jnp.where(idx >= s, pltpu.roll(x, s, axis), 0.0)
        s *= 2
    return x


def _drop_alias_refs(body, n_in, n_alias):
    def wrapped(*refs):
        return body(*refs[:n_in], *refs[n_in + n_alias:])
    return wrapped


_ANY = pl.BlockSpec(memory_space=pl.ANY)


def _ffn_body(x_ref, g_ref, wa_ref, wb_ref, wo_ref, gf_ref, o_ref, xn_ref, acc_ref, *, final_norm):
    j = pl.program_id(1)

    @pl.when(j == 0)
    def _():
        xn_ref[...] = _rms(x_ref[...], g_ref[0]).astype(BF16)
        acc_ref[...] = jnp.zeros_like(acc_ref)

    xn = xn_ref[...]
    a = _dot(xn, wa_ref[0].astype(BF16))
    b = _dot(xn, wb_ref[0].astype(BF16))
    acc_ref[...] += _dot((_silu(a) * b).astype(BF16), wo_ref[0].astype(BF16))

    @pl.when(j == pl.num_programs(1) - 1)
    def _():
        y = x_ref[...] + 0.5 * acc_ref[...]
        if final_norm:
            y = _rms(y, gf_ref[...])
        o_ref[...] = y


def _ffn(x, g, wi, wo, gf, *, layer, final_norm, tm, tf):
    m, d = x.shape
    f = wo.shape[1]
    nf = f // tf
    return pl.pallas_call(
        functools.partial(_ffn_body, final_norm=final_norm),
        grid=(m // tm, nf),
        in_specs=[
            pl.BlockSpec((tm, d), lambda i, j: (i, 0)),
            pl.BlockSpec((1, 1, d), lambda i, j: (layer, 0, 0)),
            pl.BlockSpec((1, d, tf), lambda i, j: (layer, 0, j)),
            pl.BlockSpec((1, d, tf), lambda i, j: (layer, 0, j + nf)),
            pl.BlockSpec((1, tf, d), lambda i, j: (layer, j, 0)),
            pl.BlockSpec((1, d), lambda i, j: (0, 0)),
        ],
        out_specs=pl.BlockSpec((tm, d), lambda i, j: (i, 0)),
        out_shape=jax.ShapeDtypeStruct((m, d), F32),
        scratch_shapes=[pltpu.VMEM((tm, d), BF16), pltpu.VMEM((tm, d), F32)],
        compiler_params=_cparams("parallel", "arbitrary"),
        name="ffn",
    )(x, g, wi, wi, wo, gf)


def _inproj_body(x_ref, g_ref, wm_ref, wf_ref, bf_ref, lbp_ref,
                 q_ref, k_ref, v_ref, kh_ref, vh_ref, lft_ref, lfr_ref,
                 hq_ref, hlf_ref, hk_ref, hv_ref, hg_ref, *, layer, sample):
    h = _rms(x_ref[...], g_ref[0]).astype(BF16)
    w, hd = FOX_WIDTH, FOX_HEAD_DIM

    def sec(i):
        return _dot(h, wm_ref[:, i * w:(i + 1) * w])

    def store_heads(ref, val):
        if sample:
            for hh in range(FOX_HEADS):
                piece = val[:, hh * hd:(hh + 1) * hd]
                ref[0, :, hh] = piece.reshape(ref.shape[1], ref.shape[3], hd)
        else:
            ref[0, 0] = val.T.reshape(FOX_HEADS, hd, val.shape[0])

    q = sec(0)
    q_ref[...] = (q * FOX_SCALE).astype(BF16)
    k = sec(1)
    store_heads(kh_ref, k)
    k_ref[...] = k.astype(BF16)
    v = sec(2)
    store_heads(vh_ref, v)
    v_ref[...] = v.astype(BF16)

    lf = _log_sigmoid(_dot(h, wf_ref[...]) + bf_ref[...])
    lft_ref[...] = lf
    lfr = lf.T[:FOX_HEADS, :]
    if sample:
        lfr_ref[...] = lfr
    else:
        lfr_ref[0, 0] = lfr

    hq_ref[...] = _silu(sec(3))

    p = lbp_ref[...]
    e = jnp.exp(p - jnp.max(p, axis=0, keepdims=True))
    sm = e / jnp.sum(e, axis=0, keepdims=True)
    cs = sm[0:1]
    for r in range(1, layer + 1):
        cs = cs + sm[r:r + 1]
    lb = cs - sm[0:1]

    z = sec(4)
    a = jnp.log(lb)
    b = jnp.log1p(-lb) + _log_sigmoid(z)
    hlf_ref[...] = jnp.maximum(a, b) + jnp.log1p(jnp.exp(-jnp.abs(a - b)))
    hk_ref[...] = (1.0 - lb) * jax.nn.sigmoid(-z)
    hv_ref[...] = sec(5).astype(BF16)
    hg_ref[...] = sec(6)


def _inproj(x, g, wm, wf, bfp, lbp, carried, *, layer, sample, batch, tm):
    m, d = x.shape
    w, hd, nh = FOX_WIDTH, FOX_HEAD_DIM, FOX_HEADS
    seq = m // batch
    tiles_per_seq = seq // tm if not sample else 1
    tok = lambda width: pl.BlockSpec((tm, width), lambda i: (i, 0))
    full = lambda a: pl.BlockSpec(a.shape, lambda i: (0,) * a.ndim)
    sds = jax.ShapeDtypeStruct
    if sample:
        heads_shape = (DEPTH, batch, nh, seq, hd)
        heads_spec = pl.BlockSpec((1, batch, nh, seq, hd), lambda i: (layer, 0, 0, 0, 0))
        lfr_shape, lfr_spec = (nh, m), pl.BlockSpec((nh, m), lambda i: (0, 0))
    else:
        heads_shape = (DEPTH, batch, nh, hd, seq)
        heads_spec = pl.BlockSpec((1, 1, nh, hd, tm),
                                  lambda i: (layer, i // tiles_per_seq, 0, 0, i % tiles_per_seq))
        lfr_shape = (DEPTH, batch, nh, seq)
        lfr_spec = pl.BlockSpec((1, 1, nh, tm), lambda i: (layer, i // tiles_per_seq, 0, i % tiles_per_seq))
    out_specs = [tok(w), tok(w), tok(w), heads_spec, heads_spec, tok(LANES), lfr_spec,
                 tok(w), tok(w), tok(w), tok(w), tok(w)]
    out_shape = [sds((m, w), BF16), sds((m, w), BF16), sds((m, w), BF16),
                 sds(heads_shape, F32), sds(heads_shape, F32), sds((m, LANES), F32), sds(lfr_shape, F32),
                 sds((m, w), F32), sds((m, w), F32), sds((m, w), F32), sds((m, w), BF16), sds((m, w), F32)]
    body = functools.partial(_inproj_body, layer=layer, sample=sample)
    args = [x, g, wm, wf, bfp, lbp]
    in_specs = [tok(d), pl.BlockSpec((1, 1, d), lambda i: (layer, 0, 0)), full(wm), full(wf), full(bfp), full(lbp)]
    aliases = {}
    if carried is not None:
        out_index = (3, 4, 6)
        for n, buf in enumerate(carried):
            aliases[len(args)] = out_index[n]
            args.append(buf)
            in_specs.append(_ANY)
        body = _drop_alias_refs(body, 6, len(carried))
    return pl.pallas_call(
        body, grid=(m // tm,), in_specs=in_specs, out_specs=out_specs, out_shape=out_shape,
        input_output_aliases=aliases, compiler_params=_cparams("arbitrary"), name="inproj",
    )(*args)


def _cumsum_body(lfr_ref, lft_ref, cr_ref, cc_ref):
    cr_ref[0] = _scan(lfr_ref[0, 0], 1)
    cc_ref[0] = _scan(lft_ref[0], 0)


def _fox_cumsum(lf_rows, lf_tm, *, layer):
    _, b, h, l = lf_rows.shape
    return pl.pallas_call(
        _cumsum_body,
        grid=(b,),
        in_specs=[pl.BlockSpec((1, 1, h, l), lambda i: (layer, i, 0, 0)),
                  pl.BlockSpec((1, l, LANES), lambda i: (i, 0, 0))],
        out_specs=[pl.BlockSpec((1, h, l), lambda i: (i, 0, 0)),
                   pl.BlockSpec((1, l, LANES), lambda i: (i, 0, 0))],
        out_shape=[jax.ShapeDtypeStruct((b, h, l), F32),
                   jax.ShapeDtypeStruct((b, l, LANES), F32)],
        compiler_params=_cparams("parallel"),
        name="fox_cumsum",
    )(lf_rows, lf_tm)


def _fox_prompt_body(q_ref, k_ref, v_ref, cc_ref, cr_ref, o_ref, m_ref, l_ref, acc_ref, kn_ref, *, t):
    hp = pl.program_id(1)
    i = pl.program_id(2)
    nk = cr_ref.shape[2]
    hd = FOX_HEAD_DIM
    lane = lax.broadcasted_iota(jnp.int32, (t, LANES), 1)
    row = lax.broadcasted_iota(jnp.int32, (t, t), 0)
    col = lax.broadcasted_iota(jnp.int32, (t, t), 1)
    head_lanes = [lane < hd, lane >= hd]

    @pl.when(i == 0)
    def _():
        step = 1024
        lane_s = lax.broadcasted_iota(jnp.int32, (step, LANES), 1)

        def chunk(ci, mx):
            kf = k_ref[0, pl.ds(pl.multiple_of(ci * step, step), step), :].astype(F32)
            k2 = kf * kf
            n0 = jnp.max(jnp.sum(jnp.where(lane_s < hd, k2, 0.0), axis=1, keepdims=True))
            n1 = jnp.max(jnp.sum(jnp.where(lane_s >= hd, k2, 0.0), axis=1, keepdims=True))
            return jnp.maximum(mx[0], n0), jnp.maximum(mx[1], n1)

        n0, n1 = lax.fori_loop(0, k_ref.shape[1] // step, chunk, (jnp.float32(0.0), jnp.float32(0.0)))
        kn_ref[0] = jnp.sqrt(n0)
        kn_ref[1] = jnp.sqrt(n1)

    qq = q_ref[0]
    cc = cc_ref[0]
    zero = jnp.zeros_like(qq)
    qs = jnp.concatenate([jnp.where(head_lanes[h], qq, zero) for h in range(2)], axis=0)
    cq = [jnp.sum(jnp.where(lane == 2 * hp + h, cc, 0.0), axis=1, keepdims=True) for h in range(2)]
    qf = qq.astype(F32)
    q2 = qf * qf
    qn = [jnp.sqrt(jnp.sum(jnp.where(head_lanes[h], q2, 0.0), axis=1, keepdims=True)) for h in range(2)]
    causal = jnp.concatenate([col <= row, col <= row], axis=0)

    def scores(j, diag):
        kk = k_ref[0, pl.ds(pl.multiple_of(j * t, t), t), :]
        s = _dot_nt(qs, kk)
        s = jnp.concatenate(
            [s[h * t:(h + 1) * t] + (cq[h] - cr_ref[0, h, pl.ds(j, 1), :]) for h in range(2)], axis=0)
        return jnp.where(causal, s, -jnp.inf) if diag else s

    def values(j):
        return v_ref[0, pl.ds(pl.multiple_of(j * t, t), t), :]

    def first_round(js):
        m = l = acc = None
        for n, j in enumerate(js):
            s = scores(j, n == len(js) - 1)
            m_blk = jnp.max(s, axis=1, keepdims=True)
            if m is None:
                m_next = m_blk
                p = jnp.exp(s - m_next)
                l = jnp.sum(p, axis=1, keepdims=True)
                acc = _dot(p.astype(BF16), values(j))
            else:
                m_next = jnp.maximum(m, m_blk)
                alpha = jnp.exp(m - m_next)
                p = jnp.exp(s - m_next)
                l = alpha * l + jnp.sum(p, axis=1, keepdims=True)
                acc = alpha * acc + _dot(p.astype(BF16), values(j))
            m = m_next
        m_ref[...] = jnp.broadcast_to(m, m_ref.shape)
        l_ref[...] = jnp.broadcast_to(l, l_ref.shape)
        acc_ref[...] = acc

    def block(j):
        s = scores(j, False)
        m_prev = m_ref[...]
        m_next = jnp.maximum(m_prev, jnp.max(s, axis=1, keepdims=True))
        p = jnp.exp(s - pltpu.repeat(m_next, t // LANES, 1))
        alpha = jnp.exp(m_prev - m_next)
        l_ref[...] = alpha * l_ref[...] + jnp.sum(p, axis=1, keepdims=True)
        m_ref[...] = m_next
        acc_ref[...] = alpha * acc_ref[...] + _dot(p.astype(BF16), values(j))

    window = FOX_FIRST_ROUND_BLOCKS

    @pl.when(i >= window - 1)
    def _():
        first_round([i - (window - 1) + n for n in range(window)])

    @pl.when(i < window - 1)
    def _():
        first_round([i])

    j_next = jnp.where(i >= window - 1, i - window, i - 1)

    jidx = lax.broadcasted_iota(jnp.int32, (nk, 1), 0)
    j_min = j_next + 1
    for h in range(2):
        reach = jnp.max(qn[h] * (kn_ref[h] * FOX_NORM_SLACK) + cq[h] - m_ref[h * t:(h + 1) * t, :1])
        c_min = jnp.min(cr_ref[0, h], axis=1, keepdims=True)
        needed = (reach - c_min >= -FOX_SKIP_GAP) & (jidx <= j_next)
        j_min = jnp.minimum(j_min, jnp.min(jnp.where(needed, jidx, j_next + 1)))

    def earlier(n, carry):
        block(j_next - n)
        return carry

    lax.fori_loop(0, j_next + 1 - j_min, earlier, 0)
    o = acc_ref[...] / l_ref[...]
    o_ref[0] = jnp.where(head_lanes[0], o[:t], o[t:]).astype(BF16)


def _fox_prompt(q, k, v, c_col, c_rows, *, t):
    b, l, w = q.shape
    nhp = w // LANES
    nk = l // t
    return pl.pallas_call(
        functools.partial(_fox_prompt_body, t=t),
        grid=(b, nhp, l // t),
        in_specs=[
            pl.BlockSpec((1, t, LANES), lambda bi, hp, i: (bi, i, hp)),
            pl.BlockSpec((1, l, LANES), lambda bi, hp, i: (bi, 0, hp)),
            pl.BlockSpec((1, l, LANES), lambda bi, hp, i: (bi, 0, hp)),
            pl.BlockSpec((1, t, LANES), lambda bi, hp, i: (bi, i, 0)),
            pl.BlockSpec((1, 2, nk, t), lambda bi, hp, i: (bi, hp, 0, 0)),
        ],
        out_specs=pl.BlockSpec((1, t, LANES), lambda bi, hp, i: (bi, i, hp)),
        out_shape=jax.ShapeDtypeStruct((b, l, w), BF16),
        scratch_shapes=[pltpu.VMEM((2 * t, LANES), F32), pltpu.VMEM((2 * t, LANES), F32),
                        pltpu.VMEM((2 * t, LANES), F32), pltpu.SMEM((2,), F32)],
        compiler_params=_cparams("parallel", "parallel", "arbitrary"),
        name="fox_prompt",
    )(q, k, v, c_col, c_rows)


def _fox_sample_body(q_ref, k_ref, v_ref, ck_ref, cv_ref, clf_ref, lfr_ref, lft_ref, o_ref):
    hp = pl.program_id(1)
    t = q_ref.shape[1]
    p_len = clf_ref.shape[4]
    hd = FOX_HEAD_DIM
    lane = lax.broadcasted_iota(jnp.int32, (t, LANES), 1)
    row = lax.broadcasted_iota(jnp.int32, (t, t), 0)
    col = lax.broadcasted_iota(jnp.int32, (t, t), 1)
    qq = q_ref[0]
    kk = k_ref[0]
    vv = v_ref[0]
    zero = jnp.zeros_like(qq)
    c_cache = _scan(clf_ref[0, 0, 0], 1)
    cn_row = _scan(lfr_ref[0, 0], 1)
    cn_col = _scan(lft_ref[0], 0)
    outs = []
    for h in range(2):
        lo = h * hd
        qm = jnp.where((lane >= lo) & (lane < lo + hd), qq, zero)
        q_h = qq[:, lo:lo + hd]
        kc_t = ck_ref[0, 0, h].astype(BF16)
        vc_t = cv_ref[0, 0, h].astype(BF16)
        c_h = c_cache[h:h + 1, :]
        tot = c_h[:, p_len - 1:p_len]
        cq = jnp.sum(jnp.where(lane == 2 * hp + h, cn_col, 0.0), axis=1, keepdims=True)
        s_c = _dot(q_h, kc_t) + (cq + (tot - c_h))
        s_s = _dot_nt(qm, kk) + (cq - cn_row[h:h + 1, :t])
        s_s = jnp.where(col <= row, s_s, -jnp.inf)
        m = jnp.maximum(jnp.max(s_c, axis=1, keepdims=True), jnp.max(s_s, axis=1, keepdims=True))
        p_c = jnp.exp(s_c - m)
        p_s = jnp.exp(s_s - m)
        den = jnp.sum(p_c, axis=1, keepdims=True) + jnp.sum(p_s, axis=1, keepdims=True)
        o = _dot_nt(p_c.astype(BF16), vc_t) + _dot(p_s.astype(BF16), vv)[:, lo:lo + hd]
        outs.append(o / den)
    o_ref[0] = jnp.concatenate(outs, axis=1).astype(BF16)


def _fox_sample(q, k, v, cache_k, cache_v, clf, lf_row, lf_tm, *, layer):
    b, t, w = q.shape
    nhp = w // LANES
    hd, p_len = cache_k.shape[3], cache_k.shape[4]
    tokq = pl.BlockSpec((1, t, LANES), lambda bi, hp: (bi, 0, hp))
    cache = pl.BlockSpec((1, 1, 2, hd, p_len), lambda bi, hp: (layer, bi, hp, 0, 0))
    return pl.pallas_call(
        _fox_sample_body,
        grid=(b, nhp),
        in_specs=[tokq, tokq, tokq, cache, cache,
                  pl.BlockSpec((1, 1, 1, 2, p_len), lambda bi, hp: (layer, bi, hp, 0, 0)),
                  pl.BlockSpec((1, 1, 2, LANES), lambda bi, hp: (bi, hp, 0, 0)),
                  pl.BlockSpec((1, t, LANES), lambda bi, hp: (bi, 0, 0))],
        out_specs=tokq,
        out_shape=jax.ShapeDtypeStruct((b, t, w), BF16),
        compiler_params=_cparams("parallel", "parallel"),
        name="fox_sample",
    )(q, k, v, cache_k, cache_v, clf, lf_row, lf_tm)


def _hgrn_body(q_ref, lf_ref, k_ref, v_ref, s0_ref, o_ref, so_ref, st_ref, tmp_ref, *, c, n_chunks, zero_init):
    ti = pl.program_id(2)
    d = HGRN_HEAD_DIM

    @pl.when(ti == 0)
    def _():
        st_ref[...] = jnp.zeros((d, d), F32) if zero_init else s0_ref[0, 0, 0].T

    row = lax.broadcasted_iota(jnp.int32, (c, c), 0)
    col = lax.broadcasted_iota(jnp.int32, (c, c), 1)
    rowc = lax.broadcasted_iota(jnp.int32, (c, 1), 0)
    half = c // 2
    b_all = _scan(lf_ref[0], 0, period=c)

    span = jnp.zeros((1, d), F32)
    for ci in range(n_chunks):
        b = b_all[ci * c:(ci + 1) * c]
        b_mid = b[half - 1:half]
        span = jnp.maximum(span, jnp.maximum(b[0:1] - b_mid, b_mid - b[c - 1:c]))
    factorable = jnp.max(span) <= HGRN_FACTOR_LIMIT

    def carry_state(st, k, v, b):
        b_last = b[c - 1:c]
        kh = (k * jnp.exp(b_last - b)).astype(BF16)
        return st * jnp.exp(b_last) + _dot_tn(v, kh)

    @pl.when(factorable)
    def _():
        st = st_ref[...]
        for ci in range(n_chunks):
            r0 = ci * c
            q = q_ref[0, r0:r0 + c, :]
            k = k_ref[0, r0:r0 + c, :]
            v = v_ref[0, r0:r0 + c, :]
            b = b_all[r0:r0 + c]
            b_mid = b[half - 1:half]
            o = _dot_nt((q * jnp.exp(b)).astype(BF16), st.astype(BF16))
            qt = (q * jnp.exp(b - b_mid)).astype(BF16)
            kt = (k * jnp.exp(b_mid - b)).astype(BF16)
            sc = jnp.where(col <= row, _dot_nt(qt, kt), 0.0)
            o_ref[0, r0:r0 + c, :] = o + _dot(sc.astype(BF16), v)
            st = carry_state(st, k, v, b)
        st_ref[...] = st

    @pl.when(jnp.logical_not(factorable))
    def _():
        def chunk(ci, carry):
            r0 = pl.multiple_of(ci * c, c)
            q = q_ref[0, pl.ds(r0, c), :]
            k = k_ref[0, pl.ds(r0, c), :]
            v = v_ref[0, pl.ds(r0, c), :]
            b = _scan(lf_ref[0, pl.ds(r0, c), :], 0)
            st = st_ref[...]
            tmp_ref[0] = b
            tmp_ref[1] = k
            tmp_ref[2] = v.astype(F32)

            def key_row(s, acc):
                bs = tmp_ref[0, pl.ds(s, 1), :]
                ks = tmp_ref[1, pl.ds(s, 1), :]
                vs = tmp_ref[2, pl.ds(s, 1), :]
                w = jnp.exp(jnp.minimum(b - bs, 0.0))
                a = jnp.sum(q * ks * w, axis=1, keepdims=True)
                return acc + jnp.where(rowc >= s, a, 0.0) * vs

            o = _dot_nt((q * jnp.exp(b)).astype(BF16), st.astype(BF16))
            o_ref[0, pl.ds(r0, c), :] = lax.fori_loop(0, c, key_row, o)
            st_ref[...] = carry_state(st, k, v, b)
            return carry

        lax.fori_loop(0, n_chunks, chunk, 0)

    @pl.when(ti == pl.num_programs(2) - 1)
    def _():
        so_ref[0, 0, 0] = st_ref[...].T


def _hgrn(hq, hlf, hk, hv, s0, carried, *, layer, c, tl):
    b, l, w = hq.shape
    d = HGRN_HEAD_DIM
    nh = w // d
    tok = pl.BlockSpec((1, tl, d), lambda bi, h, ti: (bi, ti, h))
    st = pl.BlockSpec((1, 1, 1, d, d), lambda bi, h, ti: (layer, bi, h, 0, 0))
    zero_init = s0 is None
    body = functools.partial(_hgrn_body, c=c, n_chunks=tl // c, zero_init=zero_init)
    args, in_specs = [hq, hlf, hk, hv], [tok, tok, tok, tok]
    if zero_init:
        core = body
        body = lambda q, lf, k, v, *rest: core(q, lf, k, v, None, *rest)
    else:
        args.append(s0)
        in_specs.append(st)
    aliases = {}
    if carried is not None:
        aliases[len(args)] = 1
        body = _drop_alias_refs(body, len(args), 1)
        args.append(carried)
        in_specs.append(_ANY)
    return pl.pallas_call(
        body,
        grid=(b, nh, l // tl),
        in_specs=in_specs,
        out_specs=[tok, st],
        out_shape=[jax.ShapeDtypeStruct((b, l, w), F32), jax.ShapeDtypeStruct((DEPTH, b, nh, d, d), F32)],
        scratch_shapes=[pltpu.VMEM((d, d), F32), pltpu.VMEM((3, c, d), F32)],
        input_output_aliases=aliases,
        compiler_params=_cparams("parallel", "parallel", "arbitrary"),
        name="hgrn",
    )(*args)


def _outproj_body(x_ref, fo_ref, ho_ref, hg_ref, gn_ref, w_ref, o_ref, wb_ref):
    @pl.when(pl.program_id(0) == 0)
    def _():
        wb_ref[...] = w_ref[0].astype(BF16)

    gn = gn_ref[0]
    d = HGRN_HEAD_DIM
    parts = [fo_ref[...]]
    for h in range(HGRN_HEADS):
        ho = ho_ref[:, h * d:(h + 1) * d]
        parts.append((_rms(ho, gn) * _silu(hg_ref[:, h * d:(h + 1) * d])).astype(BF16))
    o_ref[...] = x_ref[...] + _dot(jnp.concatenate(parts, axis=1), wb_ref[...])


def _outproj(x, fo, ho, hg, gn, w_out, *, layer, tm):
    m, d = x.shape
    dm = w_out.shape[1]
    tok = lambda width: pl.BlockSpec((tm, width), lambda i: (i, 0))
    return pl.pallas_call(
        _outproj_body,
        grid=(m // tm,),
        in_specs=[tok(d), tok(FOX_WIDTH), tok(HGRN_WIDTH), tok(HGRN_WIDTH),
                  pl.BlockSpec((1, 1, HGRN_HEAD_DIM), lambda i: (layer, 0, 0)),
                  pl.BlockSpec((1, dm, d), lambda i: (layer, 0, 0))],
        out_specs=tok(d),
        out_shape=jax.ShapeDtypeStruct((m, d), F32),
        scratch_shapes=[pltpu.VMEM((dm, d), BF16)],
        compiler_params=_cparams("arbitrary"),
        name="outproj",
    )(x, fo, ho, hg, gn, w_out)


def kernel(x_prompt, x_sample, cache_k, cache_v, cache_logf, state_hgrn, norm_ffn1, ffn1_wi, ffn1_wo,
           norm_mix, w_in, b_fgate, hgrn_lb, hgrn_gnorm, w_out, norm_ffn2, ffn2_wi, ffn2_wo, norm_final):
    bp, lp, d = x_prompt.shape
    bs, ls, _ = x_sample.shape
    mp, ms = bp * lp, bs * ls
    p_len = cache_k.shape[3]
    f, w, h8 = FOX_WIDTH, HGRN_WIDTH, FOX_HEADS
    tm_ffn, tf_ffn, tm_proj, t_attn, c_hgrn, tl_hgrn = 1024, 256, 512, 256, 64, 512

    xp = x_prompt.reshape(mp, d)
    xs = x_sample.reshape(ms, d)
    gfin = norm_final.reshape(1, d)
    g1, g2, gm = (a.reshape(DEPTH, 1, d) for a in (norm_ffn1, norm_ffn2, norm_mix))
    gn = hgrn_gnorm.reshape(DEPTH, 1, HGRN_HEAD_DIM)
    clf = cache_logf.reshape(DEPTH, bs, h8 // 2, 2, p_len)
    cache_kt = jnp.swapaxes(cache_k, 3, 4)
    cache_vt = jnp.swapaxes(cache_v, 3, 4)

    carry_p = carry_s = None
    st_p = st_s = None
    lfs = []
    for l in range(DEPTH):
        last = l == DEPTH - 1
        wl = w_in[l]
        wm = jnp.concatenate([wl[:, :3 * f], wl[:, 3 * f + h8:]], axis=1).astype(BF16)
        wf = jnp.pad(wl[:, 3 * f:3 * f + h8], ((0, 0), (0, LANES - h8))).astype(BF16)
        bfp = jnp.pad(b_fgate[l].reshape(1, h8), ((0, 0), (0, LANES - h8)))

        xp = _ffn(xp, g1, ffn1_wi, ffn1_wo, gfin, layer=l, final_norm=False, tm=tm_ffn, tf=tf_ffn)
        xs = _ffn(xs, g1, ffn1_wi, ffn1_wo, gfin, layer=l, final_norm=False, tm=ms, tf=tf_ffn)

        (q, k, v, kh, vh, lft, lfr, hq, hlf, hk, hv, hg) = _inproj(
            xp, gm, wm, wf, bfp, hgrn_lb, carry_p, layer=l, sample=False, batch=bp, tm=tm_proj)
        carry_p = (kh, vh, lfr)
        (q_s, k_s, v_s, kh_s, vh_s, lft_s, lfr_s, hq_s, hlf_s, hk_s, hv_s, hg_s) = _inproj(
            xs, gm, wm, wf, bfp, hgrn_lb, carry_s, layer=l, sample=True, batch=bs, tm=ms)
        carry_s = (kh_s, vh_s)

        c_row, c_col = _fox_cumsum(lfr, lft.reshape(bp, lp, LANES), layer=l)
        fo = _fox_prompt(q.reshape(bp, lp, f), k.reshape(bp, lp, f), v.reshape(bp, lp, f),
                         c_col, c_row.reshape(bp, h8, lp // t_attn, t_attn), t=t_attn)
        lfs_l = lfr_s.reshape(h8, bs, ls).transpose(1, 0, 2)
        lfs.append(lfs_l)
        lfs_pad = jnp.pad(lfs_l, ((0, 0), (0, 0), (0, LANES - ls))).reshape(bs, h8 // 2, 2, LANES)
        fo_s = _fox_sample(q_s.reshape(bs, ls, f), k_s.reshape(bs, ls, f), v_s.reshape(bs, ls, f),
                           cache_kt, cache_vt, clf, lfs_pad, lft_s.reshape(bs, ls, LANES), layer=l)
        to3 = lambda a, b_, l_: a.reshape(b_, l_, w)
        ho, st_p = _hgrn(to3(hq, bp, lp), to3(hlf, bp, lp), to3(hk, bp, lp), to3(hv, bp, lp), None, st_p,
                         layer=l, c=c_hgrn, tl=tl_hgrn)
        ho_s, st_s = _hgrn(to3(hq_s, bs, ls), to3(hlf_s, bs, ls), to3(hk_s, bs, ls), to3(hv_s, bs, ls),
                           state_hgrn, st_s, layer=l, c=ls, tl=ls)

        xp = _outproj(xp, fo.reshape(mp, f), ho.reshape(mp, w), hg, gn, w_out, layer=l, tm=tm_proj)
        xs = _outproj(xs, fo_s.reshape(ms, f), ho_s.reshape(ms, w), hg_s, gn, w_out, layer=l, tm=ms)
        xp = _ffn(xp, g2, ffn2_wi, ffn2_wo, gfin, layer=l, final_norm=last, tm=tm_ffn, tf=tf_ffn)
        xs = _ffn(xs, g2, ffn2_wi, ffn2_wo, gfin, layer=l, final_norm=last, tm=ms, tf=tf_ffn)

    k_prompt = jnp.swapaxes(carry_p[0], 3, 4)
    v_prompt = jnp.swapaxes(carry_p[1], 3, 4)
    return (xp.reshape(bp, lp, d), xs.reshape(bs, ls, d), k_prompt, v_prompt, carry_p[2], st_p,
            carry_s[0], carry_s[1], jnp.stack(lfs), st_s)
```

```python
import functools

import jax
import jax.numpy as jnp
from jax import lax
from jax.experimental import pallas as pl
from jax.experimental.pallas import tpu as pltpu

F32 = jnp.float32
BF16 = jnp.bfloat16

D_MODEL = 1024
DEPTH = 2
FOX_HEADS = 8
FOX_HEAD_DIM = 64
FOX_WIDTH = FOX_HEADS * FOX_HEAD_DIM
HGRN_HEADS = 4
HGRN_HEAD_DIM = 128
HGRN_WIDTH = HGRN_HEADS * HGRN_HEAD_DIM
D_FF = 2816
RMS_EPS = 1e-6
FOX_SCALE = FOX_HEAD_DIM ** -0.5

LANES = 128
VMEM_LIMIT_BYTES = 56 * 1024 * 1024
HGRN_FACTOR_LIMIT = 50.0
FOX_SKIP_GAP = 106.0
FOX_NORM_SLACK = 1.001
FOX_FIRST_ROUND_BLOCKS = 4


def _cparams(*sem):
    return pltpu.CompilerParams(dimension_semantics=sem, vmem_limit_bytes=VMEM_LIMIT_BYTES)


def _rms(x, g):
    ms = jnp.mean(x * x, axis=-1, keepdims=True)
    return x * lax.rsqrt(ms + RMS_EPS) * g


def _silu(x):
    return x * jax.nn.sigmoid(x)


def _log_sigmoid(z):
    return jnp.minimum(z, 0.0) - jnp.log1p(jnp.exp(-jnp.abs(z)))


def _dot(a, b):
    return jnp.dot(a, b, preferred_element_type=F32)


def _dot_nt(a, b):
    return lax.dot_general(a, b, (((1,), (1,)), ((), ())), preferred_element_type=F32)


def _dot_tn(a, b):
    return lax.dot_general(a, b, (((0,), (0,)), ((), ())), preferred_element_type=F32)


def _scan(x, axis, period=None):
    n = x.shape[axis] if period is None else period
    idx = lax.broadcasted_iota(jnp.int32, x.shape, axis)
    if period is not None:
        idx = idx & (period - 1)
    s = 1
    while s < n:
        x = x + jnp.where(idx >= s, pltpu.roll(x, s, axis), 0.0)
        s *= 2
    return x


def _drop_alias_refs(body, n_in, n_alias):
    def wrapped(*refs):
        return body(*refs[:n_in], *refs[n_in + n_alias:])
    return wrapped


_ANY = pl.BlockSpec(memory_space=pl.ANY)


def _mix(fo_ref, ho_ref, hg_ref, gn_ref):
    gn = gn_ref[0]
    d = HGRN_HEAD_DIM
    parts = [fo_ref[...]]
    for h in range(HGRN_HEADS):
        ho = ho_ref[:, h * d:(h + 1) * d]
        parts.append((_rms(ho, gn) * _silu(hg_ref[:, h * d:(h + 1) * d])).astype(BF16))
    return jnp.concatenate(parts, axis=1)


def _ffn_body(*refs, final_norm, with_outproj, tf):
    if with_outproj:
        (x_ref, fo_ref, ho_ref, hg_ref, gn_ref, wout_ref,
         g_ref, wi_ref, wo_ref, gf_ref, o_ref, acc_ref) = refs
        x = x_ref[...] + _dot(_mix(fo_ref, ho_ref, hg_ref, gn_ref), wout_ref[...])
    else:
        x_ref, g_ref, wi_ref, wo_ref, gf_ref, o_ref, acc_ref = refs
        x = x_ref[...]
    f = wo_ref.shape[0]
    xn = _rms(x, g_ref[0]).astype(BF16)
    for c in range(f // tf):
        a = _dot(xn, wi_ref[:, c * tf:(c + 1) * tf])
        b = _dot(xn, wi_ref[:, f + c * tf:f + (c + 1) * tf])
        part = _dot((_silu(a) * b).astype(BF16), wo_ref[c * tf:(c + 1) * tf, :])
        if c == 0:
            acc_ref[...] = part
        else:
            acc_ref[...] += part
    y = x + 0.5 * acc_ref[...]
    if final_norm:
        y = _rms(y, gf_ref[...])
    o_ref[...] = y


def _ffn(x, g, wi, wo, gf, outproj=None, *, layer, final_norm, tm, tf):
    m, d = x.shape
    tok = lambda width: pl.BlockSpec((tm, width), lambda i: (i, 0))
    held = lambda a: pl.BlockSpec(a.shape, lambda i: (0,) * a.ndim, pipeline_mode=pl.Buffered(1))
    args, in_specs = [x], [tok(d)]
    if outproj is not None:
        fo, ho, hg, gn, wout = outproj
        args += [fo, ho, hg, gn, wout]
        in_specs += [tok(fo.shape[1]), tok(ho.shape[1]), tok(hg.shape[1]),
                     pl.BlockSpec((1, 1, gn.shape[2]), lambda i: (layer, 0, 0)), held(wout)]
    args += [g, wi, wo, gf]
    in_specs += [pl.BlockSpec((1, 1, d), lambda i: (layer, 0, 0)), held(wi), held(wo),
                 pl.BlockSpec((1, d), lambda i: (0, 0))]
    return pl.pallas_call(
        functools.partial(_ffn_body, final_norm=final_norm, with_outproj=outproj is not None, tf=tf),
        grid=(m // tm,),
        in_specs=in_specs,
        out_specs=tok(d),
        out_shape=jax.ShapeDtypeStruct((m, d), F32),
        scratch_shapes=[pltpu.VMEM((tm, d), F32)],
        compiler_params=_cparams("parallel"),
        name="ffn",
    )(*args)


def _inproj_body(x_ref, g_ref, wm_ref, wf_ref, bf_ref, lbp_ref,
                 q_ref, k_ref, v_ref, kh_ref, vh_ref, lft_ref, lfr_ref,
                 hq_ref, hlf_ref, hk_ref, hv_ref, hg_ref, *, layer, sample):
    h = _rms(x_ref[...], g_ref[0]).astype(BF16)
    w, hd = FOX_WIDTH, FOX_HEAD_DIM

    def sec(i):
        return _dot(h, wm_ref[:, i * w:(i + 1) * w])

    def store_heads(ref, val):
        if sample:
            for hh in range(FOX_HEADS):
                piece = val[:, hh * hd:(hh + 1) * hd]
                ref[0, :, hh] = piece.reshape(ref.shape[1], ref.shape[3], hd)
        else:
            ref[0, 0] = val.T.reshape(FOX_HEADS, hd, val.shape[0])

    q = sec(0)
    q_ref[...] = (q * FOX_SCALE).astype(BF16)
    k = sec(1)
    store_heads(kh_ref, k)
    k_ref[...] = k.astype(BF16)
    v = sec(2)
    store_heads(vh_ref, v)
    v_ref[...] = v.astype(BF16)

    lf = _log_sigmoid(_dot(h, wf_ref[...]) + bf_ref[...])
    lft_ref[...] = lf
    lfr = lf.T[:FOX_HEADS, :]
    if sample:
        lfr_ref[...] = lfr
    else:
        lfr_ref[0, 0] = lfr

    hq_ref[...] = _silu(sec(3))

    p = lbp_ref[...]
    e = jnp.exp(p - jnp.max(p, axis=0, keepdims=True))
    sm = e / jnp.sum(e, axis=0, keepdims=True)
    cs = sm[0:1]
    for r in range(1, layer + 1):
        cs = cs + sm[r:r + 1]
    lb = cs - sm[0:1]

    z = sec(4)
    a = jnp.log(lb)
    b = jnp.log1p(-lb) + _log_sigmoid(z)
    hlf_ref[...] = jnp.maximum(a, b) + jnp.log1p(jnp.exp(-jnp.abs(a - b)))
    hk_ref[...] = (1.0 - lb) * jax.nn.sigmoid(-z)
    hv_ref[...] = sec(5).astype(BF16)
    hg_ref[...] = sec(6)


def _inproj(x, g, wm, wf, bfp, lbp, carried, *, layer, sample, batch, tm):
    m, d = x.shape
    w, hd, nh = FOX_WIDTH, FOX_HEAD_DIM, FOX_HEADS
    seq = m // batch
    tiles_per_seq = seq // tm if not sample else 1
    tok = lambda width: pl.BlockSpec((tm, width), lambda i: (i, 0))
    full = lambda a: pl.BlockSpec(a.shape, lambda i: (0,) * a.ndim)
    sds = jax.ShapeDtypeStruct
    if sample:
        heads_shape = (DEPTH, batch, nh, seq, hd)
        heads_spec = pl.BlockSpec((1, batch, nh, seq, hd), lambda i: (layer, 0, 0, 0, 0))
        lfr_shape, lfr_spec = (nh, m), pl.BlockSpec((nh, m), lambda i: (0, 0))
    else:
        heads_shape = (DEPTH, batch, nh, hd, seq)
        heads_spec = pl.BlockSpec((1, 1, nh, hd, tm),
                                  lambda i: (layer, i // tiles_per_seq, 0, 0, i % tiles_per_seq))
        lfr_shape = (DEPTH, batch, nh, seq)
        lfr_spec = pl.BlockSpec((1, 1, nh, tm), lambda i: (layer, i // tiles_per_seq, 0, i % tiles_per_seq))
    out_specs = [tok(w), tok(w), tok(w), heads_spec, heads_spec, tok(LANES), lfr_spec,
                 tok(w), tok(w), tok(w), tok(w), tok(w)]
    out_shape = [sds((m, w), BF16), sds((m, w), BF16), sds((m, w), BF16),
                 sds(heads_shape, F32), sds(heads_shape, F32), sds((m, LANES), F32), sds(lfr_shape, F32),
                 sds((m, w), F32), sds((m, w), F32), sds((m, w), F32), sds((m, w), BF16), sds((m, w), F32)]
    body = functools.partial(_inproj_body, layer=layer, sample=sample)
    args = [x, g, wm, wf, bfp, lbp]
    in_specs = [tok(d), pl.BlockSpec((1, 1, d), lambda i: (layer, 0, 0)), full(wm), full(wf), full(bfp), full(lbp)]
    aliases = {}
    if carried is not None:
        out_index = (3, 4, 6)
        for n, buf in enumerate(carried):
            aliases[len(args)] = out_index[n]
            args.append(buf)
            in_specs.append(_ANY)
        body = _drop_alias_refs(body, 6, len(carried))
    return pl.pallas_call(
        body, grid=(m // tm,), in_specs=in_specs, out_specs=out_specs, out_shape=out_shape,
        input_output_aliases=aliases, compiler_params=_cparams("arbitrary"), name="inproj",
    )(*args)


def _cumsum_body(lfr_ref, lft_ref, cr_ref, cc_ref):
    cr_ref[0] = _scan(lfr_ref[0, 0], 1)
    cc_ref[0] = _scan(lft_ref[0], 0)


def _fox_cumsum(lf_rows, lf_tm, *, layer):
    _, b, h, l = lf_rows.shape
    return pl.pallas_call(
        _cumsum_body,
        grid=(b,),
        in_specs=[pl.BlockSpec((1, 1, h, l), lambda i: (layer, i, 0, 0)),
                  pl.BlockSpec((1, l, LANES), lambda i: (i, 0, 0))],
        out_specs=[pl.BlockSpec((1, h, l), lambda i: (i, 0, 0)),
                   pl.BlockSpec((1, l, LANES), lambda i: (i, 0, 0))],
        out_shape=[jax.ShapeDtypeStruct((b, h, l), F32),
                   jax.ShapeDtypeStruct((b, l, LANES), F32)],
        compiler_params=_cparams("parallel"),
        name="fox_cumsum",
    )(lf_rows, lf_tm)


def _fox_prompt_body(q_ref, k_ref, v_ref, cc_ref, cr_ref, o_ref, m_ref, l_ref, acc_ref, kn_ref, *, t):
    hp = pl.program_id(1)
    i = pl.program_id(2)
    nk = cr_ref.shape[2]
    hd = FOX_HEAD_DIM
    lane = lax.broadcasted_iota(jnp.int32, (t, LANES), 1)
    row = lax.broadcasted_iota(jnp.int32, (t, t), 0)
    col = lax.broadcasted_iota(jnp.int32, (t, t), 1)
    head_lanes = [lane < hd, lane >= hd]

    @pl.when(i == 0)
    def _():
        step = 1024
        lane_s = lax.broadcasted_iota(jnp.int32, (step, LANES), 1)

        def chunk(ci, mx):
            kf = k_ref[0, pl.ds(pl.multiple_of(ci * step, step), step), :].astype(F32)
            k2 = kf * kf
            n0 = jnp.max(jnp.sum(jnp.where(lane_s < hd, k2, 0.0), axis=1, keepdims=True))
            n1 = jnp.max(jnp.sum(jnp.where(lane_s >= hd, k2, 0.0), axis=1, keepdims=True))
            return jnp.maximum(mx[0], n0), jnp.maximum(mx[1], n1)

        n0, n1 = lax.fori_loop(0, k_ref.shape[1] // step, chunk, (jnp.float32(0.0), jnp.float32(0.0)))
        kn_ref[0] = jnp.sqrt(n0)
        kn_ref[1] = jnp.sqrt(n1)

    qq = q_ref[0]
    cc = cc_ref[0]
    zero = jnp.zeros_like(qq)
    qs = jnp.concatenate([jnp.where(head_lanes[h], qq, zero) for h in range(2)], axis=0)
    cq = [jnp.sum(jnp.where(lane == 2 * hp + h, cc, 0.0), axis=1, keepdims=True) for h in range(2)]
    qf = qq.astype(F32)
    q2 = qf * qf
    qn = [jnp.sqrt(jnp.sum(jnp.where(head_lanes[h], q2, 0.0), axis=1, keepdims=True)) for h in range(2)]
    causal = jnp.concatenate([col <= row, col <= row], axis=0)

    def scores(j, diag):
        kk = k_ref[0, pl.ds(pl.multiple_of(j * t, t), t), :]
        s = _dot_nt(qs, kk)
        s = jnp.concatenate(
            [s[h * t:(h + 1) * t] + (cq[h] - cr_ref[0, h, pl.ds(j, 1), :]) for h in range(2)], axis=0)
        return jnp.where(causal, s, -jnp.inf) if diag else s

    def values(j):
        return v_ref[0, pl.ds(pl.multiple_of(j * t, t), t), :]

    def first_round(js):
        m = l = acc = None
        for n, j in enumerate(js):
            s = scores(j, n == len(js) - 1)
            m_blk = jnp.max(s, axis=1, keepdims=True)
            if m is None:
                m_next = m_blk
                p = jnp.exp(s - m_next)
                l = jnp.sum(p, axis=1, keepdims=True)
                acc = _dot(p.astype(BF16), values(j))
            else:
                m_next = jnp.maximum(m, m_blk)
                alpha = jnp.exp(m - m_next)
                p = jnp.exp(s - m_next)
                l = alpha * l + jnp.sum(p, axis=1, keepdims=True)
                acc = alpha * acc + _dot(p.astype(BF16), values(j))
            m = m_next
        m_ref[...] = jnp.broadcast_to(m, m_ref.shape)
        l_ref[...] = jnp.broadcast_to(l, l_ref.shape)
        acc_ref[...] = acc

    def block(j):
        s = scores(j, False)
        m_prev = m_ref[...]
        m_next = jnp.maximum(m_prev, jnp.max(s, axis=1, keepdims=True))
        p = jnp.exp(s - jnp.concatenate([m_next] * (t // LANES), axis=1))
        alpha = jnp.exp(m_prev - m_next)
        l_ref[...] = alpha * l_ref[...] + jnp.sum(p, axis=1, keepdims=True)
        m_ref[...] = m_next
        acc_ref[...] = alpha * acc_ref[...] + _dot(p.astype(BF16), values(j))

    window = FOX_FIRST_ROUND_BLOCKS

    @pl.when(i >= window - 1)
    def _():
        first_round([i - (window - 1) + n for n in range(window)])

    @pl.when(i < window - 1)
    def _():
        first_round([i])

    j_next = jnp.where(i >= window - 1, i - window, i - 1)

    jidx = lax.broadcasted_iota(jnp.int32, (nk, 1), 0)
    j_min = j_next + 1
    for h in range(2):
        reach = jnp.max(qn[h] * (kn_ref[h] * FOX_NORM_SLACK) + cq[h] - m_ref[h * t:(h + 1) * t, :1])
        c_min = jnp.min(cr_ref[0, h], axis=1, keepdims=True)
        needed = (reach - c_min >= -FOX_SKIP_GAP) & (jidx <= j_next)
        j_min = jnp.minimum(j_min, jnp.min(jnp.where(needed, jidx, j_next + 1)))

    def earlier(n, carry):
        block(j_next - n)
        return carry

    lax.fori_loop(0, j_next + 1 - j_min, earlier, 0)
    o = acc_ref[...] / l_ref[...]
    o_ref[0] = jnp.where(head_lanes[0], o[:t], o[t:]).astype(BF16)


def _fox_prompt(q, k, v, c_col, c_rows, *, t):
    b, l, w = q.shape
    nhp = w // LANES
    nk = l // t
    return pl.pallas_call(
        functools.partial(_fox_prompt_body, t=t),
        grid=(b, nhp, l // t),
        in_specs=[
            pl.BlockSpec((1, t, LANES), lambda bi, hp, i: (bi, i, hp)),
            pl.BlockSpec((1, l, LANES), lambda bi, hp, i: (bi, 0, hp)),
            pl.BlockSpec((1, l, LANES), lambda bi, hp, i: (bi, 0, hp)),
            pl.BlockSpec((1, t, LANES), lambda bi, hp, i: (bi, i, 0)),
            pl.BlockSpec((1, 2, nk, t), lambda bi, hp, i: (bi, hp, 0, 0)),
        ],
        out_specs=pl.BlockSpec((1, t, LANES), lambda bi, hp, i: (bi, i, hp)),
        out_shape=jax.ShapeDtypeStruct((b, l, w), BF16),
        scratch_shapes=[pltpu.VMEM((2 * t, LANES), F32), pltpu.VMEM((2 * t, LANES), F32),
                        pltpu.VMEM((2 * t, LANES), F32), pltpu.SMEM((2,), F32)],
        compiler_params=_cparams("parallel", "parallel", "arbitrary"),
        name="fox_prompt",
    )(q, k, v, c_col, c_rows)


def _sample_scan_body(clf_ref, lfr_ref, lft_ref, cc_ref, cnr_ref, cnc_ref, *, t):
    cc_ref[...] = _scan(clf_ref[0], 1)
    cnr_ref[...] = _scan(lfr_ref[...], 1)
    cnc_ref[...] = _scan(lft_ref[...], 0, period=t)


def _fox_sample_scan(clf, lf_row, lf_tm, *, layer, t):
    _, r, p_len = clf.shape
    full = lambda a: pl.BlockSpec(a.shape, lambda i: (0,) * a.ndim)
    return pl.pallas_call(
        functools.partial(_sample_scan_body, t=t),
        grid=(1,),
        in_specs=[pl.BlockSpec((1, r, p_len), lambda i: (layer, 0, 0)), full(lf_row), full(lf_tm)],
        out_specs=[pl.BlockSpec((r, p_len), lambda i: (0, 0)), full(lf_row), full(lf_tm)],
        out_shape=[jax.ShapeDtypeStruct((r, p_len), F32), jax.ShapeDtypeStruct(lf_row.shape, F32),
                   jax.ShapeDtypeStruct(lf_tm.shape, F32)],
        compiler_params=_cparams("arbitrary"),
        name="fox_sample_scan",
    )(clf, lf_row, lf_tm)


def _fox_sample_body(q_ref, k_ref, v_ref, ck_ref, cv_ref, cc_ref, cnr_ref, cnc_ref, o_ref):
    t = q_ref.shape[1]
    nh, p_len = ck_ref.shape[2], ck_ref.shape[4]
    hd = FOX_HEAD_DIM
    lane = lax.broadcasted_iota(jnp.int32, (t, LANES), 1)
    row = lax.broadcasted_iota(jnp.int32, (t, t), 0)
    col = lax.broadcasted_iota(jnp.int32, (t, t), 1)
    c_cache = cc_ref[...]
    cn_row = cnr_ref[...]
    cn_col = cnc_ref[0]
    outs = []
    for h in range(nh):
        grp = slice((h // 2) * LANES, (h // 2 + 1) * LANES)
        lo = (h % 2) * hd
        qq = q_ref[0, :, grp]
        qm = jnp.where((lane >= lo) & (lane < lo + hd), qq, jnp.zeros_like(qq))
        q_h = qq[:, lo:lo + hd]
        kc_t = ck_ref[0, 0, h].astype(BF16)
        vc_t = cv_ref[0, 0, h].astype(BF16)
        c_h = c_cache[h:h + 1, :]
        tot = c_h[:, p_len - 1:p_len]
        cq = jnp.sum(jnp.where(lane == h, cn_col, 0.0), axis=1, keepdims=True)
        s_c = _dot(q_h, kc_t) + (cq + (tot - c_h))
        s_s = _dot_nt(qm, k_ref[0, :, grp]) + (cq - cn_row[h:h + 1, :t])
        s_s = jnp.where(col <= row, s_s, -jnp.inf)
        m = jnp.maximum(jnp.max(s_c, axis=1, keepdims=True), jnp.max(s_s, axis=1, keepdims=True))
        p_c = jnp.exp(s_c - m)
        p_s = jnp.exp(s_s - m)
        den = jnp.sum(p_c, axis=1, keepdims=True) + jnp.sum(p_s, axis=1, keepdims=True)
        o = _dot_nt(p_c.astype(BF16), vc_t) + _dot(p_s.astype(BF16), v_ref[0, :, grp])[:, lo:lo + hd]
        outs.append(o / den)
    o_ref[0] = jnp.concatenate(outs, axis=1).astype(BF16)


def _fox_sample(q, k, v, cache_k, cache_v, c_cache, cn_row, cn_col, *, layer):
    b, t, w = q.shape
    nh, hd, p_len = cache_k.shape[2:]
    tokq = pl.BlockSpec((1, t, w), lambda bi: (bi, 0, 0))
    cache = pl.BlockSpec((1, 1, nh, hd, p_len), lambda bi: (layer, bi, 0, 0, 0))
    return pl.pallas_call(
        _fox_sample_body,
        grid=(b,),
        in_specs=[tokq, tokq, tokq, cache, cache,
                  pl.BlockSpec((nh, p_len), lambda bi: (bi, 0)),
                  pl.BlockSpec((nh, LANES), lambda bi: (bi, 0)),
                  pl.BlockSpec((1, t, LANES), lambda bi: (bi, 0, 0))],
        out_specs=tokq,
        out_shape=jax.ShapeDtypeStruct((b, t, w), BF16),
        compiler_params=_cparams("parallel"),
        name="fox_sample",
    )(q, k, v, cache_k, cache_v, c_cache, cn_row, cn_col)


def _hgrn_body(q_ref, lf_ref, k_ref, v_ref, s0_ref, o_ref, so_ref, st_ref, tmp_ref, *, c, n_chunks, hps, zero_init):
    ti = pl.program_id(2)
    d = HGRN_HEAD_DIM

    @pl.when(ti == 0)
    def _():
        for hh in range(hps):
            st_ref[hh] = jnp.zeros((d, d), F32) if zero_init else s0_ref[0, 0, hh].T

    row = lax.broadcasted_iota(jnp.int32, (c, c), 0)
    col = lax.broadcasted_iota(jnp.int32, (c, c), 1)
    rowc = lax.broadcasted_iota(jnp.int32, (c, 1), 0)
    half = c // 2
    b_all = _scan(lf_ref[0], 0, period=c)

    span = jnp.zeros((1, hps * d), F32)
    for ci in range(n_chunks):
        b = b_all[ci * c:(ci + 1) * c]
        b_mid = b[half - 1:half]
        span = jnp.maximum(span, jnp.maximum(b[0:1] - b_mid, b_mid - b[c - 1:c]))
    factorable = jnp.max(span) <= HGRN_FACTOR_LIMIT

    def carry_state(st, k, v, b):
        b_last = b[c - 1:c]
        kh = (k * jnp.exp(b_last - b)).astype(BF16)
        return st * jnp.exp(b_last) + _dot_tn(v, kh)

    @pl.when(factorable)
    def _():
        for hh in range(hps):
            hs = slice(hh * d, (hh + 1) * d)
            st = st_ref[hh]
            for ci in range(n_chunks):
                rs = slice(ci * c, (ci + 1) * c)
                q = q_ref[0, rs, hs]
                k = k_ref[0, rs, hs]
                v = v_ref[0, rs, hs]
                b = b_all[rs, hs]
                b_mid = b[half - 1:half]
                o = _dot_nt((q * jnp.exp(b)).astype(BF16), st.astype(BF16))
                qt = (q * jnp.exp(b - b_mid)).astype(BF16)
                kt = (k * jnp.exp(b_mid - b)).astype(BF16)
                sc = jnp.where(col <= row, _dot_nt(qt, kt), 0.0)
                o_ref[0, rs, hs] = o + _dot(sc.astype(BF16), v)
                st = carry_state(st, k, v, b)
            st_ref[hh] = st

    @pl.when(jnp.logical_not(factorable))
    def _():
        for hh in range(hps):
            hs = slice(hh * d, (hh + 1) * d)

            def chunk(ci, carry, hh=hh, hs=hs):
                rs = pl.ds(pl.multiple_of(ci * c, c), c)
                q = q_ref[0, rs, hs]
                k = k_ref[0, rs, hs]
                v = v_ref[0, rs, hs]
                b = _scan(lf_ref[0, rs, hs], 0)
                st = st_ref[hh]
                tmp_ref[0] = b
                tmp_ref[1] = k
                tmp_ref[2] = v.astype(F32)

                def key_row(s, acc):
                    bs = tmp_ref[0, pl.ds(s, 1), :]
                    ks = tmp_ref[1, pl.ds(s, 1), :]
                    vs = tmp_ref[2, pl.ds(s, 1), :]
                    w = jnp.exp(jnp.minimum(b - bs, 0.0))
                    a = jnp.sum(q * ks * w, axis=1, keepdims=True)
                    return acc + jnp.where(rowc >= s, a, 0.0) * vs

                o = _dot_nt((q * jnp.exp(b)).astype(BF16), st.astype(BF16))
                o_ref[0, rs, hs] = lax.fori_loop(0, c, key_row, o)
                st_ref[hh] = carry_state(st, k, v, b)
                return carry

            lax.fori_loop(0, n_chunks, chunk, 0)

    @pl.when(ti == pl.num_programs(2) - 1)
    def _():
        for hh in range(hps):
            so_ref[0, 0, hh] = st_ref[hh].T


def _hgrn(hq, hlf, hk, hv, s0, carried, *, layer, c, tl, hps):
    b, l, w = hq.shape
    d = HGRN_HEAD_DIM
    nh = w // d
    tok = pl.BlockSpec((1, tl, hps * d), lambda bi, h, ti: (bi, ti, h))
    st = pl.BlockSpec((1, 1, hps, d, d), lambda bi, h, ti: (layer, bi, h, 0, 0))
    zero_init = s0 is None
    body = functools.partial(_hgrn_body, c=c, n_chunks=tl // c, hps=hps, zero_init=zero_init)
    args, in_specs = [hq, hlf, hk, hv], [tok, tok, tok, tok]
    if zero_init:
        core = body
        body = lambda q, lf, k, v, *rest: core(q, lf, k, v, None, *rest)
    else:
        args.append(s0)
        in_specs.append(st)
    aliases = {}
    if carried is not None:
        aliases[len(args)] = 1
        body = _drop_alias_refs(body, len(args), 1)
        args.append(carried)
        in_specs.append(_ANY)
    return pl.pallas_call(
        body,
        grid=(b, nh // hps, l // tl),
        in_specs=in_specs,
        out_specs=[tok, st],
        out_shape=[jax.ShapeDtypeStruct((b, l, w), F32), jax.ShapeDtypeStruct((DEPTH, b, nh, d, d), F32)],
        scratch_shapes=[pltpu.VMEM((hps, d, d), F32), pltpu.VMEM((3, c, d), F32)],
        input_output_aliases=aliases,
        compiler_params=_cparams("parallel", "parallel", "arbitrary"),
        name="hgrn",
    )(*args)


def kernel(x_prompt, x_sample, cache_k, cache_v, cache_logf, state_hgrn, norm_ffn1, ffn1_wi, ffn1_wo,
           norm_mix, w_in, b_fgate, hgrn_lb, hgrn_gnorm, w_out, norm_ffn2, ffn2_wi, ffn2_wo, norm_final):
    bp, lp, d = x_prompt.shape
    bs, ls, _ = x_sample.shape
    mp, ms = bp * lp, bs * ls
    p_len = cache_k.shape[3]
    f, w, h8 = FOX_WIDTH, HGRN_WIDTH, FOX_HEADS
    tm_ffn, tf_ffn, tm_proj, t_attn, c_hgrn, tl_hgrn = 512, 256, 512, 256, 64, 1024

    xp = x_prompt.reshape(mp, d)
    xs = x_sample.reshape(ms, d)
    gfin = norm_final.reshape(1, d)
    g1, g2, gm = (a.reshape(DEPTH, 1, d) for a in (norm_ffn1, norm_ffn2, norm_mix))
    gn = hgrn_gnorm.reshape(DEPTH, 1, HGRN_HEAD_DIM)
    clf = cache_logf.reshape(DEPTH, bs * h8, p_len)
    cache_kt = jnp.swapaxes(cache_k, 3, 4)
    cache_vt = jnp.swapaxes(cache_v, 3, 4)

    carry_p = carry_s = None
    st_p = st_s = None
    lfs = []
    for l in range(DEPTH):
        last = l == DEPTH - 1
        wl = w_in[l]
        wm = jnp.concatenate([wl[:, :3 * f], wl[:, 3 * f + h8:]], axis=1).astype(BF16)
        wf = jnp.pad(wl[:, 3 * f:3 * f + h8], ((0, 0), (0, LANES - h8))).astype(BF16)
        bfp = jnp.pad(b_fgate[l].reshape(1, h8), ((0, 0), (0, LANES - h8)))
        wi1, wo1 = ffn1_wi[l].astype(BF16), ffn1_wo[l].astype(BF16)
        wi2, wo2 = ffn2_wi[l].astype(BF16), ffn2_wo[l].astype(BF16)
        wout = w_out[l].astype(BF16)

        xp = _ffn(xp, g1, wi1, wo1, gfin, layer=l, final_norm=False, tm=tm_ffn, tf=tf_ffn)
        xs = _ffn(xs, g1, wi1, wo1, gfin, layer=l, final_norm=False, tm=ms, tf=tf_ffn)

        (q, k, v, kh, vh, lft, lfr, hq, hlf, hk, hv, hg) = _inproj(
            xp, gm, wm, wf, bfp, hgrn_lb, carry_p, layer=l, sample=False, batch=bp, tm=tm_proj)
        carry_p = (kh, vh, lfr)
        (q_s, k_s, v_s, kh_s, vh_s, lft_s, lfr_s, hq_s, hlf_s, hk_s, hv_s, hg_s) = _inproj(
            xs, gm, wm, wf, bfp, hgrn_lb, carry_s, layer=l, sample=True, batch=bs, tm=ms)
        carry_s = (kh_s, vh_s)

        c_row, c_col = _fox_cumsum(lfr, lft.reshape(bp, lp, LANES), layer=l)
        fo = _fox_prompt(q.reshape(bp, lp, f), k.reshape(bp, lp, f), v.reshape(bp, lp, f),
                         c_col, c_row.reshape(bp, h8, lp // t_attn, t_attn), t=t_attn)
        lfs_l = lfr_s.reshape(h8, bs, ls).transpose(1, 0, 2)
        lfs.append(lfs_l)
        lfs_pad = jnp.pad(lfs_l, ((0, 0), (0, 0), (0, LANES - ls))).reshape(bs * h8, LANES)
        c_cache, cn_row, cn_col = _fox_sample_scan(clf, lfs_pad, lft_s, layer=l, t=ls)
        fo_s = _fox_sample(q_s.reshape(bs, ls, f), k_s.reshape(bs, ls, f), v_s.reshape(bs, ls, f),
                           cache_kt, cache_vt, c_cache, cn_row, cn_col.reshape(bs, ls, LANES), layer=l)
        to3 = lambda a, b_, l_: a.reshape(b_, l_, w)
        ho, st_p = _hgrn(to3(hq, bp, lp), to3(hlf, bp, lp), to3(hk, bp, lp), to3(hv, bp, lp), None, st_p,
                         layer=l, c=c_hgrn, tl=tl_hgrn, hps=1)
        ho_s, st_s = _hgrn(to3(hq_s, bs, ls), to3(hlf_s, bs, ls), to3(hk_s, bs, ls), to3(hv_s, bs, ls),
                           state_hgrn, st_s, layer=l, c=ls, tl=ls, hps=HGRN_HEADS)

        xp = _ffn(xp, g2, wi2, wo2, gfin, (fo.reshape(mp, f), ho.reshape(mp, w), hg, gn, wout),
                  layer=l, final_norm=last, tm=tm_ffn, tf=tf_ffn)
        xs = _ffn(xs, g2, wi2, wo2, gfin, (fo_s.reshape(ms, f), ho_s.reshape(ms, w), hg_s, gn, wout),
                  layer=l, final_norm=last, tm=ms, tf=tf_ffn)

    k_prompt = jnp.swapaxes(carry_p[0], 3, 4)
    v_prompt = jnp.swapaxes(carry_p[1], 3, 4)
    return (xp.reshape(bp, lp, d), xs.reshape(bs, ls, d), k_prompt, v_prompt, carry_p[2], st_p,
            carry_s[0], carry_s[1], jnp.stack(lfs), st_s)
```

```python
import functools

import jax
import jax.numpy as jnp
from jax import lax
from jax.experimental import pallas as pl
from jax.experimental.pallas import tpu as pltpu

F32 = jnp.float32
BF16 = jnp.bfloat16

D_MODEL = 1024
DEPTH = 2
FOX_HEADS = 8
FOX_HEAD_DIM = 64
FOX_WIDTH = FOX_HEADS * FOX_HEAD_DIM
HGRN_HEADS = 4
HGRN_HEAD_DIM = 128
HGRN_WIDTH = HGRN_HEADS * HGRN_HEAD_DIM
D_FF = 2816
RMS_EPS = 1e-6
FOX_SCALE = FOX_HEAD_DIM ** -0.5

LANES = 128
VMEM_LIMIT_BYTES = 56 * 1024 * 1024
HGRN_FACTOR_LIMIT = 50.0
FOX_SKIP_GAP = 106.0
FOX_NORM_SLACK = 1.001
FOX_FIRST_ROUND_BLOCKS = 3


def _cparams(*sem):
    return pltpu.CompilerParams(dimension_semantics=sem, vmem_limit_bytes=VMEM_LIMIT_BYTES)


def _rms(x, g):
    ms = jnp.mean(x * x, axis=-1, keepdims=True)
    return x * lax.rsqrt(ms + RMS_EPS) * g


def _silu(x):
    return x * jax.nn.sigmoid(x)


def _log_sigmoid(z):
    return jnp.minimum(z, 0.0) - jnp.log1p(jnp.exp(-jnp.abs(z)))


def _dot(a, b):
    return jnp.dot(a, b, preferred_element_type=F32)


def _dot_nt(a, b):
    return lax.dot_general(a, b, (((1,), (1,)), ((), ())), preferred_element_type=F32)


def _dot_tn(a, b):
    return lax.dot_general(a, b, (((0,), (0,)), ((), ())), preferred_element_type=F32)


def _scan(x, axis, period=None):
    n = x.shape[axis] if period is None else period
    idx = lax.broadcasted_iota(jnp.int32, x.shape, axis)
    if period is not None:
        idx = idx & (period - 1)
    s = 1
    while s < n:
        x = x + jnp.where(idx >= s, pltpu.roll(x, s, axis), 0.0)
        s *= 2
    return x


def _drop_alias_refs(body, n_in, n_alias):
    def wrapped(*refs):
        return body(*refs[:n_in], *refs[n_in + n_alias:])
    return wrapped


_ANY = pl.BlockSpec(memory_space=pl.ANY)


def _mix(fo_ref, ho_ref, hg_ref, gn_ref):
    gn = gn_ref[0]
    d = HGRN_HEAD_DIM
    parts = [fo_ref[...]]
    for h in range(HGRN_HEADS):
        ho = ho_ref[:, h * d:(h + 1) * d]
        parts.append((_rms(ho, gn) * _silu(hg_ref[:, h * d:(h + 1) * d])).astype(BF16))
    return jnp.concatenate(parts, axis=1)


def _ffn_body(*refs, final_norm, with_outproj, tf):
    if with_outproj:
        (x_ref, fo_ref, ho_ref, hg_ref, gn_ref, wout_ref,
         g_ref, wi_ref, wo_ref, gf_ref, o_ref, acc_ref) = refs
        x = x_ref[...] + _dot(_mix(fo_ref, ho_ref, hg_ref, gn_ref), wout_ref[...])
    else:
        x_ref, g_ref, wi_ref, wo_ref, gf_ref, o_ref, acc_ref = refs
        x = x_ref[...]
    f = wo_ref.shape[0]
    xn = _rms(x, g_ref[0]).astype(BF16)
    for c in range(f // tf):
        a = _dot(xn, wi_ref[:, c * tf:(c + 1) * tf])
        b = _dot(xn, wi_ref[:, f + c * tf:f + (c + 1) * tf])
        part = _dot((_silu(a) * b).astype(BF16), wo_ref[c * tf:(c + 1) * tf, :])
        if c == 0:
            acc_ref[...] = part
        else:
            acc_ref[...] += part
    y = x + 0.5 * acc_ref[...]
    if final_norm:
        y = _rms(y, gf_ref[...])
    o_ref[...] = y


def _ffn(x, g, wi, wo, gf, outproj=None, *, layer, final_norm, tm, tf):
    m, d = x.shape
    tok = lambda width: pl.BlockSpec((tm, width), lambda i: (i, 0))
    held = lambda a: pl.BlockSpec(a.shape, lambda i: (0,) * a.ndim, pipeline_mode=pl.Buffered(1))
    args, in_specs = [x], [tok(d)]
    if outproj is not None:
        fo, ho, hg, gn, wout = outproj
        args += [fo, ho, hg, gn, wout]
        in_specs += [tok(fo.shape[1]), tok(ho.shape[1]), tok(hg.shape[1]),
                     pl.BlockSpec((1, 1, gn.shape[2]), lambda i: (layer, 0, 0)), held(wout)]
    args += [g, wi, wo, gf]
    in_specs += [pl.BlockSpec((1, 1, d), lambda i: (layer, 0, 0)), held(wi), held(wo),
                 pl.BlockSpec((1, d), lambda i: (0, 0))]
    return pl.pallas_call(
        functools.partial(_ffn_body, final_norm=final_norm, with_outproj=outproj is not None, tf=tf),
        grid=(m // tm,),
        in_specs=in_specs,
        out_specs=tok(d),
        out_shape=jax.ShapeDtypeStruct((m, d), F32),
        scratch_shapes=[pltpu.VMEM((tm, d), F32)],
        compiler_params=_cparams("parallel"),
        name="ffn",
    )(*args)


def _inproj_body(x_ref, g_ref, wm_ref, wf_ref, bf_ref, lbp_ref,
                 q_ref, k_ref, v_ref, kh_ref, vh_ref, lft_ref, lfr_ref,
                 hq_ref, hlf_ref, hk_ref, hv_ref, hg_ref, *, layer, sample):
    h = _rms(x_ref[...], g_ref[0]).astype(BF16)
    w, hd = FOX_WIDTH, FOX_HEAD_DIM

    def sec(i):
        return _dot(h, wm_ref[:, i * w:(i + 1) * w])

    def store_heads(ref, val):
        if sample:
            for hh in range(FOX_HEADS):
                piece = val[:, hh * hd:(hh + 1) * hd]
                ref[0, :, hh] = piece.reshape(ref.shape[1], ref.shape[3], hd)
        else:
            ref[0, 0] = val.T.reshape(FOX_HEADS, hd, val.shape[0])

    q = sec(0)
    q_ref[...] = (q * FOX_SCALE).astype(BF16)
    k = sec(1)
    store_heads(kh_ref, k)
    k_ref[...] = k.astype(BF16)
    v = sec(2)
    store_heads(vh_ref, v)
    v_ref[...] = v.astype(BF16)

    lf = _log_sigmoid(_dot(h, wf_ref[...]) + bf_ref[...])
    lft_ref[...] = lf
    lfr = lf.T[:FOX_HEADS, :]
    if sample:
        lfr_ref[...] = lfr
    else:
        lfr_ref[0, 0] = lfr

    hq_ref[...] = _silu(sec(3))

    p = lbp_ref[...]
    e = jnp.exp(p - jnp.max(p, axis=0, keepdims=True))
    sm = e / jnp.sum(e, axis=0, keepdims=True)
    cs = sm[0:1]
    for r in range(1, layer + 1):
        cs = cs + sm[r:r + 1]
    lb = cs - sm[0:1]

    z = sec(4)
    a = jnp.log(lb)
    b = jnp.log1p(-lb) + _log_sigmoid(z)
    hlf_ref[...] = jnp.maximum(a, b) + jnp.log1p(jnp.exp(-jnp.abs(a - b)))
    hk_ref[...] = (1.0 - lb) * jax.nn.sigmoid(-z)
    hv_ref[...] = sec(5).astype(BF16)
    hg_ref[...] = sec(6)


def _inproj(x, g, wm, wf, bfp, lbp, carried, *, layer, sample, batch, tm):
    m, d = x.shape
    w, hd, nh = FOX_WIDTH, FOX_HEAD_DIM, FOX_HEADS
    seq = m // batch
    tiles_per_seq = seq // tm if not sample else 1
    tok = lambda width: pl.BlockSpec((tm, width), lambda i: (i, 0))
    full = lambda a: pl.BlockSpec(a.shape, lambda i: (0,) * a.ndim)
    sds = jax.ShapeDtypeStruct
    if sample:
        heads_shape = (DEPTH, batch, nh, seq, hd)
        heads_spec = pl.BlockSpec((1, batch, nh, seq, hd), lambda i: (layer, 0, 0, 0, 0))
        lfr_shape, lfr_spec = (nh, m), pl.BlockSpec((nh, m), lambda i: (0, 0))
    else:
        heads_shape = (DEPTH, batch, nh, hd, seq)
        heads_spec = pl.BlockSpec((1, 1, nh, hd, tm),
                                  lambda i: (layer, i // tiles_per_seq, 0, 0, i % tiles_per_seq))
        lfr_shape = (DEPTH, batch, nh, seq)
        lfr_spec = pl.BlockSpec((1, 1, nh, tm), lambda i: (layer, i // tiles_per_seq, 0, i % tiles_per_seq))
    out_specs = [tok(w), tok(w), tok(w), heads_spec, heads_spec, tok(LANES), lfr_spec,
                 tok(w), tok(w), tok(w), tok(w), tok(w)]
    out_shape = [sds((m, w), BF16), sds((m, w), BF16), sds((m, w), BF16),
                 sds(heads_shape, F32), sds(heads_shape, F32), sds((m, LANES), F32), sds(lfr_shape, F32),
                 sds((m, w), F32), sds((m, w), F32), sds((m, w), F32), sds((m, w), BF16), sds((m, w), F32)]
    body = functools.partial(_inproj_body, layer=layer, sample=sample)
    args = [x, g, wm, wf, bfp, lbp]
    in_specs = [tok(d), pl.BlockSpec((1, 1, d), lambda i: (layer, 0, 0)), full(wm), full(wf), full(bfp), full(lbp)]
    aliases = {}
    if carried is not None:
        out_index = (3, 4, 6)
        for n, buf in enumerate(carried):
            aliases[len(args)] = out_index[n]
            args.append(buf)
            in_specs.append(_ANY)
        body = _drop_alias_refs(body, 6, len(carried))
    return pl.pallas_call(
        body, grid=(m // tm,), in_specs=in_specs, out_specs=out_specs, out_shape=out_shape,
        input_output_aliases=aliases, compiler_params=_cparams("arbitrary"), name="inproj",
    )(*args)


def _cumsum_body(lfr_ref, lft_ref, cr_ref, cc_ref):
    cr_ref[0] = _scan(lfr_ref[0, 0], 1)
    cc_ref[0] = _scan(lft_ref[0], 0)


def _fox_cumsum(lf_rows, lf_tm, *, layer):
    _, b, h, l = lf_rows.shape
    return pl.pallas_call(
        _cumsum_body,
        grid=(b,),
        in_specs=[pl.BlockSpec((1, 1, h, l), lambda i: (layer, i, 0, 0)),
                  pl.BlockSpec((1, l, LANES), lambda i: (i, 0, 0))],
        out_specs=[pl.BlockSpec((1, h, l), lambda i: (i, 0, 0)),
                   pl.BlockSpec((1, l, LANES), lambda i: (i, 0, 0))],
        out_shape=[jax.ShapeDtypeStruct((b, h, l), F32),
                   jax.ShapeDtypeStruct((b, l, LANES), F32)],
        compiler_params=_cparams("parallel"),
        name="fox_cumsum",
    )(lf_rows, lf_tm)


def _fox_prompt_body(q_ref, k_ref, v_ref, cc_ref, cr_ref, o_ref, m_ref, l_ref, acc_ref, kn_ref, *, t, pairs):
    grp = pl.program_id(1)
    i = pl.program_id(2)
    nk = cr_ref.shape[2]
    hd = FOX_HEAD_DIM
    lane = lax.broadcasted_iota(jnp.int32, (t, LANES), 1)
    row = lax.broadcasted_iota(jnp.int32, (t, t), 0)
    col = lax.broadcasted_iota(jnp.int32, (t, t), 1)
    head_lanes = [lane < hd, lane >= hd]
    lanes_of = [slice(p * LANES, (p + 1) * LANES) for p in range(pairs)]

    @pl.when(i == 0)
    def _():
        step = 1024
        lane_s = lax.broadcasted_iota(jnp.int32, (step, LANES), 1)
        for p in range(pairs):
            def chunk(ci, mx, p=p):
                kf = k_ref[0, pl.ds(pl.multiple_of(ci * step, step), step), lanes_of[p]].astype(F32)
                k2 = kf * kf
                n0 = jnp.max(jnp.sum(jnp.where(lane_s < hd, k2, 0.0), axis=1, keepdims=True))
                n1 = jnp.max(jnp.sum(jnp.where(lane_s >= hd, k2, 0.0), axis=1, keepdims=True))
                return jnp.maximum(mx[0], n0), jnp.maximum(mx[1], n1)

            n0, n1 = lax.fori_loop(0, k_ref.shape[1] // step, chunk, (jnp.float32(0.0), jnp.float32(0.0)))
            kn_ref[2 * p] = jnp.sqrt(n0)
            kn_ref[2 * p + 1] = jnp.sqrt(n1)

    cc = cc_ref[0]
    causal = jnp.concatenate([col <= row, col <= row], axis=0)
    qs, cq, qn = [], [], []
    for p in range(pairs):
        qq = q_ref[0, :, lanes_of[p]]
        zero = jnp.zeros_like(qq)
        qs.append(jnp.concatenate([jnp.where(head_lanes[h], qq, zero) for h in range(2)], axis=0))
        head0 = 2 * (grp * pairs + p)
        cq.append([jnp.sum(jnp.where(lane == head0 + h, cc, 0.0), axis=1, keepdims=True) for h in range(2)])
        q2 = qq.astype(F32) * qq.astype(F32)
        qn.append([jnp.sqrt(jnp.sum(jnp.where(head_lanes[h], q2, 0.0), axis=1, keepdims=True))
                   for h in range(2)])

    def scores(p, j, diag):
        kk = k_ref[0, pl.ds(pl.multiple_of(j * t, t), t), lanes_of[p]]
        s = _dot_nt(qs[p], kk)
        s = jnp.concatenate(
            [s[h * t:(h + 1) * t] + (cq[p][h] - cr_ref[0, 2 * p + h, pl.ds(j, 1), :]) for h in range(2)],
            axis=0)
        return jnp.where(causal, s, -jnp.inf) if diag else s

    def values(p, j):
        return v_ref[0, pl.ds(pl.multiple_of(j * t, t), t), lanes_of[p]]

    def first_round(js):
        for p in range(pairs):
            m = l = acc = None
            for n, j in enumerate(js):
                s = scores(p, j, n == len(js) - 1)
                m_blk = jnp.max(s, axis=1, keepdims=True)
                if m is None:
                    m_next = m_blk
                    pr = jnp.exp(s - m_next)
                    l = jnp.sum(pr, axis=1, keepdims=True)
                    acc = _dot(pr.astype(BF16), values(p, j))
                else:
                    m_next = jnp.maximum(m, m_blk)
                    alpha = jnp.exp(m - m_next)
                    pr = jnp.exp(s - m_next)
                    l = alpha * l + jnp.sum(pr, axis=1, keepdims=True)
                    acc = alpha * acc + _dot(pr.astype(BF16), values(p, j))
                m = m_next
            m_ref[p] = jnp.broadcast_to(m, m_ref.shape[1:])
            l_ref[p] = jnp.broadcast_to(l, l_ref.shape[1:])
            acc_ref[p] = acc

    def block(j):
        for p in range(pairs):
            s = scores(p, j, False)
            m_prev = m_ref[p]
            m_next = jnp.maximum(m_prev, jnp.max(s, axis=1, keepdims=True))
            pr = jnp.exp(s - jnp.concatenate([m_next] * (t // LANES), axis=1))
            alpha = jnp.exp(m_prev - m_next)
            l_ref[p] = alpha * l_ref[p] + jnp.sum(pr, axis=1, keepdims=True)
            m_ref[p] = m_next
            acc_ref[p] = alpha * acc_ref[p] + _dot(pr.astype(BF16), values(p, j))

    window = FOX_FIRST_ROUND_BLOCKS

    @pl.when(i >= window - 1)
    def _():
        first_round([i - (window - 1) + n for n in range(window)])

    @pl.when(i < window - 1)
    def _():
        first_round([i])

    j_next = jnp.where(i >= window - 1, i - window, i - 1)

    jidx = lax.broadcasted_iota(jnp.int32, (nk, 1), 0)
    first_needed = jnp.full((nk, 1), nk, jnp.int32)
    for p in range(pairs):
        for h in range(2):
            gap = qn[p][h] * (kn_ref[2 * p + h] * FOX_NORM_SLACK) + cq[p][h] - m_ref[p, h * t:(h + 1) * t, :1]
            reach = jnp.max(gap, axis=0, keepdims=True)
            c_min = jnp.min(cr_ref[0, 2 * p + h], axis=1, keepdims=True)
            first_needed = jnp.minimum(first_needed, jnp.where(reach - c_min >= -FOX_SKIP_GAP, jidx, nk))
    j_min = jnp.minimum(jnp.min(first_needed), j_next + 1)

    def earlier(n, carry):
        block(j_next - n)
        return carry

    lax.fori_loop(0, j_next + 1 - j_min, earlier, 0)
    for p in range(pairs):
        o = acc_ref[p] / l_ref[p]
        o_ref[0, :, lanes_of[p]] = jnp.where(head_lanes[0], o[:t], o[t:]).astype(BF16)


def _fox_prompt(q, k, v, c_col, c_rows, *, t, pairs):
    b, l, w = q.shape
    ngrp = w // (pairs * LANES)
    nk = l // t
    wide = pairs * LANES
    return pl.pallas_call(
        functools.partial(_fox_prompt_body, t=t, pairs=pairs),
        grid=(b, ngrp, l // t),
        in_specs=[
            pl.BlockSpec((1, t, wide), lambda bi, g, i: (bi, i, g)),
            pl.BlockSpec((1, l, wide), lambda bi, g, i: (bi, 0, g)),
            pl.BlockSpec((1, l, wide), lambda bi, g, i: (bi, 0, g)),
            pl.BlockSpec((1, t, LANES), lambda bi, g, i: (bi, i, 0)),
            pl.BlockSpec((1, 2 * pairs, nk, t), lambda bi, g, i: (bi, g, 0, 0)),
        ],
        out_specs=pl.BlockSpec((1, t, wide), lambda bi, g, i: (bi, i, g)),
        out_shape=jax.ShapeDtypeStruct((b, l, w), BF16),
        scratch_shapes=[pltpu.VMEM((pairs, 2 * t, LANES), F32), pltpu.VMEM((pairs, 2 * t, LANES), F32),
                        pltpu.VMEM((pairs, 2 * t, LANES), F32), pltpu.SMEM((2 * pairs,), F32)],
        compiler_params=_cparams("parallel", "parallel", "arbitrary"),
        name="fox_prompt",
    )(q, k, v, c_col, c_rows)


def _sample_scan_body(clf_ref, lfr_ref, lft_ref, cc_ref, cnr_ref, cnc_ref, *, t):
    cc_ref[...] = _scan(clf_ref[0], 1)
    cnr_ref[...] = _scan(lfr_ref[...], 1)
    cnc_ref[...] = _scan(lft_ref[...], 0, period=t)


def _fox_sample_scan(clf, lf_row, lf_tm, *, layer, t):
    _, r, p_len = clf.shape
    full = lambda a: pl.BlockSpec(a.shape, lambda i: (0,) * a.ndim)
    return pl.pallas_call(
        functools.partial(_sample_scan_body, t=t),
        grid=(1,),
        in_specs=[pl.BlockSpec((1, r, p_len), lambda i: (layer, 0, 0)), full(lf_row), full(lf_tm)],
        out_specs=[pl.BlockSpec((r, p_len), lambda i: (0, 0)), full(lf_row), full(lf_tm)],
        out_shape=[jax.ShapeDtypeStruct((r, p_len), F32), jax.ShapeDtypeStruct(lf_row.shape, F32),
                   jax.ShapeDtypeStruct(lf_tm.shape, F32)],
        compiler_params=_cparams("arbitrary"),
        name="fox_sample_scan",
    )(clf, lf_row, lf_tm)


def _fox_sample_body(q_ref, k_ref, v_ref, ck_ref, cv_ref, cc_ref, cnr_ref, cnc_ref, o_ref):
    t = q_ref.shape[1]
    nh, p_len = ck_ref.shape[2], ck_ref.shape[4]
    hd = FOX_HEAD_DIM
    lane = lax.broadcasted_iota(jnp.int32, (t, LANES), 1)
    row = lax.broadcasted_iota(jnp.int32, (t, t), 0)
    col = lax.broadcasted_iota(jnp.int32, (t, t), 1)
    c_cache = cc_ref[...]
    cn_row = cnr_ref[...]
    cn_col = cnc_ref[0]
    outs = []
    for h in range(nh):
        grp = slice((h // 2) * LANES, (h // 2 + 1) * LANES)
        lo = (h % 2) * hd
        qq = q_ref[0, :, grp]
        qm = jnp.where((lane >= lo) & (lane < lo + hd), qq, jnp.zeros_like(qq))
        q_h = qq[:, lo:lo + hd]
        kc_t = ck_ref[0, 0, h].astype(BF16)
        vc_t = cv_ref[0, 0, h].astype(BF16)
        c_h = c_cache[h:h + 1, :]
        tot = c_h[:, p_len - 1:p_len]
        cq = jnp.sum(jnp.where(lane == h, cn_col, 0.0), axis=1, keepdims=True)
        s_c = _dot(q_h, kc_t) + (cq + (tot - c_h))
        s_s = _dot_nt(qm, k_ref[0, :, grp]) + (cq - cn_row[h:h + 1, :t])
        s_s = jnp.where(col <= row, s_s, -jnp.inf)
        m = jnp.maximum(jnp.max(s_c, axis=1, keepdims=True), jnp.max(s_s, axis=1, keepdims=True))
        p_c = jnp.exp(s_c - m)
        p_s = jnp.exp(s_s - m)
        den = jnp.sum(p_c, axis=1, keepdims=True) + jnp.sum(p_s, axis=1, keepdims=True)
        o = _dot_nt(p_c.astype(BF16), vc_t) + _dot(p_s.astype(BF16), v_ref[0, :, grp])[:, lo:lo + hd]
        outs.append(o / den)
    o_ref[0] = jnp.concatenate(outs, axis=1).astype(BF16)


def _fox_sample(q, k, v, cache_k, cache_v, c_cache, cn_row, cn_col, *, layer):
    b, t, w = q.shape
    nh, hd, p_len = cache_k.shape[2:]
    tokq = pl.BlockSpec((1, t, w), lambda bi: (bi, 0, 0))
    cache = pl.BlockSpec((1, 1, nh, hd, p_len), lambda bi: (layer, bi, 0, 0, 0))
    return pl.pallas_call(
        _fox_sample_body,
        grid=(b,),
        in_specs=[tokq, tokq, tokq, cache, cache,
                  pl.BlockSpec((nh, p_len), lambda bi: (bi, 0)),
                  pl.BlockSpec((nh, LANES), lambda bi: (bi, 0)),
                  pl.BlockSpec((1, t, LANES), lambda bi: (bi, 0, 0))],
        out_specs=tokq,
        out_shape=jax.ShapeDtypeStruct((b, t, w), BF16),
        compiler_params=_cparams("parallel"),
        name="fox_sample",
    )(q, k, v, cache_k, cache_v, c_cache, cn_row, cn_col)


def _hgrn_body(q_ref, lf_ref, k_ref, v_ref, s0_ref, o_ref, so_ref, st_ref, tmp_ref, *, c, n_chunks, hps, zero_init):
    ti = pl.program_id(2)
    d = HGRN_HEAD_DIM

    @pl.when(ti == 0)
    def _():
        for hh in range(hps):
            st_ref[hh] = jnp.zeros((d, d), F32) if zero_init else s0_ref[0, 0, hh].T

    row = lax.broadcasted_iota(jnp.int32, (c, c), 0)
    col = lax.broadcasted_iota(jnp.int32, (c, c), 1)
    rowc = lax.broadcasted_iota(jnp.int32, (c, 1), 0)
    half = c // 2
    b_all = _scan(lf_ref[0], 0, period=c)

    span = jnp.zeros((1, hps * d), F32)
    for ci in range(n_chunks):
        b = b_all[ci * c:(ci + 1) * c]
        b_mid = b[half - 1:half]
        span = jnp.maximum(span, jnp.maximum(b[0:1] - b_mid, b_mid - b[c - 1:c]))
    factorable = jnp.max(span) <= HGRN_FACTOR_LIMIT

    def carry_state(st, k, v, b):
        b_last = b[c - 1:c]
        kh = (k * jnp.exp(b_last - b)).astype(BF16)
        return st * jnp.exp(b_last) + _dot_tn(v, kh)

    @pl.when(factorable)
    def _():
        for hh in range(hps):
            hs = slice(hh * d, (hh + 1) * d)
            st = st_ref[hh]
            for ci in range(n_chunks):
                rs = slice(ci * c, (ci + 1) * c)
                q = q_ref[0, rs, hs]
                k = k_ref[0, rs, hs]
                v = v_ref[0, rs, hs]
                b = b_all[rs, hs]
                b_mid = b[half - 1:half]
                o = _dot_nt((q * jnp.exp(b)).astype(BF16), st.astype(BF16))
                qt = (q * jnp.exp(b - b_mid)).astype(BF16)
                kt = (k * jnp.exp(b_mid - b)).astype(BF16)
                sc = jnp.where(col <= row, _dot_nt(qt, kt), 0.0)
                o_ref[0, rs, hs] = o + _dot(sc.astype(BF16), v)
                st = carry_state(st, k, v, b)
            st_ref[hh] = st

    @pl.when(jnp.logical_not(factorable))
    def _():
        for hh in range(hps):
            hs = slice(hh * d, (hh + 1) * d)

            def chunk(ci, carry, hh=hh, hs=hs):
                rs = pl.ds(pl.multiple_of(ci * c, c), c)
                q = q_ref[0, rs, hs]
                k = k_ref[0, rs, hs]
                v = v_ref[0, rs, hs]
                b = _scan(lf_ref[0, rs, hs], 0)
                st = st_ref[hh]
                tmp_ref[0] = b
                tmp_ref[1] = k
                tmp_ref[2] = v.astype(F32)

                def key_row(s, acc):
                    bs = tmp_ref[0, pl.ds(s, 1), :]
                    ks = tmp_ref[1, pl.ds(s, 1), :]
                    vs = tmp_ref[2, pl.ds(s, 1), :]
                    w = jnp.exp(jnp.minimum(b - bs, 0.0))
                    a = jnp.sum(q * ks * w, axis=1, keepdims=True)
                    return acc + jnp.where(rowc >= s, a, 0.0) * vs

                o = _dot_nt((q * jnp.exp(b)).astype(BF16), st.astype(BF16))
                o_ref[0, rs, hs] = lax.fori_loop(0, c, key_row, o)
                st_ref[hh] = carry_state(st, k, v, b)
                return carry

            lax.fori_loop(0, n_chunks, chunk, 0)

    @pl.when(ti == pl.num_programs(2) - 1)
    def _():
        for hh in range(hps):
            so_ref[0, 0, hh] = st_ref[hh].T


def _hgrn(hq, hlf, hk, hv, s0, carried, *, layer, c, tl, hps):
    b, l, w = hq.shape
    d = HGRN_HEAD_DIM
    nh = w // d
    tok = pl.BlockSpec((1, tl, hps * d), lambda bi, h, ti: (bi, ti, h))
    st = pl.BlockSpec((1, 1, hps, d, d), lambda bi, h, ti: (layer, bi, h, 0, 0))
    zero_init = s0 is None
    body = functools.partial(_hgrn_body, c=c, n_chunks=tl // c, hps=hps, zero_init=zero_init)
    args, in_specs = [hq, hlf, hk, hv], [tok, tok, tok, tok]
    if zero_init:
        core = body
        body = lambda q, lf, k, v, *rest: core(q, lf, k, v, None, *rest)
    else:
        args.append(s0)
        in_specs.append(st)
    aliases = {}
    if carried is not None:
        aliases[len(args)] = 1
        body = _drop_alias_refs(body, len(args), 1)
        args.append(carried)
        in_specs.append(_ANY)
    return pl.pallas_call(
        body,
        grid=(b, nh // hps, l // tl),
        in_specs=in_specs,
        out_specs=[tok, st],
        out_shape=[jax.ShapeDtypeStruct((b, l, w), F32), jax.ShapeDtypeStruct((DEPTH, b, nh, d, d), F32)],
        scratch_shapes=[pltpu.VMEM((hps, d, d), F32), pltpu.VMEM((3, c, d), F32)],
        input_output_aliases=aliases,
        compiler_params=_cparams("parallel", "parallel", "arbitrary"),
        name="hgrn",
    )(*args)


def kernel(x_prompt, x_sample, cache_k, cache_v, cache_logf, state_hgrn, norm_ffn1, ffn1_wi, ffn1_wo,
           norm_mix, w_in, b_fgate, hgrn_lb, hgrn_gnorm, w_out, norm_ffn2, ffn2_wi, ffn2_wo, norm_final):
    bp, lp, d = x_prompt.shape
    bs, ls, _ = x_sample.shape
    mp, ms = bp * lp, bs * ls
    p_len = cache_k.shape[3]
    f, w, h8 = FOX_WIDTH, HGRN_WIDTH, FOX_HEADS
    tm_ffn, tf_ffn, tm_proj, t_attn, c_hgrn, tl_hgrn = 512, 256, 512, 256, 64, 1024

    xp = x_prompt.reshape(mp, d)
    xs = x_sample.reshape(ms, d)
    gfin = norm_final.reshape(1, d)
    g1, g2, gm = (a.reshape(DEPTH, 1, d) for a in (norm_ffn1, norm_ffn2, norm_mix))
    gn = hgrn_gnorm.reshape(DEPTH, 1, HGRN_HEAD_DIM)
    clf = cache_logf.reshape(DEPTH, bs * h8, p_len)
    cache_kt = jnp.swapaxes(cache_k, 3, 4)
    cache_vt = jnp.swapaxes(cache_v, 3, 4)

    carry_p = carry_s = None
    st_p = st_s = None
    lfs = []
    for l in range(DEPTH):
        last = l == DEPTH - 1
        wl = w_in[l]
        wm = jnp.concatenate([wl[:, :3 * f], wl[:, 3 * f + h8:]], axis=1).astype(BF16)
        wf = jnp.pad(wl[:, 3 * f:3 * f + h8], ((0, 0), (0, LANES - h8))).astype(BF16)
        bfp = jnp.pad(b_fgate[l].reshape(1, h8), ((0, 0), (0, LANES - h8)))
        wi1, wo1 = ffn1_wi[l].astype(BF16), ffn1_wo[l].astype(BF16)
        wi2, wo2 = ffn2_wi[l].astype(BF16), ffn2_wo[l].astype(BF16)
        wout = w_out[l].astype(BF16)

        xp = _ffn(xp, g1, wi1, wo1, gfin, layer=l, final_norm=False, tm=tm_ffn, tf=tf_ffn)
        xs = _ffn(xs, g1, wi1, wo1, gfin, layer=l, final_norm=False, tm=ms, tf=tf_ffn)

        (q, k, v, kh, vh, lft, lfr, hq, hlf, hk, hv, hg) = _inproj(
            xp, gm, wm, wf, bfp, hgrn_lb, carry_p, layer=l, sample=False, batch=bp, tm=tm_proj)
        carry_p = (kh, vh, lfr)
        (q_s, k_s, v_s, kh_s, vh_s, lft_s, lfr_s, hq_s, hlf_s, hk_s, hv_s, hg_s) = _inproj(
            xs, gm, wm, wf, bfp, hgrn_lb, carry_s, layer=l, sample=True, batch=bs, tm=ms)
        carry_s = (kh_s, vh_s)

        c_row, c_col = _fox_cumsum(lfr, lft.reshape(bp, lp, LANES), layer=l)
        fo = _fox_prompt(q.reshape(bp, lp, f), k.reshape(bp, lp, f), v.reshape(bp, lp, f),
                         c_col, c_row.reshape(bp, h8, lp // t_attn, t_attn), t=t_attn, pairs=2)
        lfs_l = lfr_s.reshape(h8, bs, ls).transpose(1, 0, 2)
        lfs.append(lfs_l)
        lfs_pad = jnp.pad(lfs_l, ((0, 0), (0, 0), (0, LANES - ls))).reshape(bs * h8, LANES)
        c_cache, cn_row, cn_col = _fox_sample_scan(clf, lfs_pad, lft_s, layer=l, t=ls)
        fo_s = _fox_sample(q_s.reshape(bs, ls, f), k_s.reshape(bs, ls, f), v_s.reshape(bs, ls, f),
                           cache_kt, cache_vt, c_cache, cn_row, cn_col.reshape(bs, ls, LANES), layer=l)
        to3 = lambda a, b_, l_: a.reshape(b_, l_, w)
        ho, st_p = _hgrn(to3(hq, bp, lp), to3(hlf, bp, lp), to3(hk, bp, lp), to3(hv, bp, lp), None, st_p,
                         layer=l, c=c_hgrn, tl=tl_hgrn, hps=2)
        ho_s, st_s = _hgrn(to3(hq_s, bs, ls), to3(hlf_s, bs, ls), to3(hk_s, bs, ls), to3(hv_s, bs, ls),
                           state_hgrn, st_s, layer=l, c=ls, tl=ls, hps=HGRN_HEADS)

        xp = _ffn(xp, g2, wi2, wo2, gfin, (fo.reshape(mp, f), ho.reshape(mp, w), hg, gn, wout),
                  layer=l, final_norm=last, tm=tm_ffn, tf=tf_ffn)
        xs = _ffn(xs, g2, wi2, wo2, gfin, (fo_s.reshape(ms, f), ho_s.reshape(ms, w), hg_s, gn, wout),
                  layer=l, final_norm=last, tm=ms, tf=tf_ffn)

    k_prompt = jnp.swapaxes(carry_p[0], 3, 4)
    v_prompt = jnp.swapaxes(carry_p[1], 3, 4)
    return (xp.reshape(bp, lp, d), xs.reshape(bs, ls, d), k_prompt, v_prompt, carry_p[2], st_p,
            carry_s[0], carry_s[1], jnp.stack(lfs), st_s)
```

```python
import functools

import jax
import jax.numpy as jnp
from jax import lax
from jax.experimental import pallas as pl
from jax.experimental.pallas import tpu as pltpu

F32 = jnp.float32
BF16 = jnp.bfloat16

D_MODEL = 1024
DEPTH = 2
FOX_HEADS = 8
FOX_HEAD_DIM = 64
FOX_WIDTH = FOX_HEADS * FOX_HEAD_DIM
HGRN_HEADS = 4
HGRN_HEAD_DIM = 128
HGRN_WIDTH = HGRN_HEADS * HGRN_HEAD_DIM
D_FF = 2816
RMS_EPS = 1e-6
FOX_SCALE = FOX_HEAD_DIM ** -0.5

LANES = 128
VMEM_LIMIT_BYTES = 56 * 1024 * 1024
HGRN_FACTOR_LIMIT = 50.0
FOX_SKIP_GAP = 106.0
FOX_NORM_SLACK = 1.001
FOX_FIRST_ROUND_BLOCKS = 3


def _cparams(*sem):
    return pltpu.CompilerParams(dimension_semantics=sem, vmem_limit_bytes=VMEM_LIMIT_BYTES)


def _rms(x, g):
    ms = jnp.mean(x * x, axis=-1, keepdims=True)
    return x * lax.rsqrt(ms + RMS_EPS) * g


def _silu(x):
    return x * jax.nn.sigmoid(x)


def _log_sigmoid(z):
    return jnp.minimum(z, 0.0) - jnp.log1p(jnp.exp(-jnp.abs(z)))


def _dot(a, b):
    return jnp.dot(a, b, preferred_element_type=F32)


def _dot_nt(a, b):
    return lax.dot_general(a, b, (((1,), (1,)), ((), ())), preferred_element_type=F32)


def _dot_tn(a, b):
    return lax.dot_general(a, b, (((0,), (0,)), ((), ())), preferred_element_type=F32)


def _scan(x, axis, period=None):
    n = x.shape[axis] if period is None else period
    idx = lax.broadcasted_iota(jnp.int32, x.shape, axis)
    if period is not None:
        idx = idx & (period - 1)
    s = 1
    while s < n:
        x = x + jnp.where(idx >= s, pltpu.roll(x, s, axis), 0.0)
        s *= 2
    return x


def _drop_alias_refs(body, n_in, n_alias):
    def wrapped(*refs):
        return body(*refs[:n_in], *refs[n_in + n_alias:])
    return wrapped


_ANY = pl.BlockSpec(memory_space=pl.ANY)


def _mix(fo_ref, ho_ref, hg_ref, gn_ref):
    gn = gn_ref[0]
    d = HGRN_HEAD_DIM
    parts = [fo_ref[...]]
    for h in range(HGRN_HEADS):
        ho = ho_ref[:, h * d:(h + 1) * d]
        parts.append((_rms(ho, gn) * _silu(hg_ref[:, h * d:(h + 1) * d])).astype(BF16))
    return jnp.concatenate(parts, axis=1)


def _ffn_body(*refs, final_norm, with_outproj, tf):
    if with_outproj:
        (x_ref, fo_ref, ho_ref, hg_ref, gn_ref, wout_ref,
         g_ref, wi_ref, wo_ref, gf_ref, o_ref, acc_ref) = refs
        x = x_ref[...] + _dot(_mix(fo_ref, ho_ref, hg_ref, gn_ref), wout_ref[...])
    else:
        x_ref, g_ref, wi_ref, wo_ref, gf_ref, o_ref, acc_ref = refs
        x = x_ref[...]
    f = wo_ref.shape[0]
    xn = _rms(x, g_ref[0]).astype(BF16)
    for c in range(f // tf):
        a = _dot(xn, wi_ref[:, c * tf:(c + 1) * tf])
        b = _dot(xn, wi_ref[:, f + c * tf:f + (c + 1) * tf])
        part = _dot((_silu(a) * b).astype(BF16), wo_ref[c * tf:(c + 1) * tf, :])
        if c == 0:
            acc_ref[...] = part
        else:
            acc_ref[...] += part
    y = x + 0.5 * acc_ref[...]
    if final_norm:
        y = _rms(y, gf_ref[...])
    o_ref[...] = y


def _ffn(x, g, wi, wo, gf, outproj=None, *, layer, final_norm, tm, tf):
    m, d = x.shape
    tok = lambda width: pl.BlockSpec((tm, width), lambda i: (i, 0))
    held = lambda a: pl.BlockSpec(a.shape, lambda i: (0,) * a.ndim, pipeline_mode=pl.Buffered(1))
    args, in_specs = [x], [tok(d)]
    if outproj is not None:
        fo, ho, hg, gn, wout = outproj
        args += [fo, ho, hg, gn, wout]
        in_specs += [tok(fo.shape[1]), tok(ho.shape[1]), tok(hg.shape[1]),
                     pl.BlockSpec((1, 1, gn.shape[2]), lambda i: (layer, 0, 0)), held(wout)]
    args += [g, wi, wo, gf]
    in_specs += [pl.BlockSpec((1, 1, d), lambda i: (layer, 0, 0)), held(wi), held(wo),
                 pl.BlockSpec((1, d), lambda i: (0, 0))]
    return pl.pallas_call(
        functools.partial(_ffn_body, final_norm=final_norm, with_outproj=outproj is not None, tf=tf),
        grid=(m // tm,),
        in_specs=in_specs,
        out_specs=tok(d),
        out_shape=jax.ShapeDtypeStruct((m, d), F32),
        scratch_shapes=[pltpu.VMEM((tm, d), F32)],
        compiler_params=_cparams("parallel"),
        name="ffn",
    )(*args)


def _inproj_body(x_ref, g_ref, wm_ref, wf_ref, bf_ref, lbp_ref,
                 q_ref, k_ref, v_ref, kh_ref, vh_ref, lft_ref, lfr_ref,
                 hq_ref, hlf_ref, hk_ref, hv_ref, hg_ref, *, layer, sample):
    h = _rms(x_ref[...], g_ref[0]).astype(BF16)
    w, hd = FOX_WIDTH, FOX_HEAD_DIM
    wc = 2 * LANES
    heads_per_chunk = wc // hd

    def chunks(i):
        for n in range(w // wc):
            yield n, slice(n * wc, (n + 1) * wc), _dot(h, wm_ref[:, i * w + n * wc:i * w + (n + 1) * wc])

    def store_heads(ref, n, val):
        h0 = n * heads_per_chunk
        if sample:
            for hh in range(heads_per_chunk):
                piece = val[:, hh * hd:(hh + 1) * hd]
                ref[0, :, h0 + hh] = piece.reshape(ref.shape[1], ref.shape[3], hd)
        else:
            ref[0, 0, h0:h0 + heads_per_chunk] = val.T.reshape(heads_per_chunk, hd, val.shape[0])

    for n, cols, q in chunks(0):
        q_ref[:, cols] = (q * FOX_SCALE).astype(BF16)
    for n, cols, k in chunks(1):
        store_heads(kh_ref, n, k)
        k_ref[:, cols] = k.astype(BF16)
    for n, cols, v in chunks(2):
        store_heads(vh_ref, n, v)
        v_ref[:, cols] = v.astype(BF16)

    lf = _log_sigmoid(_dot(h, wf_ref[...]) + bf_ref[...])
    lft_ref[...] = lf
    lfr = lf.T[:FOX_HEADS, :]
    if sample:
        lfr_ref[...] = lfr
    else:
        lfr_ref[0, 0] = lfr

    for n, cols, hq in chunks(3):
        hq_ref[:, cols] = _silu(hq)

    p = lbp_ref[...]
    e = jnp.exp(p - jnp.max(p, axis=0, keepdims=True))
    sm = e / jnp.sum(e, axis=0, keepdims=True)
    cs = sm[0:1]
    for r in range(1, layer + 1):
        cs = cs + sm[r:r + 1]
    lb = cs - sm[0:1]
    log_lb = jnp.log(lb)
    log_1m_lb = jnp.log1p(-lb)

    for n, cols, z in chunks(4):
        a = log_lb[:, cols]
        b = log_1m_lb[:, cols] + _log_sigmoid(z)
        hlf_ref[:, cols] = jnp.maximum(a, b) + jnp.log1p(jnp.exp(-jnp.abs(a - b)))
        hk_ref[:, cols] = (1.0 - lb[:, cols]) * jax.nn.sigmoid(-z)
    for n, cols, hv in chunks(5):
        hv_ref[:, cols] = hv.astype(BF16)
    for n, cols, hg in chunks(6):
        hg_ref[:, cols] = hg


def _inproj(x, g, wm, wf, bfp, lbp, carried, *, layer, sample, batch, tm):
    m, d = x.shape
    w, hd, nh = FOX_WIDTH, FOX_HEAD_DIM, FOX_HEADS
    seq = m // batch
    tiles_per_seq = seq // tm if not sample else 1
    tok = lambda width: pl.BlockSpec((tm, width), lambda i: (i, 0))
    full = lambda a: pl.BlockSpec(a.shape, lambda i: (0,) * a.ndim)
    sds = jax.ShapeDtypeStruct
    if sample:
        heads_shape = (DEPTH, batch, nh, seq, hd)
        heads_spec = pl.BlockSpec((1, batch, nh, seq, hd), lambda i: (layer, 0, 0, 0, 0))
        lfr_shape, lfr_spec = (nh, m), pl.BlockSpec((nh, m), lambda i: (0, 0))
    else:
        heads_shape = (DEPTH, batch, nh, hd, seq)
        heads_spec = pl.BlockSpec((1, 1, nh, hd, tm),
                                  lambda i: (layer, i // tiles_per_seq, 0, 0, i % tiles_per_seq))
        lfr_shape = (DEPTH, batch, nh, seq)
        lfr_spec = pl.BlockSpec((1, 1, nh, tm), lambda i: (layer, i // tiles_per_seq, 0, i % tiles_per_seq))
    out_specs = [tok(w), tok(w), tok(w), heads_spec, heads_spec, tok(LANES), lfr_spec,
                 tok(w), tok(w), tok(w), tok(w), tok(w)]
    out_shape = [sds((m, w), BF16), sds((m, w), BF16), sds((m, w), BF16),
                 sds(heads_shape, F32), sds(heads_shape, F32), sds((m, LANES), F32), sds(lfr_shape, F32),
                 sds((m, w), F32), sds((m, w), F32), sds((m, w), F32), sds((m, w), BF16), sds((m, w), F32)]
    body = functools.partial(_inproj_body, layer=layer, sample=sample)
    args = [x, g, wm, wf, bfp, lbp]
    in_specs = [tok(d), pl.BlockSpec((1, 1, d), lambda i: (layer, 0, 0)), full(wm), full(wf), full(bfp), full(lbp)]
    aliases = {}
    if carried is not None:
        out_index = (3, 4, 6)
        for n, buf in enumerate(carried):
            aliases[len(args)] = out_index[n]
            args.append(buf)
            in_specs.append(_ANY)
        body = _drop_alias_refs(body, 6, len(carried))
    return pl.pallas_call(
        body, grid=(m // tm,), in_specs=in_specs, out_specs=out_specs, out_shape=out_shape,
        input_output_aliases=aliases, compiler_params=_cparams("arbitrary"), name="inproj",
    )(*args)


def _cumsum_body(lfr_ref, lft_ref, cr_ref, cc_ref):
    cr_ref[0] = _scan(lfr_ref[0, 0], 1)
    cc_ref[0] = _scan(lft_ref[0], 0)


def _fox_cumsum(lf_rows, lf_tm, *, layer):
    _, b, h, l = lf_rows.shape
    return pl.pallas_call(
        _cumsum_body,
        grid=(b,),
        in_specs=[pl.BlockSpec((1, 1, h, l), lambda i: (layer, i, 0, 0)),
                  pl.BlockSpec((1, l, LANES), lambda i: (i, 0, 0))],
        out_specs=[pl.BlockSpec((1, h, l), lambda i: (i, 0, 0)),
                   pl.BlockSpec((1, l, LANES), lambda i: (i, 0, 0))],
        out_shape=[jax.ShapeDtypeStruct((b, h, l), F32),
                   jax.ShapeDtypeStruct((b, l, LANES), F32)],
        compiler_params=_cparams("parallel"),
        name="fox_cumsum",
    )(lf_rows, lf_tm)


def _fox_prompt_body(q_ref, k_ref, v_ref, cc_ref, cr_ref, o_ref, m_ref, l_ref, acc_ref, kn_ref, *, t, pairs):
    grp = pl.program_id(1)
    i = pl.program_id(2)
    nk = cr_ref.shape[2]
    hd = FOX_HEAD_DIM
    lane = lax.broadcasted_iota(jnp.int32, (t, LANES), 1)
    row = lax.broadcasted_iota(jnp.int32, (t, t), 0)
    col = lax.broadcasted_iota(jnp.int32, (t, t), 1)
    head_lanes = [lane < hd, lane >= hd]
    lanes_of = [slice(p * LANES, (p + 1) * LANES) for p in range(pairs)]

    @pl.when(i == 0)
    def _():
        step = 1024
        lane_s = lax.broadcasted_iota(jnp.int32, (step, LANES), 1)
        for p in range(pairs):
            def chunk(ci, mx, p=p):
                kf = k_ref[0, pl.ds(pl.multiple_of(ci * step, step), step), lanes_of[p]].astype(F32)
                k2 = kf * kf
                n0 = jnp.max(jnp.sum(jnp.where(lane_s < hd, k2, 0.0), axis=1, keepdims=True))
                n1 = jnp.max(jnp.sum(jnp.where(lane_s >= hd, k2, 0.0), axis=1, keepdims=True))
                return jnp.maximum(mx[0], n0), jnp.maximum(mx[1], n1)

            n0, n1 = lax.fori_loop(0, k_ref.shape[1] // step, chunk, (jnp.float32(0.0), jnp.float32(0.0)))
            kn_ref[2 * p] = jnp.sqrt(n0)
            kn_ref[2 * p + 1] = jnp.sqrt(n1)

    cc = cc_ref[0]
    causal = jnp.concatenate([col <= row, col <= row], axis=0)
    qs, cq, qn = [], [], []
    for p in range(pairs):
        qq = q_ref[0, :, lanes_of[p]]
        zero = jnp.zeros_like(qq)
        qs.append(jnp.concatenate([jnp.where(head_lanes[h], qq, zero) for h in range(2)], axis=0))
        head0 = 2 * (grp * pairs + p)
        cq.append([jnp.sum(jnp.where(lane == head0 + h, cc, 0.0), axis=1, keepdims=True) for h in range(2)])
        q2 = qq.astype(F32) * qq.astype(F32)
        qn.append([jnp.sqrt(jnp.sum(jnp.where(head_lanes[h], q2, 0.0), axis=1, keepdims=True))
                   for h in range(2)])

    def scores(p, j, diag):
        kk = k_ref[0, pl.ds(pl.multiple_of(j * t, t), t), lanes_of[p]]
        s = _dot_nt(qs[p], kk)
        s = jnp.concatenate(
            [s[h * t:(h + 1) * t] + (cq[p][h] - cr_ref[0, 2 * p + h, pl.ds(j, 1), :]) for h in range(2)],
            axis=0)
        return jnp.where(causal, s, -jnp.inf) if diag else s

    def values(p, j):
        return v_ref[0, pl.ds(pl.multiple_of(j * t, t), t), lanes_of[p]]

    def first_round(js):
        for p in range(pairs):
            m = l = acc = None
            for n, j in enumerate(js):
                s = scores(p, j, n == len(js) - 1)
                m_blk = jnp.max(s, axis=1, keepdims=True)
                if m is None:
                    m_next = m_blk
                    pr = jnp.exp(s - m_next)
                    l = jnp.sum(pr, axis=1, keepdims=True)
                    acc = _dot(pr.astype(BF16), values(p, j))
                else:
                    m_next = jnp.maximum(m, m_blk)
                    alpha = jnp.exp(m - m_next)
                    pr = jnp.exp(s - m_next)
                    l = alpha * l + jnp.sum(pr, axis=1, keepdims=True)
                    acc = alpha * acc + _dot(pr.astype(BF16), values(p, j))
                m = m_next
            m_ref[p] = jnp.broadcast_to(m, m_ref.shape[1:])
            l_ref[p] = jnp.broadcast_to(l, l_ref.shape[1:])
            acc_ref[p] = acc

    def block(j):
        for p in range(pairs):
            s = scores(p, j, False)
            m_prev = m_ref[p]
            m_next = jnp.maximum(m_prev, jnp.max(s, axis=1, keepdims=True))
            pr = jnp.exp(s - jnp.concatenate([m_next] * (t // LANES), axis=1))
            alpha = jnp.exp(m_prev - m_next)
            l_ref[p] = alpha * l_ref[p] + jnp.sum(pr, axis=1, keepdims=True)
            m_ref[p] = m_next
            acc_ref[p] = alpha * acc_ref[p] + _dot(pr.astype(BF16), values(p, j))

    window = FOX_FIRST_ROUND_BLOCKS

    @pl.when(i >= window - 1)
    def _():
        first_round([i - (window - 1) + n for n in range(window)])

    @pl.when(i < window - 1)
    def _():
        first_round([i])

    j_next = jnp.where(i >= window - 1, i - window, i - 1)

    jidx = lax.broadcasted_iota(jnp.int32, (nk, 1), 0)
    first_needed = jnp.full((nk, 1), nk, jnp.int32)
    for p in range(pairs):
        for h in range(2):
            gap = qn[p][h] * (kn_ref[2 * p + h] * FOX_NORM_SLACK) + cq[p][h] - m_ref[p, h * t:(h + 1) * t, :1]
            reach = jnp.max(gap, axis=0, keepdims=True)
            c_min = jnp.min(cr_ref[0, 2 * p + h], axis=1, keepdims=True)
            first_needed = jnp.minimum(first_needed, jnp.where(reach - c_min >= -FOX_SKIP_GAP, jidx, nk))
    j_min = jnp.minimum(jnp.min(first_needed), j_next + 1)

    def earlier(n, carry):
        block(j_next - n)
        return carry

    lax.fori_loop(0, j_next + 1 - j_min, earlier, 0)
    for p in range(pairs):
        o = acc_ref[p] / l_ref[p]
        o_ref[0, :, lanes_of[p]] = jnp.where(head_lanes[0], o[:t], o[t:]).astype(BF16)


def _fox_prompt(q, k, v, c_col, c_rows, *, t, pairs):
    b, l, w = q.shape
    ngrp = w // (pairs * LANES)
    nk = l // t
    wide = pairs * LANES
    return pl.pallas_call(
        functools.partial(_fox_prompt_body, t=t, pairs=pairs),
        grid=(b, ngrp, l // t),
        in_specs=[
            pl.BlockSpec((1, t, wide), lambda bi, g, i: (bi, i, g)),
            pl.BlockSpec((1, l, wide), lambda bi, g, i: (bi, 0, g)),
            pl.BlockSpec((1, l, wide), lambda bi, g, i: (bi, 0, g)),
            pl.BlockSpec((1, t, LANES), lambda bi, g, i: (bi, i, 0)),
            pl.BlockSpec((1, 2 * pairs, nk, t), lambda bi, g, i: (bi, g, 0, 0)),
        ],
        out_specs=pl.BlockSpec((1, t, wide), lambda bi, g, i: (bi, i, g)),
        out_shape=jax.ShapeDtypeStruct((b, l, w), BF16),
        scratch_shapes=[pltpu.VMEM((pairs, 2 * t, LANES), F32), pltpu.VMEM((pairs, 2 * t, LANES), F32),
                        pltpu.VMEM((pairs, 2 * t, LANES), F32), pltpu.SMEM((2 * pairs,), F32)],
        compiler_params=_cparams("parallel", "parallel", "arbitrary"),
        name="fox_prompt",
    )(q, k, v, c_col, c_rows)


def _sample_scan_body(clf_ref, lfr_ref, lft_ref, cc_ref, cnr_ref, cnc_ref, *, t):
    cc_ref[...] = _scan(clf_ref[0], 1)
    cnr_ref[...] = _scan(lfr_ref[...], 1)
    cnc_ref[...] = _scan(lft_ref[...], 0, period=t)


def _fox_sample_scan(clf, lf_row, lf_tm, *, layer, t):
    _, r, p_len = clf.shape
    full = lambda a: pl.BlockSpec(a.shape, lambda i: (0,) * a.ndim)
    return pl.pallas_call(
        functools.partial(_sample_scan_body, t=t),
        grid=(1,),
        in_specs=[pl.BlockSpec((1, r, p_len), lambda i: (layer, 0, 0)), full(lf_row), full(lf_tm)],
        out_specs=[pl.BlockSpec((r, p_len), lambda i: (0, 0)), full(lf_row), full(lf_tm)],
        out_shape=[jax.ShapeDtypeStruct((r, p_len), F32), jax.ShapeDtypeStruct(lf_row.shape, F32),
                   jax.ShapeDtypeStruct(lf_tm.shape, F32)],
        compiler_params=_cparams("arbitrary"),
        name="fox_sample_scan",
    )(clf, lf_row, lf_tm)


def _fox_sample_body(q_ref, k_ref, v_ref, ck_ref, cv_ref, cc_ref, cnr_ref, cnc_ref, o_ref):
    t = q_ref.shape[1]
    nh, p_len = ck_ref.shape[2], ck_ref.shape[4]
    hd = FOX_HEAD_DIM
    lane = lax.broadcasted_iota(jnp.int32, (t, LANES), 1)
    row = lax.broadcasted_iota(jnp.int32, (t, t), 0)
    col = lax.broadcasted_iota(jnp.int32, (t, t), 1)
    c_cache = cc_ref[...]
    cn_row = cnr_ref[...]
    cn_col = cnc_ref[0]
    outs = []
    for h in range(nh):
        grp = slice((h // 2) * LANES, (h // 2 + 1) * LANES)
        lo = (h % 2) * hd
        qq = q_ref[0, :, grp]
        qm = jnp.where((lane >= lo) & (lane < lo + hd), qq, jnp.zeros_like(qq))
        q_h = qq[:, lo:lo + hd]
        kc_t = ck_ref[0, 0, h].astype(BF16)
        vc_t = cv_ref[0, 0, h].astype(BF16)
        c_h = c_cache[h:h + 1, :]
        tot = c_h[:, p_len - 1:p_len]
        cq = jnp.sum(jnp.where(lane == h, cn_col, 0.0), axis=1, keepdims=True)
        s_c = _dot(q_h, kc_t) + (cq + (tot - c_h))
        s_s = _dot_nt(qm, k_ref[0, :, grp]) + (cq - cn_row[h:h + 1, :t])
        s_s = jnp.where(col <= row, s_s, -jnp.inf)
        m = jnp.maximum(jnp.max(s_c, axis=1, keepdims=True), jnp.max(s_s, axis=1, keepdims=True))
        p_c = jnp.exp(s_c - m)
        p_s = jnp.exp(s_s - m)
        den = jnp.sum(p_c, axis=1, keepdims=True) + jnp.sum(p_s, axis=1, keepdims=True)
        o = _dot_nt(p_c.astype(BF16), vc_t) + _dot(p_s.astype(BF16), v_ref[0, :, grp])[:, lo:lo + hd]
        outs.append(o / den)
    o_ref[0] = jnp.concatenate(outs, axis=1).astype(BF16)


def _fox_sample(q, k, v, cache_k, cache_v, c_cache, cn_row, cn_col, *, layer):
    b, t, w = q.shape
    nh, hd, p_len = cache_k.shape[2:]
    tokq = pl.BlockSpec((1, t, w), lambda bi: (bi, 0, 0))
    cache = pl.BlockSpec((1, 1, nh, hd, p_len), lambda bi: (layer, bi, 0, 0, 0))
    return pl.pallas_call(
        _fox_sample_body,
        grid=(b,),
        in_specs=[tokq, tokq, tokq, cache, cache,
                  pl.BlockSpec((nh, p_len), lambda bi: (bi, 0)),
                  pl.BlockSpec((nh, LANES), lambda bi: (bi, 0)),
                  pl.BlockSpec((1, t, LANES), lambda bi: (bi, 0, 0))],
        out_specs=tokq,
        out_shape=jax.ShapeDtypeStruct((b, t, w), BF16),
        compiler_params=_cparams("parallel"),
        name="fox_sample",
    )(q, k, v, cache_k, cache_v, c_cache, cn_row, cn_col)


def _hgrn_body(q_ref, lf_ref, k_ref, v_ref, s0_ref, o_ref, so_ref, st_ref, tmp_ref, *, c, n_chunks, hps, zero_init):
    ti = pl.program_id(2)
    d = HGRN_HEAD_DIM

    @pl.when(ti == 0)
    def _():
        for hh in range(hps):
            st_ref[hh] = jnp.zeros((d, d), F32) if zero_init else s0_ref[0, 0, hh].T

    row = lax.broadcasted_iota(jnp.int32, (c, c), 0)
    col = lax.broadcasted_iota(jnp.int32, (c, c), 1)
    rowc = lax.broadcasted_iota(jnp.int32, (c, 1), 0)
    half = c // 2
    b_all = _scan(lf_ref[0], 0, period=c)

    span = jnp.zeros((1, hps * d), F32)
    for ci in range(n_chunks):
        b = b_all[ci * c:(ci + 1) * c]
        b_mid = b[half - 1:half]
        span = jnp.maximum(span, jnp.maximum(b[0:1] - b_mid, b_mid - b[c - 1:c]))
    factorable = jnp.max(span) <= HGRN_FACTOR_LIMIT

    def carry_state(st, k, v, b):
        b_last = b[c - 1:c]
        kh = (k * jnp.exp(b_last - b)).astype(BF16)
        return st * jnp.exp(b_last) + _dot_tn(v, kh)

    @pl.when(factorable)
    def _():
        units = [(hh, ci) for hh in range(hps) for ci in range(n_chunks)]
        tile_of = {u: (slice(u[1] * c, (u[1] + 1) * c), slice(u[0] * d, (u[0] + 1) * d)) for u in units}
        sc, ds, qe, grow = {}, {}, {}, {}
        for u in units:
            rs, hs = tile_of[u]
            q = q_ref[0, rs, hs]
            k = k_ref[0, rs, hs]
            b = b_all[rs, hs]
            b_mid = b[half - 1:half]
            b_last = b[c - 1:c]
            qe[u] = (q * jnp.exp(b)).astype(BF16)
            grow[u] = jnp.exp(b_last)
            qt = (q * jnp.exp(b - b_mid)).astype(BF16)
            kt = (k * jnp.exp(b_mid - b)).astype(BF16)
            kh = (k * jnp.exp(b_last - b)).astype(BF16)
            sc[u] = _dot_nt(qt, kt)
            ds[u] = _dot_tn(v_ref[0, rs, hs], kh)
        st_in = {}
        for hh in range(hps):
            st = st_ref[hh]
            for ci in range(n_chunks):
                st_in[(hh, ci)] = st.astype(BF16)
                st = st * grow[(hh, ci)] + ds[(hh, ci)]
            st_ref[hh] = st
        for u in units:
            rs, hs = tile_of[u]
            intra = _dot(jnp.where(col <= row, sc[u], 0.0).astype(BF16), v_ref[0, rs, hs])
            o_ref[0, rs, hs] = _dot_nt(qe[u], st_in[u]) + intra

    @pl.when(jnp.logical_not(factorable))
    def _():
        for hh in range(hps):
            hs = slice(hh * d, (hh + 1) * d)

            def chunk(ci, carry, hh=hh, hs=hs):
                rs = pl.ds(pl.multiple_of(ci * c, c), c)
                q = q_ref[0, rs, hs]
                k = k_ref[0, rs, hs]
                v = v_ref[0, rs, hs]
                b = _scan(lf_ref[0, rs, hs], 0)
                st = st_ref[hh]
                tmp_ref[0] = b
                tmp_ref[1] = k
                tmp_ref[2] = v.astype(F32)

                def key_row(s, acc):
                    bs = tmp_ref[0, pl.ds(s, 1), :]
                    ks = tmp_ref[1, pl.ds(s, 1), :]
                    vs = tmp_ref[2, pl.ds(s, 1), :]
                    w = jnp.exp(jnp.minimum(b - bs, 0.0))
                    a = jnp.sum(q * ks * w, axis=1, keepdims=True)
                    return acc + jnp.where(rowc >= s, a, 0.0) * vs

                o = _dot_nt((q * jnp.exp(b)).astype(BF16), st.astype(BF16))
                o_ref[0, rs, hs] = lax.fori_loop(0, c, key_row, o)
                st_ref[hh] = carry_state(st, k, v, b)
                return carry

            lax.fori_loop(0, n_chunks, chunk, 0)

    @pl.when(ti == pl.num_programs(2) - 1)
    def _():
        for hh in range(hps):
            so_ref[0, 0, hh] = st_ref[hh].T


def _hgrn(hq, hlf, hk, hv, s0, carried, *, layer, c, tl, hps):
    b, l, w = hq.shape
    d = HGRN_HEAD_DIM
    nh = w // d
    tok = pl.BlockSpec((1, tl, hps * d), lambda bi, h, ti: (bi, ti, h))
    st = pl.BlockSpec((1, 1, hps, d, d), lambda bi, h, ti: (layer, bi, h, 0, 0))
    zero_init = s0 is None
    body = functools.partial(_hgrn_body, c=c, n_chunks=tl // c, hps=hps, zero_init=zero_init)
    args, in_specs = [hq, hlf, hk, hv], [tok, tok, tok, tok]
    if zero_init:
        core = body
        body = lambda q, lf, k, v, *rest: core(q, lf, k, v, None, *rest)
    else:
        args.append(s0)
        in_specs.append(st)
    aliases = {}
    if carried is not None:
        aliases[len(args)] = 1
        body = _drop_alias_refs(body, len(args), 1)
        args.append(carried)
        in_specs.append(_ANY)
    return pl.pallas_call(
        body,
        grid=(b, nh // hps, l // tl),
        in_specs=in_specs,
        out_specs=[tok, st],
        out_shape=[jax.ShapeDtypeStruct((b, l, w), F32), jax.ShapeDtypeStruct((DEPTH, b, nh, d, d), F32)],
        scratch_shapes=[pltpu.VMEM((hps, d, d), F32), pltpu.VMEM((3, c, d), F32)],
        input_output_aliases=aliases,
        compiler_params=_cparams("parallel", "parallel", "arbitrary"),
        name="hgrn",
    )(*args)


def kernel(x_prompt, x_sample, cache_k, cache_v, cache_logf, state_hgrn, norm_ffn1, ffn1_wi, ffn1_wo,
           norm_mix, w_in, b_fgate, hgrn_lb, hgrn_gnorm, w_out, norm_ffn2, ffn2_wi, ffn2_wo, norm_final):
    bp, lp, d = x_prompt.shape
    bs, ls, _ = x_sample.shape
    mp, ms = bp * lp, bs * ls
    p_len = cache_k.shape[3]
    f, w, h8 = FOX_WIDTH, HGRN_WIDTH, FOX_HEADS
    tm_ffn, tf_ffn, tm_proj, t_attn, c_hgrn, tl_hgrn = 512, 256, 512, 256, 64, 1024

    xp = x_prompt.reshape(mp, d)
    xs = x_sample.reshape(ms, d)
    gfin = norm_final.reshape(1, d)
    g1, g2, gm = (a.reshape(DEPTH, 1, d) for a in (norm_ffn1, norm_ffn2, norm_mix))
    gn = hgrn_gnorm.reshape(DEPTH, 1, HGRN_HEAD_DIM)
    clf = cache_logf.reshape(DEPTH, bs * h8, p_len)
    cache_kt = jnp.swapaxes(cache_k, 3, 4)
    cache_vt = jnp.swapaxes(cache_v, 3, 4)

    carry_p = carry_s = None
    st_p = st_s = None
    lfs = []
    for l in range(DEPTH):
        last = l == DEPTH - 1
        wl = w_in[l]
        wm = jnp.concatenate([wl[:, :3 * f], wl[:, 3 * f + h8:]], axis=1).astype(BF16)
        wf = jnp.pad(wl[:, 3 * f:3 * f + h8], ((0, 0), (0, LANES - h8))).astype(BF16)
        bfp = jnp.pad(b_fgate[l].reshape(1, h8), ((0, 0), (0, LANES - h8)))
        wi1, wo1 = ffn1_wi[l].astype(BF16), ffn1_wo[l].astype(BF16)
        wi2, wo2 = ffn2_wi[l].astype(BF16), ffn2_wo[l].astype(BF16)
        wout = w_out[l].astype(BF16)

        xp = _ffn(xp, g1, wi1, wo1, gfin, layer=l, final_norm=False, tm=tm_ffn, tf=tf_ffn)
        xs = _ffn(xs, g1, wi1, wo1, gfin, layer=l, final_norm=False, tm=ms, tf=tf_ffn)

        (q, k, v, kh, vh, lft, lfr, hq, hlf, hk, hv, hg) = _inproj(
            xp, gm, wm, wf, bfp, hgrn_lb, carry_p, layer=l, sample=False, batch=bp, tm=tm_proj)
        carry_p = (kh, vh, lfr)
        (q_s, k_s, v_s, kh_s, vh_s, lft_s, lfr_s, hq_s, hlf_s, hk_s, hv_s, hg_s) = _inproj(
            xs, gm, wm, wf, bfp, hgrn_lb, carry_s, layer=l, sample=True, batch=bs, tm=ms)
        carry_s = (kh_s, vh_s)

        c_row, c_col = _fox_cumsum(lfr, lft.reshape(bp, lp, LANES), layer=l)
        fo = _fox_prompt(q.reshape(bp, lp, f), k.reshape(bp, lp, f), v.reshape(bp, lp, f),
                         c_col, c_row.reshape(bp, h8, lp // t_attn, t_attn), t=t_attn, pairs=2)
        lfs_l = lfr_s.reshape(h8, bs, ls).transpose(1, 0, 2)
        lfs.append(lfs_l)
        lfs_pad = jnp.pad(lfs_l, ((0, 0), (0, 0), (0, LANES - ls))).reshape(bs * h8, LANES)
        c_cache, cn_row, cn_col = _fox_sample_scan(clf, lfs_pad, lft_s, layer=l, t=ls)
        fo_s = _fox_sample(q_s.reshape(bs, ls, f), k_s.reshape(bs, ls, f), v_s.reshape(bs, ls, f),
                           cache_kt, cache_vt, c_cache, cn_row, cn_col.reshape(bs, ls, LANES), layer=l)
        to3 = lambda a, b_, l_: a.reshape(b_, l_, w)
        ho, st_p = _hgrn(to3(hq, bp, lp), to3(hlf, bp, lp), to3(hk, bp, lp), to3(hv, bp, lp), None, st_p,
                         layer=l, c=c_hgrn, tl=tl_hgrn, hps=2)
        ho_s, st_s = _hgrn(to3(hq_s, bs, ls), to3(hlf_s, bs, ls), to3(hk_s, bs, ls), to3(hv_s, bs, ls),
                           state_hgrn, st_s, layer=l, c=ls, tl=ls, hps=HGRN_HEADS)

        xp = _ffn(xp, g2, wi2, wo2, gfin, (fo.reshape(mp, f), ho.reshape(mp, w), hg, gn, wout),
                  layer=l, final_norm=last, tm=tm_ffn, tf=tf_ffn)
        xs = _ffn(xs, g2, wi2, wo2, gfin, (fo_s.reshape(ms, f), ho_s.reshape(ms, w), hg_s, gn, wout),
                  layer=l, final_norm=last, tm=ms, tf=tf_ffn)

    k_prompt = jnp.swapaxes(carry_p[0], 3, 4)
    v_prompt = jnp.swapaxes(carry_p[1], 3, 4)
    return (xp.reshape(bp, lp, d), xs.reshape(bs, ls, d), k_prompt, v_prompt, carry_p[2], st_p,
            carry_s[0], carry_s[1], jnp.stack(lfs), st_s)
```

```python
import functools

import jax
import jax.numpy as jnp
from jax import lax
from jax.experimental import pallas as pl
from jax.experimental.pallas import tpu as pltpu

F32 = jnp.float32
BF16 = jnp.bfloat16

D_MODEL = 1024
DEPTH = 2
FOX_HEADS = 8
FOX_HEAD_DIM = 64
FOX_WIDTH = FOX_HEADS * FOX_HEAD_DIM
HGRN_HEADS = 4
HGRN_HEAD_DIM = 128
HGRN_WIDTH = HGRN_HEADS * HGRN_HEAD_DIM
D_FF = 2816
RMS_EPS = 1e-6
FOX_SCALE = FOX_HEAD_DIM ** -0.5

LANES = 128
VMEM_LIMIT_BYTES = 56 * 1024 * 1024
HGRN_FACTOR_LIMIT = 50.0
FOX_SKIP_GAP = 106.0
FOX_NORM_SLACK = 1.001
FOX_FIRST_ROUND_BLOCKS = 3


def _cparams(*sem):
    return pltpu.CompilerParams(dimension_semantics=sem, vmem_limit_bytes=VMEM_LIMIT_BYTES)


def _rms(x, g):
    ms = jnp.mean(x * x, axis=-1, keepdims=True)
    return x * lax.rsqrt(ms + RMS_EPS) * g


def _silu(x):
    return x * jax.nn.sigmoid(x)


def _log1p_exp_neg(d):
    return jnp.log(1.0 + jnp.exp(-d))


def _log_sigmoid(z):
    return jnp.minimum(z, 0.0) - _log1p_exp_neg(jnp.abs(z))


def _gate_terms(z):
    e = jnp.exp(-jnp.abs(z))
    t = 1.0 + e
    return jnp.minimum(z, 0.0) - jnp.log(t), jnp.where(z >= 0.0, e, 1.0) / t


def _dot(a, b):
    return jnp.dot(a, b, preferred_element_type=F32)


def _dot_nt(a, b):
    return lax.dot_general(a, b, (((1,), (1,)), ((), ())), preferred_element_type=F32)


def _dot_tn(a, b):
    return lax.dot_general(a, b, (((0,), (0,)), ((), ())), preferred_element_type=F32)


def _scan(x, axis, period=None):
    n = x.shape[axis] if period is None else period
    idx = lax.broadcasted_iota(jnp.int32, x.shape, axis)
    if period is not None:
        idx = idx & (period - 1)
    s = 1
    while s < n:
        x = x + jnp.where(idx >= s, pltpu.roll(x, s, axis), 0.0)
        s *= 2
    return x


def _drop_alias_refs(body, n_in, n_alias):
    def wrapped(*refs):
        return body(*refs[:n_in], *refs[n_in + n_alias:])
    return wrapped


_ANY = pl.BlockSpec(memory_space=pl.ANY)


def _cast_body(w_ref, o_ref):
    o_ref[...] = w_ref[0].astype(o_ref.dtype)


def _layer_bf16(w, *, layer, steps):
    _, rows, cols = w.shape
    rb = rows // steps
    return pl.pallas_call(
        _cast_body,
        grid=(steps,),
        in_specs=[pl.BlockSpec((1, rb, cols), lambda i: (layer, i, 0))],
        out_specs=pl.BlockSpec((rb, cols), lambda i: (i, 0)),
        out_shape=jax.ShapeDtypeStruct((rows, cols), BF16),
        compiler_params=_cparams("parallel"),
        name="to_bf16",
    )(w)


def _mix(fo_ref, ho_ref, hg_ref, gn_ref):
    gn = gn_ref[0]
    d = HGRN_HEAD_DIM
    parts = [fo_ref[...]]
    for h in range(HGRN_HEADS):
        ho = ho_ref[:, h * d:(h + 1) * d]
        parts.append((_rms(ho, gn) * _silu(hg_ref[:, h * d:(h + 1) * d])).astype(BF16))
    return jnp.concatenate(parts, axis=1)


def _ffn_body(*refs, final_norm, with_outproj, tf):
    if with_outproj:
        (x_ref, fo_ref, ho_ref, hg_ref, gn_ref, wout_ref,
         g_ref, wi_ref, wo_ref, gf_ref, o_ref, acc_ref) = refs
        x = x_ref[...] + _dot(_mix(fo_ref, ho_ref, hg_ref, gn_ref), wout_ref[...])
    else:
        x_ref, g_ref, wi_ref, wo_ref, gf_ref, o_ref, acc_ref = refs
        x = x_ref[...]
    f = wo_ref.shape[0]
    xn = _rms(x, g_ref[0]).astype(BF16)
    for c in range(f // tf):
        a = _dot(xn, wi_ref[:, c * tf:(c + 1) * tf])
        b = _dot(xn, wi_ref[:, f + c * tf:f + (c + 1) * tf])
        part = _dot((_silu(a) * b).astype(BF16), wo_ref[c * tf:(c + 1) * tf, :])
        if c == 0:
            acc_ref[...] = part
        else:
            acc_ref[...] += part
    y = x + 0.5 * acc_ref[...]
    if final_norm:
        y = _rms(y, gf_ref[...])
    o_ref[...] = y


def _ffn(x, g, wi, wo, gf, outproj=None, *, layer, final_norm, tm, tf):
    m, d = x.shape
    tok = lambda width: pl.BlockSpec((tm, width), lambda i: (i, 0))
    held = lambda a: pl.BlockSpec(a.shape, lambda i: (0,) * a.ndim, pipeline_mode=pl.Buffered(1))
    args, in_specs = [x], [tok(d)]
    if outproj is not None:
        fo, ho, hg, gn, wout = outproj
        args += [fo, ho, hg, gn, wout]
        in_specs += [tok(fo.shape[1]), tok(ho.shape[1]), tok(hg.shape[1]),
                     pl.BlockSpec((1, 1, gn.shape[2]), lambda i: (layer, 0, 0)), held(wout)]
    args += [g, wi, wo, gf]
    in_specs += [pl.BlockSpec((1, 1, d), lambda i: (layer, 0, 0)), held(wi), held(wo),
                 pl.BlockSpec((1, d), lambda i: (0, 0))]
    return pl.pallas_call(
        functools.partial(_ffn_body, final_norm=final_norm, with_outproj=outproj is not None, tf=tf),
        grid=(m // tm,),
        in_specs=in_specs,
        out_specs=tok(d),
        out_shape=jax.ShapeDtypeStruct((m, d), F32),
        scratch_shapes=[pltpu.VMEM((tm, d), F32)],
        compiler_params=_cparams("parallel"),
        name="ffn",
    )(*args)


def _inproj_body(x_ref, g_ref, wm_ref, wf_ref, bf_ref, lbp_ref,
                 q_ref, k_ref, v_ref, kh_ref, vh_ref, lft_ref, lfr_ref,
                 hq_ref, hlf_ref, hk_ref, hv_ref, hg_ref, *, layer, sample):
    h = _rms(x_ref[...], g_ref[0]).astype(BF16)
    w, hd = FOX_WIDTH, FOX_HEAD_DIM
    wc = 2 * LANES
    heads_per_chunk = wc // hd

    def chunks(i):
        for n in range(w // wc):
            yield n, slice(n * wc, (n + 1) * wc), _dot(h, wm_ref[:, i * w + n * wc:i * w + (n + 1) * wc])

    def store_heads(ref, n, val):
        h0 = n * heads_per_chunk
        if sample:
            for hh in range(heads_per_chunk):
                piece = val[:, hh * hd:(hh + 1) * hd]
                ref[0, :, h0 + hh] = piece.reshape(ref.shape[1], ref.shape[3], hd)
        else:
            ref[0, 0, h0:h0 + heads_per_chunk] = val.T.reshape(heads_per_chunk, hd, val.shape[0])

    p = lbp_ref[...]
    e = jnp.exp(p - jnp.max(p, axis=0, keepdims=True))
    sm = e / jnp.sum(e, axis=0, keepdims=True)
    cs = sm[0:1]
    for r in range(1, layer + 1):
        cs = cs + sm[r:r + 1]
    lb = cs - sm[0:1]
    log_lb = jnp.log(lb)
    log_1m_lb = jnp.log1p(-lb)

    for n, cols, z in chunks(4):
        log_sig, sig_neg = _gate_terms(z)
        a = log_lb[:, cols]
        b = log_1m_lb[:, cols] + log_sig
        hlf_ref[:, cols] = jnp.maximum(a, b) + _log1p_exp_neg(jnp.abs(a - b))
        hk_ref[:, cols] = (1.0 - lb[:, cols]) * sig_neg
    for n, cols, hq in chunks(3):
        hq_ref[:, cols] = _silu(hq)

    lf = _log_sigmoid(_dot(h, wf_ref[...]) + bf_ref[...])
    lft_ref[...] = lf
    lfr = lf.T[:FOX_HEADS, :]
    if sample:
        lfr_ref[...] = lfr
    else:
        lfr_ref[0, 0] = lfr

    for n, cols, q in chunks(0):
        q_ref[:, cols] = (q * FOX_SCALE).astype(BF16)
    for n, cols, k in chunks(1):
        store_heads(kh_ref, n, k)
        k_ref[:, cols] = k.astype(BF16)
    for n, cols, v in chunks(2):
        store_heads(vh_ref, n, v)
        v_ref[:, cols] = v.astype(BF16)
    for n, cols, hv in chunks(5):
        hv_ref[:, cols] = hv.astype(BF16)
    for n, cols, hg in chunks(6):
        hg_ref[:, cols] = hg


def _inproj(x, g, wm, wf, bfp, lbp, carried, *, layer, sample, batch, tm):
    m, d = x.shape
    w, hd, nh = FOX_WIDTH, FOX_HEAD_DIM, FOX_HEADS
    seq = m // batch
    tiles_per_seq = seq // tm if not sample else 1
    tok = lambda width: pl.BlockSpec((tm, width), lambda i: (i, 0))
    full = lambda a: pl.BlockSpec(a.shape, lambda i: (0,) * a.ndim)
    sds = jax.ShapeDtypeStruct
    if sample:
        heads_shape = (DEPTH, batch, nh, seq, hd)
        heads_spec = pl.BlockSpec((1, batch, nh, seq, hd), lambda i: (layer, 0, 0, 0, 0))
        lfr_shape, lfr_spec = (nh, m), pl.BlockSpec((nh, m), lambda i: (0, 0))
    else:
        heads_shape = (DEPTH, batch, nh, hd, seq)
        heads_spec = pl.BlockSpec((1, 1, nh, hd, tm),
                                  lambda i: (layer, i // tiles_per_seq, 0, 0, i % tiles_per_seq))
        lfr_shape = (DEPTH, batch, nh, seq)
        lfr_spec = pl.BlockSpec((1, 1, nh, tm), lambda i: (layer, i // tiles_per_seq, 0, i % tiles_per_seq))
    out_specs = [tok(w), tok(w), tok(w), heads_spec, heads_spec, tok(LANES), lfr_spec,
                 tok(w), tok(w), tok(w), tok(w), tok(w)]
    out_shape = [sds((m, w), BF16), sds((m, w), BF16), sds((m, w), BF16),
                 sds(heads_shape, F32), sds(heads_shape, F32), sds((m, LANES), F32), sds(lfr_shape, F32),
                 sds((m, w), F32), sds((m, w), F32), sds((m, w), F32), sds((m, w), BF16), sds((m, w), F32)]
    body = functools.partial(_inproj_body, layer=layer, sample=sample)
    args = [x, g, wm, wf, bfp, lbp]
    in_specs = [tok(d), pl.BlockSpec((1, 1, d), lambda i: (layer, 0, 0)), full(wm), full(wf), full(bfp), full(lbp)]
    aliases = {}
    if carried is not None:
        out_index = (3, 4, 6)
        for n, buf in enumerate(carried):
            aliases[len(args)] = out_index[n]
            args.append(buf)
            in_specs.append(_ANY)
        body = _drop_alias_refs(body, 6, len(carried))
    return pl.pallas_call(
        body, grid=(m // tm,), in_specs=in_specs, out_specs=out_specs, out_shape=out_shape,
        input_output_aliases=aliases, compiler_params=_cparams("arbitrary"), name="inproj",
    )(*args)


def _cumsum_body(lfr_ref, lft_ref, cr_ref, cc_ref):
    cr_ref[0] = _scan(lfr_ref[0, 0], 1)
    cc_ref[0] = _scan(lft_ref[0], 0)


def _fox_cumsum(lf_rows, lf_tm, *, layer):
    _, b, h, l = lf_rows.shape
    return pl.pallas_call(
        _cumsum_body,
        grid=(b,),
        in_specs=[pl.BlockSpec((1, 1, h, l), lambda i: (layer, i, 0, 0)),
                  pl.BlockSpec((1, l, LANES), lambda i: (i, 0, 0))],
        out_specs=[pl.BlockSpec((1, h, l), lambda i: (i, 0, 0)),
                   pl.BlockSpec((1, l, LANES), lambda i: (i, 0, 0))],
        out_shape=[jax.ShapeDtypeStruct((b, h, l), F32),
                   jax.ShapeDtypeStruct((b, l, LANES), F32)],
        compiler_params=_cparams("parallel"),
        name="fox_cumsum",
    )(lf_rows, lf_tm)


def _fox_prompt_body(q_ref, k_ref, v_ref, cc_ref, cr_ref, o_ref, m_ref, l_ref, acc_ref, kn_ref, *, t, pairs):
    grp = pl.program_id(1)
    i = pl.program_id(2)
    nk = cr_ref.shape[2]
    hd = FOX_HEAD_DIM
    lane = lax.broadcasted_iota(jnp.int32, (t, LANES), 1)
    row = lax.broadcasted_iota(jnp.int32, (t, t), 0)
    col = lax.broadcasted_iota(jnp.int32, (t, t), 1)
    head_lanes = [lane < hd, lane >= hd]
    lanes_of = [slice(p * LANES, (p + 1) * LANES) for p in range(pairs)]

    @pl.when(i == 0)
    def _():
        step = 1024
        lane_s = lax.broadcasted_iota(jnp.int32, (step, LANES), 1)
        for p in range(pairs):
            def chunk(ci, mx, p=p):
                kf = k_ref[0, pl.ds(pl.multiple_of(ci * step, step), step), lanes_of[p]].astype(F32)
                k2 = kf * kf
                n0 = jnp.max(jnp.sum(jnp.where(lane_s < hd, k2, 0.0), axis=1, keepdims=True))
                n1 = jnp.max(jnp.sum(jnp.where(lane_s >= hd, k2, 0.0), axis=1, keepdims=True))
                return jnp.maximum(mx[0], n0), jnp.maximum(mx[1], n1)

            n0, n1 = lax.fori_loop(0, k_ref.shape[1] // step, chunk, (jnp.float32(0.0), jnp.float32(0.0)))
            kn_ref[2 * p] = jnp.sqrt(n0)
            kn_ref[2 * p + 1] = jnp.sqrt(n1)

    cc = cc_ref[0]
    causal = jnp.concatenate([col <= row, col <= row], axis=0)
    qs, cq, qn = [], [], []
    for p in range(pairs):
        qq = q_ref[0, :, lanes_of[p]]
        zero = jnp.zeros_like(qq)
        qs.append(jnp.concatenate([jnp.where(head_lanes[h], qq, zero) for h in range(2)], axis=0))
        head0 = 2 * (grp * pairs + p)
        cq.append([jnp.sum(jnp.where(lane == head0 + h, cc, 0.0), axis=1, keepdims=True) for h in range(2)])
        q2 = qq.astype(F32) * qq.astype(F32)
        qn.append([jnp.sqrt(jnp.sum(jnp.where(head_lanes[h], q2, 0.0), axis=1, keepdims=True))
                   for h in range(2)])

    def scores(p, j, diag):
        kk = k_ref[0, pl.ds(pl.multiple_of(j * t, t), t), lanes_of[p]]
        s = _dot_nt(qs[p], kk)
        s = jnp.concatenate(
            [s[h * t:(h + 1) * t] + (cq[p][h] - cr_ref[0, 2 * p + h, pl.ds(j, 1), :]) for h in range(2)],
            axis=0)
        return jnp.where(causal, s, -jnp.inf) if diag else s

    def values(p, j):
        return v_ref[0, pl.ds(pl.multiple_of(j * t, t), t), lanes_of[p]]

    def first_round(js):
        for p in range(pairs):
            m = l = acc = None
            for n, j in enumerate(js):
                s = scores(p, j, n == len(js) - 1)
                m_blk = jnp.max(s, axis=1, keepdims=True)
                if m is None:
                    m_next = m_blk
                    pr = jnp.exp(s - m_next)
                    l = jnp.sum(pr, axis=1, keepdims=True)
                    acc = _dot(pr.astype(BF16), values(p, j))
                else:
                    m_next = jnp.maximum(m, m_blk)
                    alpha = jnp.exp(m - m_next)
                    pr = jnp.exp(s - m_next)
                    l = alpha * l + jnp.sum(pr, axis=1, keepdims=True)
                    acc = alpha * acc + _dot(pr.astype(BF16), values(p, j))
                m = m_next
            m_ref[p] = jnp.broadcast_to(m, m_ref.shape[1:])
            l_ref[p] = jnp.broadcast_to(l, l_ref.shape[1:])
            acc_ref[p] = acc

    def block(j):
        for p in range(pairs):
            s = scores(p, j, False)
            m_prev = m_ref[p]
            m_next = jnp.maximum(m_prev, jnp.max(s, axis=1, keepdims=True))
            pr = jnp.exp(s - jnp.concatenate([m_next] * (t // LANES), axis=1))
            alpha = jnp.exp(m_prev - m_next)
            l_ref[p] = alpha * l_ref[p] + jnp.sum(pr, axis=1, keepdims=True)
            m_ref[p] = m_next
            acc_ref[p] = alpha * acc_ref[p] + _dot(pr.astype(BF16), values(p, j))

    window = FOX_FIRST_ROUND_BLOCKS

    @pl.when(i >= window - 1)
    def _():
        first_round([i - (window - 1) + n for n in range(window)])

    @pl.when(i < window - 1)
    def _():
        first_round([i])

    j_next = jnp.where(i >= window - 1, i - window, i - 1)

    jidx = lax.broadcasted_iota(jnp.int32, (nk, 1), 0)
    first_needed = jnp.full((nk, 1), nk, jnp.int32)
    for p in range(pairs):
        for h in range(2):
            gap = qn[p][h] * (kn_ref[2 * p + h] * FOX_NORM_SLACK) + cq[p][h] - m_ref[p, h * t:(h + 1) * t, :1]
            reach = jnp.max(gap, axis=0, keepdims=True)
            c_min = jnp.min(cr_ref[0, 2 * p + h], axis=1, keepdims=True)
            first_needed = jnp.minimum(first_needed, jnp.where(reach - c_min >= -FOX_SKIP_GAP, jidx, nk))
    j_min = jnp.minimum(jnp.min(first_needed), j_next + 1)

    def earlier(n, carry):
        block(j_next - n)
        return carry

    lax.fori_loop(0, j_next + 1 - j_min, earlier, 0)
    for p in range(pairs):
        o = acc_ref[p] / l_ref[p]
        o_ref[0, :, lanes_of[p]] = jnp.where(head_lanes[0], o[:t], o[t:]).astype(BF16)


def _fox_prompt(q, k, v, c_col, c_rows, *, t, pairs):
    b, l, w = q.shape
    ngrp = w // (pairs * LANES)
    nk = l // t
    wide = pairs * LANES
    return pl.pallas_call(
        functools.partial(_fox_prompt_body, t=t, pairs=pairs),
        grid=(b, ngrp, l // t),
        in_specs=[
            pl.BlockSpec((1, t, wide), lambda bi, g, i: (bi, i, g)),
            pl.BlockSpec((1, l, wide), lambda bi, g, i: (bi, 0, g)),
            pl.BlockSpec((1, l, wide), lambda bi, g, i: (bi, 0, g)),
            pl.BlockSpec((1, t, LANES), lambda bi, g, i: (bi, i, 0)),
            pl.BlockSpec((1, 2 * pairs, nk, t), lambda bi, g, i: (bi, g, 0, 0)),
        ],
        out_specs=pl.BlockSpec((1, t, wide), lambda bi, g, i: (bi, i, g)),
        out_shape=jax.ShapeDtypeStruct((b, l, w), BF16),
        scratch_shapes=[pltpu.VMEM((pairs, 2 * t, LANES), F32), pltpu.VMEM((pairs, 2 * t, LANES), F32),
                        pltpu.VMEM((pairs, 2 * t, LANES), F32), pltpu.SMEM((2 * pairs,), F32)],
        compiler_params=_cparams("parallel", "parallel", "arbitrary"),
        name="fox_prompt",
    )(q, k, v, c_col, c_rows)


def _sample_scan_body(clf_ref, lfr_ref, lft_ref, cc_ref, cnr_ref, cnc_ref, *, t):
    cc_ref[...] = _scan(clf_ref[0], 1)
    cnr_ref[...] = _scan(lfr_ref[...], 1)
    cnc_ref[...] = _scan(lft_ref[...], 0, period=t)


def _fox_sample_scan(clf, lf_row, lf_tm, *, layer, t):
    _, r, p_len = clf.shape
    full = lambda a: pl.BlockSpec(a.shape, lambda i: (0,) * a.ndim)
    return pl.pallas_call(
        functools.partial(_sample_scan_body, t=t),
        grid=(1,),
        in_specs=[pl.BlockSpec((1, r, p_len), lambda i: (layer, 0, 0)), full(lf_row), full(lf_tm)],
        out_specs=[pl.BlockSpec((r, p_len), lambda i: (0, 0)), full(lf_row), full(lf_tm)],
        out_shape=[jax.ShapeDtypeStruct((r, p_len), F32), jax.ShapeDtypeStruct(lf_row.shape, F32),
                   jax.ShapeDtypeStruct(lf_tm.shape, F32)],
        compiler_params=_cparams("arbitrary"),
        name="fox_sample_scan",
    )(clf, lf_row, lf_tm)


def _fox_sample_body(q_ref, k_ref, v_ref, ck_ref, cv_ref, cc_ref, cnr_ref, cnc_ref, o_ref):
    t = q_ref.shape[1]
    nh, p_len = ck_ref.shape[2], ck_ref.shape[4]
    hd = FOX_HEAD_DIM
    lane = lax.broadcasted_iota(jnp.int32, (t, LANES), 1)
    row = lax.broadcasted_iota(jnp.int32, (t, t), 0)
    col = lax.broadcasted_iota(jnp.int32, (t, t), 1)
    c_cache = cc_ref[...]
    cn_row = cnr_ref[...]
    cn_col = cnc_ref[0]
    outs = []
    for h in range(nh):
        grp = slice((h // 2) * LANES, (h // 2 + 1) * LANES)
        lo = (h % 2) * hd
        qq = q_ref[0, :, grp]
        qm = jnp.where((lane >= lo) & (lane < lo + hd), qq, jnp.zeros_like(qq))
        q_h = qq[:, lo:lo + hd]
        kc_t = ck_ref[0, 0, h].astype(BF16)
        vc_t = cv_ref[0, 0, h].astype(BF16)
        c_h = c_cache[h:h + 1, :]
        tot = c_h[:, p_len - 1:p_len]
        cq = jnp.sum(jnp.where(lane == h, cn_col, 0.0), axis=1, keepdims=True)
        s_c = _dot(q_h, kc_t) + (cq + (tot - c_h))
        s_s = _dot_nt(qm, k_ref[0, :, grp]) + (cq - cn_row[h:h + 1, :t])
        s_s = jnp.where(col <= row, s_s, -jnp.inf)
        m = jnp.maximum(jnp.max(s_c, axis=1, keepdims=True), jnp.max(s_s, axis=1, keepdims=True))
        p_c = jnp.exp(s_c - m)
        p_s = jnp.exp(s_s - m)
        den = jnp.sum(p_c, axis=1, keepdims=True) + jnp.sum(p_s, axis=1, keepdims=True)
        o = _dot_nt(p_c.astype(BF16), vc_t) + _dot(p_s.astype(BF16), v_ref[0, :, grp])[:, lo:lo + hd]
        outs.append(o / den)
    o_ref[0] = jnp.concatenate(outs, axis=1).astype(BF16)


def _fox_sample(q, k, v, cache_k, cache_v, c_cache, cn_row, cn_col, *, layer):
    b, t, w = q.shape
    nh, hd, p_len = cache_k.shape[2:]
    tokq = pl.BlockSpec((1, t, w), lambda bi: (bi, 0, 0))
    cache = pl.BlockSpec((1, 1, nh, hd, p_len), lambda bi: (layer, bi, 0, 0, 0))
    return pl.pallas_call(
        _fox_sample_body,
        grid=(b,),
        in_specs=[tokq, tokq, tokq, cache, cache,
                  pl.BlockSpec((nh, p_len), lambda bi: (bi, 0)),
                  pl.BlockSpec((nh, LANES), lambda bi: (bi, 0)),
                  pl.BlockSpec((1, t, LANES), lambda bi: (bi, 0, 0))],
        out_specs=tokq,
        out_shape=jax.ShapeDtypeStruct((b, t, w), BF16),
        compiler_params=_cparams("parallel"),
        name="fox_sample",
    )(q, k, v, cache_k, cache_v, c_cache, cn_row, cn_col)


def _hgrn_body(q_ref, lf_ref, k_ref, v_ref, s0_ref, o_ref, so_ref, st_ref, tmp_ref, *, c, n_chunks, hps, zero_init):
    ti = pl.program_id(2)
    d = HGRN_HEAD_DIM

    @pl.when(ti == 0)
    def _():
        for hh in range(hps):
            st_ref[hh] = jnp.zeros((d, d), F32) if zero_init else s0_ref[0, 0, hh].T

    row = lax.broadcasted_iota(jnp.int32, (c, c), 0)
    col = lax.broadcasted_iota(jnp.int32, (c, c), 1)
    rowc = lax.broadcasted_iota(jnp.int32, (c, 1), 0)
    half = c // 2
    b_all = _scan(lf_ref[0], 0, period=c)

    span = jnp.zeros((1, hps * d), F32)
    for ci in range(n_chunks):
        b = b_all[ci * c:(ci + 1) * c]
        b_mid = b[half - 1:half]
        span = jnp.maximum(span, jnp.maximum(b[0:1] - b_mid, b_mid - b[c - 1:c]))
    factorable = jnp.max(span) <= HGRN_FACTOR_LIMIT

    def carry_state(st, k, v, b):
        b_last = b[c - 1:c]
        kh = (k * jnp.exp(b_last - b)).astype(BF16)
        return st * jnp.exp(b_last) + _dot_tn(v, kh)

    @pl.when(factorable)
    def _():
        units = [(hh, ci) for hh in range(hps) for ci in range(n_chunks)]
        tile_of = {u: (slice(u[1] * c, (u[1] + 1) * c), slice(u[0] * d, (u[0] + 1) * d)) for u in units}
        sc, ds, qe, grow = {}, {}, {}, {}
        for u in units:
            rs, hs = tile_of[u]
            q = q_ref[0, rs, hs]
            k = k_ref[0, rs, hs]
            b = b_all[rs, hs]
            b_mid = b[half - 1:half]
            b_last = b[c - 1:c]
            qe[u] = (q * jnp.exp(b)).astype(BF16)
            grow[u] = jnp.exp(b_last)
            qt = (q * jnp.exp(b - b_mid)).astype(BF16)
            kt = (k * jnp.exp(b_mid - b)).astype(BF16)
            kh = (k * jnp.exp(b_last - b)).astype(BF16)
            sc[u] = _dot_nt(qt, kt)
            ds[u] = _dot_tn(v_ref[0, rs, hs], kh)
        st_in = {}
        for hh in range(hps):
            st = st_ref[hh]
            for ci in range(n_chunks):
                st_in[(hh, ci)] = st.astype(BF16)
                st = st * grow[(hh, ci)] + ds[(hh, ci)]
            st_ref[hh] = st
        for u in units:
            rs, hs = tile_of[u]
            intra = _dot(jnp.where(col <= row, sc[u], 0.0).astype(BF16), v_ref[0, rs, hs])
            o_ref[0, rs, hs] = _dot_nt(qe[u], st_in[u]) + intra

    @pl.when(jnp.logical_not(factorable))
    def _():
        for hh in range(hps):
            hs = slice(hh * d, (hh + 1) * d)

            def chunk(ci, carry, hh=hh, hs=hs):
                rs = pl.ds(pl.multiple_of(ci * c, c), c)
                q = q_ref[0, rs, hs]
                k = k_ref[0, rs, hs]
                v = v_ref[0, rs, hs]
                b = _scan(lf_ref[0, rs, hs], 0)
                st = st_ref[hh]
                tmp_ref[0] = b
                tmp_ref[1] = k
                tmp_ref[2] = v.astype(F32)

                def key_row(s, acc):
                    bs = tmp_ref[0, pl.ds(s, 1), :]
                    ks = tmp_ref[1, pl.ds(s, 1), :]
                    vs = tmp_ref[2, pl.ds(s, 1), :]
                    w = jnp.exp(jnp.minimum(b - bs, 0.0))
                    a = jnp.sum(q * ks * w, axis=1, keepdims=True)
                    return acc + jnp.where(rowc >= s, a, 0.0) * vs

                o = _dot_nt((q * jnp.exp(b)).astype(BF16), st.astype(BF16))
                o_ref[0, rs, hs] = lax.fori_loop(0, c, key_row, o)
                st_ref[hh] = carry_state(st, k, v, b)
                return carry

            lax.fori_loop(0, n_chunks, chunk, 0)

    @pl.when(ti == pl.num_programs(2) - 1)
    def _():
        for hh in range(hps):
            so_ref[0, 0, hh] = st_ref[hh].T


def _hgrn(hq, hlf, hk, hv, s0, carried, *, layer, c, tl, hps):
    b, l, w = hq.shape
    d = HGRN_HEAD_DIM
    nh = w // d
    tok = pl.BlockSpec((1, tl, hps * d), lambda bi, h, ti: (bi, ti, h))
    st = pl.BlockSpec((1, 1, hps, d, d), lambda bi, h, ti: (layer, bi, h, 0, 0))
    zero_init = s0 is None
    body = functools.partial(_hgrn_body, c=c, n_chunks=tl // c, hps=hps, zero_init=zero_init)
    args, in_specs = [hq, hlf, hk, hv], [tok, tok, tok, tok]
    if zero_init:
        core = body
        body = lambda q, lf, k, v, *rest: core(q, lf, k, v, None, *rest)
    else:
        args.append(s0)
        in_specs.append(st)
    aliases = {}
    if carried is not None:
        aliases[len(args)] = 1
        body = _drop_alias_refs(body, len(args), 1)
        args.append(carried)
        in_specs.append(_ANY)
    return pl.pallas_call(
        body,
        grid=(b, nh // hps, l // tl),
        in_specs=in_specs,
        out_specs=[tok, st],
        out_shape=[jax.ShapeDtypeStruct((b, l, w), F32), jax.ShapeDtypeStruct((DEPTH, b, nh, d, d), F32)],
        scratch_shapes=[pltpu.VMEM((hps, d, d), F32), pltpu.VMEM((3, c, d), F32)],
        input_output_aliases=aliases,
        compiler_params=_cparams("parallel", "parallel", "arbitrary"),
        name="hgrn",
    )(*args)


def kernel(x_prompt, x_sample, cache_k, cache_v, cache_logf, state_hgrn, norm_ffn1, ffn1_wi, ffn1_wo,
           norm_mix, w_in, b_fgate, hgrn_lb, hgrn_gnorm, w_out, norm_ffn2, ffn2_wi, ffn2_wo, norm_final):
    bp, lp, d = x_prompt.shape
    bs, ls, _ = x_sample.shape
    mp, ms = bp * lp, bs * ls
    p_len = cache_k.shape[3]
    f, w, h8 = FOX_WIDTH, HGRN_WIDTH, FOX_HEADS
    tm_ffn, tf_ffn, tm_proj, t_attn, c_hgrn, tl_hgrn = 512, 256, 512, 256, 64, 1024

    xp = x_prompt.reshape(mp, d)
    xs = x_sample.reshape(ms, d)
    gfin = norm_final.reshape(1, d)
    g1, g2, gm = (a.reshape(DEPTH, 1, d) for a in (norm_ffn1, norm_ffn2, norm_mix))
    gn = hgrn_gnorm.reshape(DEPTH, 1, HGRN_HEAD_DIM)
    clf = cache_logf.reshape(DEPTH, bs * h8, p_len)
    cache_kt = jnp.swapaxes(cache_k, 3, 4)
    cache_vt = jnp.swapaxes(cache_v, 3, 4)

    carry_p = carry_s = None
    st_p = st_s = None
    lfs = []
    for l in range(DEPTH):
        last = l == DEPTH - 1
        wl = w_in[l]
        wm = jnp.concatenate([wl[:, :3 * f], wl[:, 3 * f + h8:]], axis=1).astype(BF16)
        wf = jnp.pad(wl[:, 3 * f:3 * f + h8], ((0, 0), (0, LANES - h8))).astype(BF16)
        bfp = jnp.pad(b_fgate[l].reshape(1, h8), ((0, 0), (0, LANES - h8)))
        wi1, wo1 = _layer_bf16(ffn1_wi, layer=l, steps=8), _layer_bf16(ffn1_wo, layer=l, steps=8)
        wi2, wo2 = _layer_bf16(ffn2_wi, layer=l, steps=8), _layer_bf16(ffn2_wo, layer=l, steps=8)
        wout = _layer_bf16(w_out, layer=l, steps=4)

        xp = _ffn(xp, g1, wi1, wo1, gfin, layer=l, final_norm=False, tm=tm_ffn, tf=tf_ffn)
        xs = _ffn(xs, g1, wi1, wo1, gfin, layer=l, final_norm=False, tm=ms, tf=tf_ffn)

        (q, k, v, kh, vh, lft, lfr, hq, hlf, hk, hv, hg) = _inproj(
            xp, gm, wm, wf, bfp, hgrn_lb, carry_p, layer=l, sample=False, batch=bp, tm=tm_proj)
        carry_p = (kh, vh, lfr)
        (q_s, k_s, v_s, kh_s, vh_s, lft_s, lfr_s, hq_s, hlf_s, hk_s, hv_s, hg_s) = _inproj(
            xs, gm, wm, wf, bfp, hgrn_lb, carry_s, layer=l, sample=True, batch=bs, tm=ms)
        carry_s = (kh_s, vh_s)

        c_row, c_col = _fox_cumsum(lfr, lft.reshape(bp, lp, LANES), layer=l)
        fo = _fox_prompt(q.reshape(bp, lp, f), k.reshape(bp, lp, f), v.reshape(bp, lp, f),
                         c_col, c_row.reshape(bp, h8, lp // t_attn, t_attn), t=t_attn, pairs=2)
        lfs_l = lfr_s.reshape(h8, bs, ls).transpose(1, 0, 2)
        lfs.append(lfs_l)
        lfs_pad = jnp.pad(lfs_l, ((0, 0), (0, 0), (0, LANES - ls))).reshape(bs * h8, LANES)
        c_cache, cn_row, cn_col = _fox_sample_scan(clf, lfs_pad, lft_s, layer=l, t=ls)
        fo_s = _fox_sample(q_s.reshape(bs, ls, f), k_s.reshape(bs, ls, f), v_s.reshape(bs, ls, f),
                           cache_kt, cache_vt, c_cache, cn_row, cn_col.reshape(bs, ls, LANES), layer=l)
        to3 = lambda a, b_, l_: a.reshape(b_, l_, w)
        ho, st_p = _hgrn(to3(hq, bp, lp), to3(hlf, bp, lp), to3(hk, bp, lp), to3(hv, bp, lp), None, st_p,
                         layer=l, c=c_hgrn, tl=tl_hgrn, hps=2)
        ho_s, st_s = _hgrn(to3(hq_s, bs, ls), to3(hlf_s, bs, ls), to3(hk_s, bs, ls), to3(hv_s, bs, ls),
                           state_hgrn, st_s, layer=l, c=ls, tl=ls, hps=HGRN_HEADS)

        xp = _ffn(xp, g2, wi2, wo2, gfin, (fo.reshape(mp, f), ho.reshape(mp, w), hg, gn, wout),
                  layer=l, final_norm=last, tm=tm_ffn, tf=tf_ffn)
        xs = _ffn(xs, g2, wi2, wo2, gfin, (fo_s.reshape(ms, f), ho_s.reshape(ms, w), hg_s, gn, wout),
                  layer=l, final_norm=last, tm=ms, tf=tf_ffn)

    k_prompt = jnp.swapaxes(carry_p[0], 3, 4)
    v_prompt = jnp.swapaxes(carry_p[1], 3, 4)
    return (xp.reshape(bp, lp, d), xs.reshape(bs, ls, d), k_prompt, v_prompt, carry_p[2], st_p,
            carry_s[0], carry_s[1], jnp.stack(lfs), st_s)
```

```python
import functools

import jax
import jax.numpy as jnp
from jax import lax
from jax.experimental import pallas as pl
from jax.experimental.pallas import tpu as pltpu

F32 = jnp.float32
BF16 = jnp.bfloat16

D_MODEL = 1024
DEPTH = 2
FOX_HEADS = 8
FOX_HEAD_DIM = 64
FOX_WIDTH = FOX_HEADS * FOX_HEAD_DIM
HGRN_HEADS = 4
HGRN_HEAD_DIM = 128
HGRN_WIDTH = HGRN_HEADS * HGRN_HEAD_DIM
D_FF = 2816
RMS_EPS = 1e-6
FOX_SCALE = FOX_HEAD_DIM ** -0.5

LANES = 128
VMEM_LIMIT_BYTES = 56 * 1024 * 1024
HGRN_FACTOR_LIMIT = 50.0
FOX_SKIP_GAP = 106.0
FOX_NORM_SLACK = 1.001
FOX_FIRST_ROUND_BLOCKS = 3


def _cparams(*sem):
    return pltpu.CompilerParams(dimension_semantics=sem, vmem_limit_bytes=VMEM_LIMIT_BYTES)


def _rms(x, g):
    ms = jnp.mean(x * x, axis=-1, keepdims=True)
    return x * lax.rsqrt(ms + RMS_EPS) * g


def _silu(x):
    return x * jax.nn.sigmoid(x)


def _log1p_exp_neg(d):
    return jnp.log(1.0 + jnp.exp(-d))


def _log_sigmoid(z):
    return jnp.minimum(z, 0.0) - _log1p_exp_neg(jnp.abs(z))


def _gate_terms(z):
    e = jnp.exp(-jnp.abs(z))
    t = 1.0 + e
    return jnp.minimum(z, 0.0) - jnp.log(t), jnp.where(z >= 0.0, e, 1.0) / t


def _dot(a, b):
    return jnp.dot(a, b, preferred_element_type=F32)


def _dot_nt(a, b):
    return lax.dot_general(a, b, (((1,), (1,)), ((), ())), preferred_element_type=F32)


def _dot_tn(a, b):
    return lax.dot_general(a, b, (((0,), (0,)), ((), ())), preferred_element_type=F32)


def _scan(x, axis, period=None):
    n = x.shape[axis] if period is None else period
    idx = lax.broadcasted_iota(jnp.int32, x.shape, axis)
    if period is not None:
        idx = idx & (period - 1)
    s = 1
    while s < n:
        x = x + jnp.where(idx >= s, pltpu.roll(x, s, axis), 0.0)
        s *= 2
    return x


def _drop_alias_refs(body, n_in, n_alias):
    def wrapped(*refs):
        return body(*refs[:n_in], *refs[n_in + n_alias:])
    return wrapped


_ANY = pl.BlockSpec(memory_space=pl.ANY)


def _cast_body(w_ref, o_ref):
    o_ref[...] = w_ref[0].astype(o_ref.dtype)


def _layer_bf16(w, *, layer, steps):
    _, rows, cols = w.shape
    rb = rows // steps
    return pl.pallas_call(
        _cast_body,
        grid=(steps,),
        in_specs=[pl.BlockSpec((1, rb, cols), lambda i: (layer, i, 0))],
        out_specs=pl.BlockSpec((rb, cols), lambda i: (i, 0)),
        out_shape=jax.ShapeDtypeStruct((rows, cols), BF16),
        compiler_params=_cparams("parallel"),
        name="to_bf16",
    )(w)


def _mix(fo_ref, ho_ref, hg_ref, gn_ref):
    gn = gn_ref[0]
    d = HGRN_HEAD_DIM
    parts = [fo_ref[...]]
    for h in range(HGRN_HEADS):
        ho = ho_ref[:, h * d:(h + 1) * d]
        parts.append((_rms(ho, gn) * _silu(hg_ref[:, h * d:(h + 1) * d])).astype(BF16))
    return jnp.concatenate(parts, axis=1)


def _ffn_body(*refs, final_norm, with_outproj, tf):
    if with_outproj:
        (x_ref, fo_ref, ho_ref, hg_ref, gn_ref, wout_ref,
         g_ref, wi_ref, wo_ref, gf_ref, o_ref, acc_ref) = refs
        x = x_ref[...] + _dot(_mix(fo_ref, ho_ref, hg_ref, gn_ref), wout_ref[...])
    else:
        x_ref, g_ref, wi_ref, wo_ref, gf_ref, o_ref, acc_ref = refs
        x = x_ref[...]
    f = wo_ref.shape[0]
    xn = _rms(x, g_ref[0]).astype(BF16)
    for c in range(f // tf):
        a = _dot(xn, wi_ref[:, c * tf:(c + 1) * tf])
        b = _dot(xn, wi_ref[:, f + c * tf:f + (c + 1) * tf])
        part = _dot((_silu(a) * b).astype(BF16), wo_ref[c * tf:(c + 1) * tf, :])
        if c == 0:
            acc_ref[...] = part
        else:
            acc_ref[...] += part
    y = x + 0.5 * acc_ref[...]
    if final_norm:
        y = _rms(y, gf_ref[...])
    o_ref[...] = y


def _ffn(x, g, wi, wo, gf, outproj=None, *, layer, final_norm, tm, tf):
    m, d = x.shape
    tok = lambda width: pl.BlockSpec((tm, width), lambda i: (i, 0))
    held = lambda a: pl.BlockSpec(a.shape, lambda i: (0,) * a.ndim, pipeline_mode=pl.Buffered(1))
    args, in_specs = [x], [tok(d)]
    if outproj is not None:
        fo, ho, hg, gn, wout = outproj
        args += [fo, ho, hg, gn, wout]
        in_specs += [tok(fo.shape[1]), tok(ho.shape[1]), tok(hg.shape[1]),
                     pl.BlockSpec((1, 1, gn.shape[2]), lambda i: (layer, 0, 0)), held(wout)]
    args += [g, wi, wo, gf]
    in_specs += [pl.BlockSpec((1, 1, d), lambda i: (layer, 0, 0)), held(wi), held(wo),
                 pl.BlockSpec((1, d), lambda i: (0, 0))]
    return pl.pallas_call(
        functools.partial(_ffn_body, final_norm=final_norm, with_outproj=outproj is not None, tf=tf),
        grid=(m // tm,),
        in_specs=in_specs,
        out_specs=tok(d),
        out_shape=jax.ShapeDtypeStruct((m, d), F32),
        scratch_shapes=[pltpu.VMEM((tm, d), F32)],
        compiler_params=_cparams("parallel"),
        name="ffn",
    )(*args)


def _inproj_body(x_ref, g_ref, wm_ref, wf_ref, bf_ref, lbp_ref,
                 q_ref, k_ref, v_ref, kh_ref, vh_ref, lft_ref, lfr_ref,
                 hq_ref, hlf_ref, hk_ref, hv_ref, hg_ref, *, layer, sample):
    h = _rms(x_ref[...], g_ref[0]).astype(BF16)
    w, hd = FOX_WIDTH, FOX_HEAD_DIM
    wc = 2 * LANES
    heads_per_chunk = wc // hd

    def chunks(i):
        for n in range(w // wc):
            yield n, slice(n * wc, (n + 1) * wc), _dot(h, wm_ref[:, i * w + n * wc:i * w + (n + 1) * wc])

    def store_heads(ref, n, val):
        h0 = n * heads_per_chunk
        if sample:
            for hh in range(heads_per_chunk):
                piece = val[:, hh * hd:(hh + 1) * hd]
                ref[0, :, h0 + hh] = piece.reshape(ref.shape[1], ref.shape[3], hd)
        else:
            ref[0, 0, h0:h0 + heads_per_chunk] = val.T.reshape(heads_per_chunk, hd, val.shape[0])

    p = lbp_ref[...]
    e = jnp.exp(p - jnp.max(p, axis=0, keepdims=True))
    sm = e / jnp.sum(e, axis=0, keepdims=True)
    cs = sm[0:1]
    for r in range(1, layer + 1):
        cs = cs + sm[r:r + 1]
    lb = cs - sm[0:1]
    log_lb = jnp.log(lb)
    log_1m_lb = jnp.log1p(-lb)

    for n, cols, z in chunks(4):
        log_sig, sig_neg = _gate_terms(z)
        a = log_lb[:, cols]
        b = log_1m_lb[:, cols] + log_sig
        hlf_ref[:, cols] = jnp.maximum(a, b) + _log1p_exp_neg(jnp.abs(a - b))
        hk_ref[:, cols] = (1.0 - lb[:, cols]) * sig_neg
    for n, cols, hq in chunks(3):
        hq_ref[:, cols] = _silu(hq)

    lf = _log_sigmoid(_dot(h, wf_ref[...]) + bf_ref[...])
    lft_ref[...] = lf
    lfr = lf.T[:FOX_HEADS, :]
    if sample:
        lfr_ref[...] = lfr
    else:
        lfr_ref[0, 0] = lfr

    for n, cols, q in chunks(0):
        q_ref[:, cols] = (q * FOX_SCALE).astype(BF16)
    for n, cols, k in chunks(1):
        store_heads(kh_ref, n, k)
        k_ref[:, cols] = k.astype(BF16)
    for n, cols, v in chunks(2):
        store_heads(vh_ref, n, v)
        v_ref[:, cols] = v.astype(BF16)
    for n, cols, hv in chunks(5):
        hv_ref[:, cols] = hv.astype(BF16)
    for n, cols, hg in chunks(6):
        hg_ref[:, cols] = hg


def _inproj(x, g, wm, wf, bfp, lbp, carried, *, layer, sample, batch, tm):
    m, d = x.shape
    w, hd, nh = FOX_WIDTH, FOX_HEAD_DIM, FOX_HEADS
    seq = m // batch
    tiles_per_seq = seq // tm if not sample else 1
    tok = lambda width: pl.BlockSpec((tm, width), lambda i: (i, 0))
    full = lambda a: pl.BlockSpec(a.shape, lambda i: (0,) * a.ndim)
    sds = jax.ShapeDtypeStruct
    if sample:
        heads_shape = (DEPTH, batch, nh, seq, hd)
        heads_spec = pl.BlockSpec((1, batch, nh, seq, hd), lambda i: (layer, 0, 0, 0, 0))
        lfr_shape, lfr_spec = (nh, m), pl.BlockSpec((nh, m), lambda i: (0, 0))
    else:
        heads_shape = (DEPTH, batch, nh, hd, seq)
        heads_spec = pl.BlockSpec((1, 1, nh, hd, tm),
                                  lambda i: (layer, i // tiles_per_seq, 0, 0, i % tiles_per_seq))
        lfr_shape = (DEPTH, batch, nh, seq)
        lfr_spec = pl.BlockSpec((1, 1, nh, tm), lambda i: (layer, i // tiles_per_seq, 0, i % tiles_per_seq))
    out_specs = [tok(w), tok(w), tok(w), heads_spec, heads_spec, tok(LANES), lfr_spec,
                 tok(w), tok(w), tok(w), tok(w), tok(w)]
    out_shape = [sds((m, w), BF16), sds((m, w), BF16), sds((m, w), BF16),
                 sds(heads_shape, F32), sds(heads_shape, F32), sds((m, LANES), F32), sds(lfr_shape, F32),
                 sds((m, w), F32), sds((m, w), F32), sds((m, w), F32), sds((m, w), BF16), sds((m, w), F32)]
    body = functools.partial(_inproj_body, layer=layer, sample=sample)
    args = [x, g, wm, wf, bfp, lbp]
    in_specs = [tok(d), pl.BlockSpec((1, 1, d), lambda i: (layer, 0, 0)), full(wm), full(wf), full(bfp), full(lbp)]
    aliases = {}
    if carried is not None:
        out_index = (3, 4, 6)
        for n, buf in enumerate(carried):
            aliases[len(args)] = out_index[n]
            args.append(buf)
            in_specs.append(_ANY)
        body = _drop_alias_refs(body, 6, len(carried))
    return pl.pallas_call(
        body, grid=(m // tm,), in_specs=in_specs, out_specs=out_specs, out_shape=out_shape,
        input_output_aliases=aliases, compiler_params=_cparams("arbitrary"), name="inproj",
    )(*args)


def _cumsum_body(lfr_ref, lft_ref, cr_ref, cc_ref):
    cr_ref[0] = _scan(lfr_ref[0, 0], 1)
    cc_ref[0] = _scan(lft_ref[0], 0)


def _fox_cumsum(lf_rows, lf_tm, *, layer):
    _, b, h, l = lf_rows.shape
    return pl.pallas_call(
        _cumsum_body,
        grid=(b,),
        in_specs=[pl.BlockSpec((1, 1, h, l), lambda i: (layer, i, 0, 0)),
                  pl.BlockSpec((1, l, LANES), lambda i: (i, 0, 0))],
        out_specs=[pl.BlockSpec((1, h, l), lambda i: (i, 0, 0)),
                   pl.BlockSpec((1, l, LANES), lambda i: (i, 0, 0))],
        out_shape=[jax.ShapeDtypeStruct((b, h, l), F32),
                   jax.ShapeDtypeStruct((b, l, LANES), F32)],
        compiler_params=_cparams("parallel"),
        name="fox_cumsum",
    )(lf_rows, lf_tm)


def _fox_prompt_body(q_ref, k_ref, v_ref, cc_ref, cr_ref, o_ref, m_ref, l_ref, acc_ref, kn_ref, *, t, pairs):
    grp = pl.program_id(1)
    i = pl.program_id(2)
    nk = cr_ref.shape[2]
    hd = FOX_HEAD_DIM
    lane = lax.broadcasted_iota(jnp.int32, (t, LANES), 1)
    row = lax.broadcasted_iota(jnp.int32, (t, t), 0)
    col = lax.broadcasted_iota(jnp.int32, (t, t), 1)
    head_lanes = [lane < hd, lane >= hd]
    lanes_of = [slice(p * LANES, (p + 1) * LANES) for p in range(pairs)]

    @pl.when(i == 0)
    def _():
        step = 1024
        lane_s = lax.broadcasted_iota(jnp.int32, (step, LANES), 1)
        for p in range(pairs):
            def chunk(ci, mx, p=p):
                kf = k_ref[0, pl.ds(pl.multiple_of(ci * step, step), step), lanes_of[p]].astype(F32)
                k2 = kf * kf
                n0 = jnp.max(jnp.sum(jnp.where(lane_s < hd, k2, 0.0), axis=1, keepdims=True))
                n1 = jnp.max(jnp.sum(jnp.where(lane_s >= hd, k2, 0.0), axis=1, keepdims=True))
                return jnp.maximum(mx[0], n0), jnp.maximum(mx[1], n1)

            n0, n1 = lax.fori_loop(0, k_ref.shape[1] // step, chunk, (jnp.float32(0.0), jnp.float32(0.0)))
            kn_ref[2 * p] = jnp.sqrt(n0)
            kn_ref[2 * p + 1] = jnp.sqrt(n1)

    cc = cc_ref[0]
    causal = jnp.concatenate([col <= row, col <= row], axis=0)
    qs, cq, qn = [], [], []
    for p in range(pairs):
        qq = q_ref[0, :, lanes_of[p]]
        zero = jnp.zeros_like(qq)
        qs.append(jnp.concatenate([jnp.where(head_lanes[h], qq, zero) for h in range(2)], axis=0))
        head0 = 2 * (grp * pairs + p)
        cq.append([jnp.sum(jnp.where(lane == head0 + h, cc, 0.0), axis=1, keepdims=True) for h in range(2)])
        q2 = qq.astype(F32) * qq.astype(F32)
        qn.append([jnp.sqrt(jnp.sum(jnp.where(head_lanes[h], q2, 0.0), axis=1, keepdims=True))
                   for h in range(2)])

    def scores(p, j, diag):
        kk = k_ref[0, pl.ds(pl.multiple_of(j * t, t), t), lanes_of[p]]
        s = _dot_nt(qs[p], kk)
        s = jnp.concatenate(
            [s[h * t:(h + 1) * t] + (cq[p][h] - cr_ref[0, 2 * p + h, pl.ds(j, 1), :]) for h in range(2)],
            axis=0)
        return jnp.where(causal, s, -jnp.inf) if diag else s

    def values(p, j):
        return v_ref[0, pl.ds(pl.multiple_of(j * t, t), t), lanes_of[p]]

    def first_round(js):
        for p in range(pairs):
            m = l = acc = None
            for n, j in enumerate(js):
                s = scores(p, j, n == len(js) - 1)
                m_blk = jnp.max(s, axis=1, keepdims=True)
                if m is None:
                    m_next = m_blk
                    pr = jnp.exp(s - m_next)
                    l = jnp.sum(pr, axis=1, keepdims=True)
                    acc = _dot(pr.astype(BF16), values(p, j))
                else:
                    m_next = jnp.maximum(m, m_blk)
                    alpha = jnp.exp(m - m_next)
                    pr = jnp.exp(s - m_next)
                    l = alpha * l + jnp.sum(pr, axis=1, keepdims=True)
                    acc = alpha * acc + _dot(pr.astype(BF16), values(p, j))
                m = m_next
            m_ref[p] = jnp.broadcast_to(m, m_ref.shape[1:])
            l_ref[p] = jnp.broadcast_to(l, l_ref.shape[1:])
            acc_ref[p] = acc

    def block(j):
        for p in range(pairs):
            s = scores(p, j, False)
            m_prev = m_ref[p]
            m_next = jnp.maximum(m_prev, jnp.max(s, axis=1, keepdims=True))
            pr = jnp.exp(s - jnp.concatenate([m_next] * (t // LANES), axis=1))
            alpha = jnp.exp(m_prev - m_next)
            l_ref[p] = alpha * l_ref[p] + jnp.sum(pr, axis=1, keepdims=True)
            m_ref[p] = m_next
            acc_ref[p] = alpha * acc_ref[p] + _dot(pr.astype(BF16), values(p, j))

    window = FOX_FIRST_ROUND_BLOCKS

    @pl.when(i >= window - 1)
    def _():
        first_round([i - (window - 1) + n for n in range(window)])

    @pl.when(i < window - 1)
    def _():
        first_round([i])

    j_next = jnp.where(i >= window - 1, i - window, i - 1)

    jidx = lax.broadcasted_iota(jnp.int32, (nk, 1), 0)
    first_needed = jnp.full((nk, 1), nk, jnp.int32)
    for p in range(pairs):
        for h in range(2):
            gap = qn[p][h] * (kn_ref[2 * p + h] * FOX_NORM_SLACK) + cq[p][h] - m_ref[p, h * t:(h + 1) * t, :1]
            reach = jnp.max(gap, axis=0, keepdims=True)
            c_min = jnp.min(cr_ref[0, 2 * p + h], axis=1, keepdims=True)
            first_needed = jnp.minimum(first_needed, jnp.where(reach - c_min >= -FOX_SKIP_GAP, jidx, nk))
    j_min = jnp.minimum(jnp.min(first_needed), j_next + 1)

    def earlier(n, carry):
        block(j_next - n)
        return carry

    lax.fori_loop(0, j_next + 1 - j_min, earlier, 0)
    for p in range(pairs):
        o = acc_ref[p] / l_ref[p]
        o_ref[0, :, lanes_of[p]] = jnp.where(head_lanes[0], o[:t], o[t:]).astype(BF16)


def _fox_prompt(q, k, v, c_col, c_rows, *, t, pairs):
    b, l, w = q.shape
    ngrp = w // (pairs * LANES)
    nk = l // t
    wide = pairs * LANES
    return pl.pallas_call(
        functools.partial(_fox_prompt_body, t=t, pairs=pairs),
        grid=(b, ngrp, l // t),
        in_specs=[
            pl.BlockSpec((1, t, wide), lambda bi, g, i: (bi, i, g)),
            pl.BlockSpec((1, l, wide), lambda bi, g, i: (bi, 0, g), pipeline_mode=pl.Buffered(1)),
            pl.BlockSpec((1, l, wide), lambda bi, g, i: (bi, 0, g), pipeline_mode=pl.Buffered(1)),
            pl.BlockSpec((1, t, LANES), lambda bi, g, i: (bi, i, 0)),
            pl.BlockSpec((1, 2 * pairs, nk, t), lambda bi, g, i: (bi, g, 0, 0)),
        ],
        out_specs=pl.BlockSpec((1, t, wide), lambda bi, g, i: (bi, i, g)),
        out_shape=jax.ShapeDtypeStruct((b, l, w), BF16),
        scratch_shapes=[pltpu.VMEM((pairs, 2 * t, LANES), F32), pltpu.VMEM((pairs, 2 * t, LANES), F32),
                        pltpu.VMEM((pairs, 2 * t, LANES), F32), pltpu.SMEM((2 * pairs,), F32)],
        compiler_params=_cparams("parallel", "parallel", "arbitrary"),
        name="fox_prompt",
    )(q, k, v, c_col, c_rows)


def _sample_scan_body(clf_ref, lfr_ref, lft_ref, cc_ref, cnr_ref, cnc_ref, *, t):
    cc_ref[...] = _scan(clf_ref[0], 1)
    cnr_ref[...] = _scan(lfr_ref[...], 1)
    cnc_ref[...] = _scan(lft_ref[...], 0, period=t)


def _fox_sample_scan(clf, lf_row, lf_tm, *, layer, t):
    _, r, p_len = clf.shape
    full = lambda a: pl.BlockSpec(a.shape, lambda i: (0,) * a.ndim)
    return pl.pallas_call(
        functools.partial(_sample_scan_body, t=t),
        grid=(1,),
        in_specs=[pl.BlockSpec((1, r, p_len), lambda i: (layer, 0, 0)), full(lf_row), full(lf_tm)],
        out_specs=[pl.BlockSpec((r, p_len), lambda i: (0, 0)), full(lf_row), full(lf_tm)],
        out_shape=[jax.ShapeDtypeStruct((r, p_len), F32), jax.ShapeDtypeStruct(lf_row.shape, F32),
                   jax.ShapeDtypeStruct(lf_tm.shape, F32)],
        compiler_params=_cparams("arbitrary"),
        name="fox_sample_scan",
    )(clf, lf_row, lf_tm)


def _fox_sample_body(q_ref, k_ref, v_ref, ck_ref, cv_ref, cc_ref, cnr_ref, cnc_ref, o_ref):
    t = q_ref.shape[1]
    nh, p_len = ck_ref.shape[2], ck_ref.shape[4]
    hd = FOX_HEAD_DIM
    lane = lax.broadcasted_iota(jnp.int32, (t, LANES), 1)
    row = lax.broadcasted_iota(jnp.int32, (t, t), 0)
    col = lax.broadcasted_iota(jnp.int32, (t, t), 1)
    c_cache = cc_ref[...]
    cn_row = cnr_ref[...]
    cn_col = cnc_ref[0]
    outs = []
    for h in range(nh):
        grp = slice((h // 2) * LANES, (h // 2 + 1) * LANES)
        lo = (h % 2) * hd
        qq = q_ref[0, :, grp]
        qm = jnp.where((lane >= lo) & (lane < lo + hd), qq, jnp.zeros_like(qq))
        q_h = qq[:, lo:lo + hd]
        kc_t = ck_ref[0, 0, h].astype(BF16)
        vc_t = cv_ref[0, 0, h].astype(BF16)
        c_h = c_cache[h:h + 1, :]
        tot = c_h[:, p_len - 1:p_len]
        cq = jnp.sum(jnp.where(lane == h, cn_col, 0.0), axis=1, keepdims=True)
        s_c = _dot(q_h, kc_t) + (cq + (tot - c_h))
        s_s = _dot_nt(qm, k_ref[0, :, grp]) + (cq - cn_row[h:h + 1, :t])
        s_s = jnp.where(col <= row, s_s, -jnp.inf)
        m = jnp.maximum(jnp.max(s_c, axis=1, keepdims=True), jnp.max(s_s, axis=1, keepdims=True))
        p_c = jnp.exp(s_c - m)
        p_s = jnp.exp(s_s - m)
        den = jnp.sum(p_c, axis=1, keepdims=True) + jnp.sum(p_s, axis=1, keepdims=True)
        o = _dot_nt(p_c.astype(BF16), vc_t) + _dot(p_s.astype(BF16), v_ref[0, :, grp])[:, lo:lo + hd]
        outs.append(o / den)
    o_ref[0] = jnp.concatenate(outs, axis=1).astype(BF16)


def _fox_sample(q, k, v, cache_k, cache_v, c_cache, cn_row, cn_col, *, layer):
    b, t, w = q.shape
    nh, hd, p_len = cache_k.shape[2:]
    tokq = pl.BlockSpec((1, t, w), lambda bi: (bi, 0, 0))
    cache = pl.BlockSpec((1, 1, nh, hd, p_len), lambda bi: (layer, bi, 0, 0, 0))
    return pl.pallas_call(
        _fox_sample_body,
        grid=(b,),
        in_specs=[tokq, tokq, tokq, cache, cache,
                  pl.BlockSpec((nh, p_len), lambda bi: (bi, 0)),
                  pl.BlockSpec((nh, LANES), lambda bi: (bi, 0)),
                  pl.BlockSpec((1, t, LANES), lambda bi: (bi, 0, 0))],
        out_specs=tokq,
        out_shape=jax.ShapeDtypeStruct((b, t, w), BF16),
        compiler_params=_cparams("parallel"),
        name="fox_sample",
    )(q, k, v, cache_k, cache_v, c_cache, cn_row, cn_col)


def _hgrn_body(q_ref, lf_ref, k_ref, v_ref, s0_ref, o_ref, so_ref, st_ref, tmp_ref, *, c, n_chunks, hps, zero_init):
    ti = pl.program_id(2)
    d = HGRN_HEAD_DIM

    @pl.when(ti == 0)
    def _():
        for hh in range(hps):
            st_ref[hh] = jnp.zeros((d, d), F32) if zero_init else s0_ref[0, 0, hh].T

    row = lax.broadcasted_iota(jnp.int32, (c, c), 0)
    col = lax.broadcasted_iota(jnp.int32, (c, c), 1)
    rowc = lax.broadcasted_iota(jnp.int32, (c, 1), 0)
    half = c // 2
    b_all = _scan(lf_ref[0], 0, period=c)

    span = jnp.zeros((1, hps * d), F32)
    for ci in range(n_chunks):
        b = b_all[ci * c:(ci + 1) * c]
        b_mid = b[half - 1:half]
        span = jnp.maximum(span, jnp.maximum(b[0:1] - b_mid, b_mid - b[c - 1:c]))
    factorable = jnp.max(span) <= HGRN_FACTOR_LIMIT

    def carry_state(st, k, v, b):
        b_last = b[c - 1:c]
        kh = (k * jnp.exp(b_last - b)).astype(BF16)
        return st * jnp.exp(b_last) + _dot_tn(v, kh)

    @pl.when(factorable)
    def _():
        units = [(hh, ci) for hh in range(hps) for ci in range(n_chunks)]
        tile_of = {u: (slice(u[1] * c, (u[1] + 1) * c), slice(u[0] * d, (u[0] + 1) * d)) for u in units}
        sc, ds, qe, grow = {}, {}, {}, {}
        for u in units:
            rs, hs = tile_of[u]
            q = q_ref[0, rs, hs]
            k = k_ref[0, rs, hs]
            b = b_all[rs, hs]
            b_mid = b[half - 1:half]
            b_last = b[c - 1:c]
            qe[u] = (q * jnp.exp(b)).astype(BF16)
            grow[u] = jnp.exp(b_last)
            qt = (q * jnp.exp(b - b_mid)).astype(BF16)
            kt = (k * jnp.exp(b_mid - b)).astype(BF16)
            kh = (k * jnp.exp(b_last - b)).astype(BF16)
            sc[u] = _dot_nt(qt, kt)
            ds[u] = _dot_tn(v_ref[0, rs, hs], kh)
        st_in = {}
        for hh in range(hps):
            st = st_ref[hh]
            for ci in range(n_chunks):
                st_in[(hh, ci)] = st.astype(BF16)
                st = st * grow[(hh, ci)] + ds[(hh, ci)]
            st_ref[hh] = st
        for u in units:
            rs, hs = tile_of[u]
            intra = _dot(jnp.where(col <= row, sc[u], 0.0).astype(BF16), v_ref[0, rs, hs])
            o_ref[0, rs, hs] = _dot_nt(qe[u], st_in[u]) + intra

    @pl.when(jnp.logical_not(factorable))
    def _():
        for hh in range(hps):
            hs = slice(hh * d, (hh + 1) * d)

            def chunk(ci, carry, hh=hh, hs=hs):
                rs = pl.ds(pl.multiple_of(ci * c, c), c)
                q = q_ref[0, rs, hs]
                k = k_ref[0, rs, hs]
                v = v_ref[0, rs, hs]
                b = _scan(lf_ref[0, rs, hs], 0)
                st = st_ref[hh]
                tmp_ref[0] = b
                tmp_ref[1] = k
                tmp_ref[2] = v.astype(F32)

                def key_row(s, acc):
                    bs = tmp_ref[0, pl.ds(s, 1), :]
                    ks = tmp_ref[1, pl.ds(s, 1), :]
                    vs = tmp_ref[2, pl.ds(s, 1), :]
                    w = jnp.exp(jnp.minimum(b - bs, 0.0))
                    a = jnp.sum(q * ks * w, axis=1, keepdims=True)
                    return acc + jnp.where(rowc >= s, a, 0.0) * vs

                o = _dot_nt((q * jnp.exp(b)).astype(BF16), st.astype(BF16))
                o_ref[0, rs, hs] = lax.fori_loop(0, c, key_row, o)
                st_ref[hh] = carry_state(st, k, v, b)
                return carry

            lax.fori_loop(0, n_chunks, chunk, 0)

    @pl.when(ti == pl.num_programs(2) - 1)
    def _():
        for hh in range(hps):
            so_ref[0, 0, hh] = st_ref[hh].T


def _hgrn(hq, hlf, hk, hv, s0, carried, *, layer, c, tl, hps):
    b, l, w = hq.shape
    d = HGRN_HEAD_DIM
    nh = w // d
    tok = pl.BlockSpec((1, tl, hps * d), lambda bi, h, ti: (bi, ti, h))
    st = pl.BlockSpec((1, 1, hps, d, d), lambda bi, h, ti: (layer, bi, h, 0, 0))
    zero_init = s0 is None
    body = functools.partial(_hgrn_body, c=c, n_chunks=tl // c, hps=hps, zero_init=zero_init)
    args, in_specs = [hq, hlf, hk, hv], [tok, tok, tok, tok]
    if zero_init:
        core = body
        body = lambda q, lf, k, v, *rest: core(q, lf, k, v, None, *rest)
    else:
        args.append(s0)
        in_specs.append(st)
    aliases = {}
    if carried is not None:
        aliases[len(args)] = 1
        body = _drop_alias_refs(body, len(args), 1)
        args.append(carried)
        in_specs.append(_ANY)
    return pl.pallas_call(
        body,
        grid=(b, nh // hps, l // tl),
        in_specs=in_specs,
        out_specs=[tok, st],
        out_shape=[jax.ShapeDtypeStruct((b, l, w), F32), jax.ShapeDtypeStruct((DEPTH, b, nh, d, d), F32)],
        scratch_shapes=[pltpu.VMEM((hps, d, d), F32), pltpu.VMEM((3, c, d), F32)],
        input_output_aliases=aliases,
        compiler_params=_cparams("parallel", "parallel", "arbitrary"),
        name="hgrn",
    )(*args)


def kernel(x_prompt, x_sample, cache_k, cache_v, cache_logf, state_hgrn, norm_ffn1, ffn1_wi, ffn1_wo,
           norm_mix, w_in, b_fgate, hgrn_lb, hgrn_gnorm, w_out, norm_ffn2, ffn2_wi, ffn2_wo, norm_final):
    bp, lp, d = x_prompt.shape
    bs, ls, _ = x_sample.shape
    mp, ms = bp * lp, bs * ls
    p_len = cache_k.shape[3]
    f, w, h8 = FOX_WIDTH, HGRN_WIDTH, FOX_HEADS
    tm_ffn, tf_ffn, tm_proj, t_attn, c_hgrn, tl_hgrn = 512, 256, 512, 256, 64, 1024

    xp = x_prompt.reshape(mp, d)
    xs = x_sample.reshape(ms, d)
    gfin = norm_final.reshape(1, d)
    g1, g2, gm = (a.reshape(DEPTH, 1, d) for a in (norm_ffn1, norm_ffn2, norm_mix))
    gn = hgrn_gnorm.reshape(DEPTH, 1, HGRN_HEAD_DIM)
    clf = cache_logf.reshape(DEPTH, bs * h8, p_len)
    cache_kt = jnp.swapaxes(cache_k, 3, 4)
    cache_vt = jnp.swapaxes(cache_v, 3, 4)

    carry_p = carry_s = None
    st_p = st_s = None
    lfs = []
    for l in range(DEPTH):
        last = l == DEPTH - 1
        wl = w_in[l]
        wm = jnp.concatenate([wl[:, :3 * f], wl[:, 3 * f + h8:]], axis=1).astype(BF16)
        wf = jnp.pad(wl[:, 3 * f:3 * f + h8], ((0, 0), (0, LANES - h8))).astype(BF16)
        bfp = jnp.pad(b_fgate[l].reshape(1, h8), ((0, 0), (0, LANES - h8)))
        wi1, wo1 = _layer_bf16(ffn1_wi, layer=l, steps=4), _layer_bf16(ffn1_wo, layer=l, steps=2)
        wi2, wo2 = _layer_bf16(ffn2_wi, layer=l, steps=4), _layer_bf16(ffn2_wo, layer=l, steps=2)
        wout = _layer_bf16(w_out, layer=l, steps=2)

        xp = _ffn(xp, g1, wi1, wo1, gfin, layer=l, final_norm=False, tm=tm_ffn, tf=tf_ffn)
        xs = _ffn(xs, g1, wi1, wo1, gfin, layer=l, final_norm=False, tm=ms, tf=tf_ffn)

        (q, k, v, kh, vh, lft, lfr, hq, hlf, hk, hv, hg) = _inproj(
            xp, gm, wm, wf, bfp, hgrn_lb, carry_p, layer=l, sample=False, batch=bp, tm=tm_proj)
        carry_p = (kh, vh, lfr)
        (q_s, k_s, v_s, kh_s, vh_s, lft_s, lfr_s, hq_s, hlf_s, hk_s, hv_s, hg_s) = _inproj(
            xs, gm, wm, wf, bfp, hgrn_lb, carry_s, layer=l, sample=True, batch=bs, tm=ms)
        carry_s = (kh_s, vh_s)

        c_row, c_col = _fox_cumsum(lfr, lft.reshape(bp, lp, LANES), layer=l)
        fo = _fox_prompt(q.reshape(bp, lp, f), k.reshape(bp, lp, f), v.reshape(bp, lp, f),
                         c_col, c_row.reshape(bp, h8, lp // t_attn, t_attn), t=t_attn, pairs=4)
        lfs_l = lfr_s.reshape(h8, bs, ls).transpose(1, 0, 2)
        lfs.append(lfs_l)
        lfs_pad = jnp.pad(lfs_l, ((0, 0), (0, 0), (0, LANES - ls))).reshape(bs * h8, LANES)
        c_cache, cn_row, cn_col = _fox_sample_scan(clf, lfs_pad, lft_s, layer=l, t=ls)
        fo_s = _fox_sample(q_s.reshape(bs, ls, f), k_s.reshape(bs, ls, f), v_s.reshape(bs, ls, f),
                           cache_kt, cache_vt, c_cache, cn_row, cn_col.reshape(bs, ls, LANES), layer=l)
        to3 = lambda a, b_, l_: a.reshape(b_, l_, w)
        ho, st_p = _hgrn(to3(hq, bp, lp), to3(hlf, bp, lp), to3(hk, bp, lp), to3(hv, bp, lp), None, st_p,
                         layer=l, c=c_hgrn, tl=tl_hgrn, hps=2)
        ho_s, st_s = _hgrn(to3(hq_s, bs, ls), to3(hlf_s, bs, ls), to3(hk_s, bs, ls), to3(hv_s, bs, ls),
                           state_hgrn, st_s, layer=l, c=ls, tl=ls, hps=HGRN_HEADS)

        xp = _ffn(xp, g2, wi2, wo2, gfin, (fo.reshape(mp, f), ho.reshape(mp, w), hg, gn, wout),
                  layer=l, final_norm=last, tm=tm_ffn, tf=tf_ffn)
        xs = _ffn(xs, g2, wi2, wo2, gfin, (fo_s.reshape(ms, f), ho_s.reshape(ms, w), hg_s, gn, wout),
                  layer=l, final_norm=last, tm=ms, tf=tf_ffn)

    k_prompt = jnp.swapaxes(carry_p[0], 3, 4)
    v_prompt = jnp.swapaxes(carry_p[1], 3, 4)
    return (xp.reshape(bp, lp, d), xs.reshape(bs, ls, d), k_prompt, v_prompt, carry_p[2], st_p,
            carry_s[0], carry_s[1], jnp.stack(lfs), st_s)
```

```python
import functools

import jax
import jax.numpy as jnp
from jax import lax
from jax.experimental import pallas as pl
from jax.experimental.pallas import tpu as pltpu

F32 = jnp.float32
BF16 = jnp.bfloat16

D_MODEL = 1024
DEPTH = 2
FOX_HEADS = 8
FOX_HEAD_DIM = 64
FOX_WIDTH = FOX_HEADS * FOX_HEAD_DIM
HGRN_HEADS = 4
HGRN_HEAD_DIM = 128
HGRN_WIDTH = HGRN_HEADS * HGRN_HEAD_DIM
D_FF = 2816
RMS_EPS = 1e-6
FOX_SCALE = FOX_HEAD_DIM ** -0.5

LANES = 128
VMEM_LIMIT_BYTES = 56 * 1024 * 1024
HGRN_FACTOR_LIMIT = 50.0
FOX_SKIP_GAP = 106.0
FOX_NORM_SLACK = 1.001
FOX_FIRST_ROUND_BLOCKS = 3


def _cparams(*sem):
    return pltpu.CompilerParams(dimension_semantics=sem, vmem_limit_bytes=VMEM_LIMIT_BYTES)


def _rms(x, g):
    ms = jnp.mean(x * x, axis=-1, keepdims=True)
    return x * lax.rsqrt(ms + RMS_EPS) * g


def _silu(x):
    return x * jax.nn.sigmoid(x)


def _log1p_exp_neg(d):
    return jnp.log(1.0 + jnp.exp(-d))


def _log_sigmoid(z):
    return jnp.minimum(z, 0.0) - _log1p_exp_neg(jnp.abs(z))


def _gate_terms(z):
    e = jnp.exp(-jnp.abs(z))
    t = 1.0 + e
    return jnp.minimum(z, 0.0) - jnp.log(t), jnp.where(z >= 0.0, e, 1.0) / t


def _dot(a, b):
    return jnp.dot(a, b, preferred_element_type=F32)


def _dot_nt(a, b):
    return lax.dot_general(a, b, (((1,), (1,)), ((), ())), preferred_element_type=F32)


def _dot_tn(a, b):
    return lax.dot_general(a, b, (((0,), (0,)), ((), ())), preferred_element_type=F32)


def _scan(x, axis, period=None):
    n = x.shape[axis] if period is None else period
    idx = lax.broadcasted_iota(jnp.int32, x.shape, axis)
    if period is not None:
        idx = idx & (period - 1)
    s = 1
    while s < n:
        x = x + jnp.where(idx >= s, pltpu.roll(x, s, axis), 0.0)
        s *= 2
    return x


def _drop_alias_refs(body, n_in, n_alias):
    def wrapped(*refs):
        return body(*refs[:n_in], *refs[n_in + n_alias:])
    return wrapped


_ANY = pl.BlockSpec(memory_space=pl.ANY)


def _cast_body(w_ref, o_ref):
    o_ref[...] = w_ref[0].astype(o_ref.dtype)


def _layer_bf16(w, *, layer, steps):
    _, rows, cols = w.shape
    rb = rows // steps
    return pl.pallas_call(
        _cast_body,
        grid=(steps,),
        in_specs=[pl.BlockSpec((1, rb, cols), lambda i: (layer, i, 0))],
        out_specs=pl.BlockSpec((rb, cols), lambda i: (i, 0)),
        out_shape=jax.ShapeDtypeStruct((rows, cols), BF16),
        compiler_params=_cparams("parallel"),
        name="to_bf16",
    )(w)


def _mix(fo_ref, ho_ref, hg_ref, gn_ref):
    gn = gn_ref[0]
    d = HGRN_HEAD_DIM
    parts = [fo_ref[...]]
    for h in range(HGRN_HEADS):
        ho = ho_ref[:, h * d:(h + 1) * d]
        parts.append((_rms(ho, gn) * _silu(hg_ref[:, h * d:(h + 1) * d])).astype(BF16))
    return jnp.concatenate(parts, axis=1)


def _ffn_body(*refs, final_norm, with_outproj, tf):
    if with_outproj:
        (x_ref, fo_ref, ho_ref, hg_ref, gn_ref, wout_ref,
         g_ref, wi_ref, wo_ref, gf_ref, o_ref, acc_ref) = refs
        x = x_ref[...] + _dot(_mix(fo_ref, ho_ref, hg_ref, gn_ref), wout_ref[...])
    else:
        x_ref, g_ref, wi_ref, wo_ref, gf_ref, o_ref, acc_ref = refs
        x = x_ref[...]
    f = wo_ref.shape[0]
    xn = _rms(x, g_ref[0]).astype(BF16)
    for c in range(f // tf):
        a = _dot(xn, wi_ref[:, c * tf:(c + 1) * tf])
        b = _dot(xn, wi_ref[:, f + c * tf:f + (c + 1) * tf])
        part = _dot((_silu(a) * b).astype(BF16), wo_ref[c * tf:(c + 1) * tf, :])
        if c == 0:
            acc_ref[...] = part
        else:
            acc_ref[...] += part
    y = x + 0.5 * acc_ref[...]
    if final_norm:
        y = _rms(y, gf_ref[...])
    o_ref[...] = y


def _ffn(x, g, wi, wo, gf, outproj=None, *, layer, final_norm, tm, tf):
    m, d = x.shape
    tok = lambda width: pl.BlockSpec((tm, width), lambda i: (i, 0))
    held = lambda a: pl.BlockSpec(a.shape, lambda i: (0,) * a.ndim, pipeline_mode=pl.Buffered(1))
    args, in_specs = [x], [tok(d)]
    if outproj is not None:
        fo, ho, hg, gn, wout = outproj
        args += [fo, ho, hg, gn, wout]
        in_specs += [tok(fo.shape[1]), tok(ho.shape[1]), tok(hg.shape[1]),
                     pl.BlockSpec((1, 1, gn.shape[2]), lambda i: (layer, 0, 0)), held(wout)]
    args += [g, wi, wo, gf]
    in_specs += [pl.BlockSpec((1, 1, d), lambda i: (layer, 0, 0)), held(wi), held(wo),
                 pl.BlockSpec((1, d), lambda i: (0, 0))]
    return pl.pallas_call(
        functools.partial(_ffn_body, final_norm=final_norm, with_outproj=outproj is not None, tf=tf),
        grid=(m // tm,),
        in_specs=in_specs,
        out_specs=tok(d),
        out_shape=jax.ShapeDtypeStruct((m, d), F32),
        scratch_shapes=[pltpu.VMEM((tm, d), F32)],
        compiler_params=_cparams("parallel"),
        name="ffn",
    )(*args)


def _inproj_body(x_ref, g_ref, wm_ref, wf_ref, bf_ref, lbp_ref,
                 q_ref, k_ref, v_ref, kh_ref, vh_ref, lfr_ref,
                 hq_ref, hlf_ref, hk_ref, hv_ref, hg_ref, *, layer, sample):
    h = _rms(x_ref[...], g_ref[0]).astype(BF16)
    w, hd = FOX_WIDTH, FOX_HEAD_DIM
    wc = 2 * LANES
    heads_per_chunk = wc // hd

    def chunks(i):
        for n in range(w // wc):
            yield n, slice(n * wc, (n + 1) * wc), _dot(h, wm_ref[:, i * w + n * wc:i * w + (n + 1) * wc])

    def store_heads(ref, n, val):
        h0 = n * heads_per_chunk
        if sample:
            for hh in range(heads_per_chunk):
                piece = val[:, hh * hd:(hh + 1) * hd]
                ref[0, :, h0 + hh] = piece.reshape(ref.shape[1], ref.shape[3], hd)
        else:
            ref[0, 0, h0:h0 + heads_per_chunk] = val.T.reshape(heads_per_chunk, hd, val.shape[0])

    p = lbp_ref[...]
    e = jnp.exp(p - jnp.max(p, axis=0, keepdims=True))
    sm = e / jnp.sum(e, axis=0, keepdims=True)
    cs = sm[0:1]
    for r in range(1, layer + 1):
        cs = cs + sm[r:r + 1]
    lb = cs - sm[0:1]
    log_lb = jnp.log(lb)
    log_1m_lb = jnp.log1p(-lb)

    for n, cols, z in chunks(4):
        log_sig, sig_neg = _gate_terms(z)
        a = log_lb[:, cols]
        b = log_1m_lb[:, cols] + log_sig
        hlf_ref[:, cols] = jnp.maximum(a, b) + _log1p_exp_neg(jnp.abs(a - b))
        hk_ref[:, cols] = (1.0 - lb[:, cols]) * sig_neg
    for n, cols, hq in chunks(3):
        hq_ref[:, cols] = _silu(hq)

    lf = _log_sigmoid(_dot(h, wf_ref[...]) + bf_ref[...])
    lfr = lf.T[:FOX_HEADS, :]
    if sample:
        lfr_ref[...] = lfr
    else:
        lfr_ref[0, 0] = lfr

    for n, cols, q in chunks(0):
        q_ref[:, cols] = (q * FOX_SCALE).astype(BF16)
    for n, cols, k in chunks(1):
        store_heads(kh_ref, n, k)
        k_ref[:, cols] = k.astype(BF16)
    for n, cols, v in chunks(2):
        store_heads(vh_ref, n, v)
        v_ref[:, cols] = v.astype(BF16)
    for n, cols, hv in chunks(5):
        hv_ref[:, cols] = hv.astype(BF16)
    for n, cols, hg in chunks(6):
        hg_ref[:, cols] = hg


def _inproj(x, g, wm, wf, bfp, lbp, carried, *, layer, sample, batch, tm):
    m, d = x.shape
    w, hd, nh = FOX_WIDTH, FOX_HEAD_DIM, FOX_HEADS
    seq = m // batch
    tiles_per_seq = seq // tm if not sample else 1
    tok = lambda width: pl.BlockSpec((tm, width), lambda i: (i, 0))
    full = lambda a: pl.BlockSpec(a.shape, lambda i: (0,) * a.ndim)
    sds = jax.ShapeDtypeStruct
    if sample:
        heads_shape = (DEPTH, batch, nh, seq, hd)
        heads_spec = pl.BlockSpec((1, batch, nh, seq, hd), lambda i: (layer, 0, 0, 0, 0))
        lfr_shape, lfr_spec = (nh, m), pl.BlockSpec((nh, m), lambda i: (0, 0))
    else:
        heads_shape = (DEPTH, batch, nh, hd, seq)
        heads_spec = pl.BlockSpec((1, 1, nh, hd, tm),
                                  lambda i: (layer, i // tiles_per_seq, 0, 0, i % tiles_per_seq))
        lfr_shape = (DEPTH, batch, nh, seq)
        lfr_spec = pl.BlockSpec((1, 1, nh, tm), lambda i: (layer, i // tiles_per_seq, 0, i % tiles_per_seq))
    out_specs = [tok(w), tok(w), tok(w), heads_spec, heads_spec, lfr_spec,
                 tok(w), tok(w), tok(w), tok(w), tok(w)]
    out_shape = [sds((m, w), BF16), sds((m, w), BF16), sds((m, w), BF16),
                 sds(heads_shape, F32), sds(heads_shape, F32), sds(lfr_shape, F32),
                 sds((m, w), F32), sds((m, w), F32), sds((m, w), F32), sds((m, w), BF16), sds((m, w), F32)]
    body = functools.partial(_inproj_body, layer=layer, sample=sample)
    args = [x, g, wm, wf, bfp, lbp]
    in_specs = [tok(d), pl.BlockSpec((1, 1, d), lambda i: (layer, 0, 0)), full(wm), full(wf), full(bfp), full(lbp)]
    aliases = {}
    if carried is not None:
        out_index = (3, 4, 5)
        for n, buf in enumerate(carried):
            aliases[len(args)] = out_index[n]
            args.append(buf)
            in_specs.append(_ANY)
        body = _drop_alias_refs(body, 6, len(carried))
    return pl.pallas_call(
        body, grid=(m // tm,), in_specs=in_specs, out_specs=out_specs, out_shape=out_shape,
        input_output_aliases=aliases, compiler_params=_cparams("arbitrary"), name="inproj",
    )(*args)


def _cumsum_body(lfr_ref, cr_ref):
    cr_ref[0] = _scan(lfr_ref[0, 0], 1)


def _fox_cumsum(lf_rows, *, layer):
    _, b, h, l = lf_rows.shape
    return pl.pallas_call(
        _cumsum_body,
        grid=(b,),
        in_specs=[pl.BlockSpec((1, 1, h, l), lambda i: (layer, i, 0, 0))],
        out_specs=pl.BlockSpec((1, h, l), lambda i: (i, 0, 0)),
        out_shape=jax.ShapeDtypeStruct((b, h, l), F32),
        compiler_params=_cparams("parallel"),
        name="fox_cumsum",
    )(lf_rows)


def _fox_prompt_body(q_ref, k_ref, v_ref, cr_ref, o_ref, m_ref, l_ref, acc_ref, kn_ref, trips_ref, *, t, pairs):
    i = pl.program_id(2)
    nk = cr_ref.shape[2]
    hd = FOX_HEAD_DIM
    lane = lax.broadcasted_iota(jnp.int32, (t, LANES), 1)
    row = lax.broadcasted_iota(jnp.int32, (t, t), 0)
    col = lax.broadcasted_iota(jnp.int32, (t, t), 1)
    head_lanes = [lane < hd, lane >= hd]
    lanes_of = [slice(p * LANES, (p + 1) * LANES) for p in range(pairs)]

    @pl.when(i == 0)
    def _():
        step = 1024
        lane_s = lax.broadcasted_iota(jnp.int32, (step, LANES), 1)
        for p in range(pairs):
            def chunk(ci, mx, p=p):
                kf = k_ref[0, pl.ds(pl.multiple_of(ci * step, step), step), lanes_of[p]].astype(F32)
                k2 = kf * kf
                n0 = jnp.max(jnp.sum(jnp.where(lane_s < hd, k2, 0.0), axis=1, keepdims=True))
                n1 = jnp.max(jnp.sum(jnp.where(lane_s >= hd, k2, 0.0), axis=1, keepdims=True))
                return jnp.maximum(mx[0], n0), jnp.maximum(mx[1], n1)

            n0, n1 = lax.fori_loop(0, k_ref.shape[1] // step, chunk, (jnp.float32(0.0), jnp.float32(0.0)))
            kn_ref[2 * p] = jnp.sqrt(n0)
            kn_ref[2 * p + 1] = jnp.sqrt(n1)

    causal = jnp.concatenate([col <= row, col <= row], axis=0)
    qs, qn = [], []
    for p in range(pairs):
        qq = q_ref[0, :, lanes_of[p]]
        zero = jnp.zeros_like(qq)
        qs.append(jnp.concatenate([jnp.where(head_lanes[h], qq, zero) for h in range(2)], axis=0))
        q2 = qq.astype(F32) * qq.astype(F32)
        qn.append(jnp.concatenate(
            [jnp.sqrt(jnp.sum(jnp.where(head_lanes[h], q2, 0.0), axis=1, keepdims=True)) for h in range(2)],
            axis=0))

    def scores(p, j, diag):
        kk = k_ref[0, pl.ds(pl.multiple_of(j * t, t), t), lanes_of[p]]
        s = _dot_nt(qs[p], kk)
        s = jnp.concatenate(
            [s[h * t:(h + 1) * t] - cr_ref[0, 2 * p + h, pl.ds(j, 1), :] for h in range(2)], axis=0)
        return jnp.where(causal, s, -jnp.inf) if diag else s

    def values(p, j):
        return v_ref[0, pl.ds(pl.multiple_of(j * t, t), t), lanes_of[p]]

    def write_out(p, acc, l):
        o = acc / l
        o_ref[0, :, lanes_of[p]] = jnp.where(head_lanes[0], o[:t], o[t:]).astype(BF16)

    def first_round(js, j_next):
        state = []
        for p in range(pairs):
            m = l = acc = None
            for n, j in enumerate(js):
                s = scores(p, j, n == len(js) - 1)
                m_blk = jnp.max(s, axis=1, keepdims=True)
                if m is None:
                    m_next = m_blk
                    pr = jnp.exp(s - m_next)
                    l = jnp.sum(pr, axis=1, keepdims=True)
                    acc = _dot(pr.astype(BF16), values(p, j))
                else:
                    m_next = jnp.maximum(m, m_blk)
                    alpha = jnp.exp(m - m_next)
                    pr = jnp.exp(s - m_next)
                    l = alpha * l + jnp.sum(pr, axis=1, keepdims=True)
                    acc = alpha * acc + _dot(pr.astype(BF16), values(p, j))
                m = m_next
            state.append((m, l, acc))

        jidx = lax.broadcasted_iota(jnp.int32, (nk, 1), 0)
        first_needed = jnp.full((nk, 1), nk, jnp.int32)
        for p in range(pairs):
            m = state[p][0]
            for h in range(2):
                rows = slice(h * t, (h + 1) * t)
                gap = qn[p][rows] * (kn_ref[2 * p + h] * FOX_NORM_SLACK) - m[rows]
                reach = jnp.max(gap, axis=0, keepdims=True)
                c_min = jnp.min(cr_ref[0, 2 * p + h], axis=1, keepdims=True)
                first_needed = jnp.minimum(first_needed, jnp.where(reach - c_min >= -FOX_SKIP_GAP, jidx, nk))
        trips = j_next + 1 - jnp.minimum(jnp.min(first_needed), j_next + 1)
        trips_ref[0] = trips

        @pl.when(trips == 0)
        def _():
            for p in range(pairs):
                write_out(p, state[p][2], state[p][1])

        @pl.when(trips > 0)
        def _():
            for p in range(pairs):
                m, l, acc = state[p]
                m_ref[p] = jnp.broadcast_to(m, m_ref.shape[1:])
                l_ref[p] = jnp.broadcast_to(l, l_ref.shape[1:])
                acc_ref[p] = acc

    def block(j):
        for p in range(pairs):
            s = scores(p, j, False)
            m_prev = m_ref[p]
            m_next = jnp.maximum(m_prev, jnp.max(s, axis=1, keepdims=True))
            pr = jnp.exp(s - jnp.concatenate([m_next] * (t // LANES), axis=1))
            alpha = jnp.exp(m_prev - m_next)
            l_ref[p] = alpha * l_ref[p] + jnp.sum(pr, axis=1, keepdims=True)
            m_ref[p] = m_next
            acc_ref[p] = alpha * acc_ref[p] + _dot(pr.astype(BF16), values(p, j))

    window = FOX_FIRST_ROUND_BLOCKS

    @pl.when(i >= window - 1)
    def _():
        first_round([i - (window - 1) + n for n in range(window)], i - window)

    @pl.when(i < window - 1)
    def _():
        first_round([i], i - 1)

    trips = trips_ref[0]

    @pl.when(trips > 0)
    def _():
        j_next = jnp.where(i >= window - 1, i - window, i - 1)

        def earlier(n, carry):
            block(j_next - n)
            return carry

        lax.fori_loop(0, trips, earlier, 0)
        for p in range(pairs):
            write_out(p, acc_ref[p], l_ref[p])


def _fox_prompt(q, k, v, c_rows, *, t, pairs):
    b, l, w = q.shape
    ngrp = w // (pairs * LANES)
    nk = l // t
    wide = pairs * LANES
    return pl.pallas_call(
        functools.partial(_fox_prompt_body, t=t, pairs=pairs),
        grid=(b, ngrp, l // t),
        in_specs=[
            pl.BlockSpec((1, t, wide), lambda bi, g, i: (bi, i, g)),
            pl.BlockSpec((1, l, wide), lambda bi, g, i: (bi, 0, g), pipeline_mode=pl.Buffered(1)),
            pl.BlockSpec((1, l, wide), lambda bi, g, i: (bi, 0, g), pipeline_mode=pl.Buffered(1)),
            pl.BlockSpec((1, 2 * pairs, nk, t), lambda bi, g, i: (bi, g, 0, 0)),
        ],
        out_specs=pl.BlockSpec((1, t, wide), lambda bi, g, i: (bi, i, g)),
        out_shape=jax.ShapeDtypeStruct((b, l, w), BF16),
        scratch_shapes=[pltpu.VMEM((pairs, 2 * t, LANES), F32), pltpu.VMEM((pairs, 2 * t, LANES), F32),
                        pltpu.VMEM((pairs, 2 * t, LANES), F32), pltpu.SMEM((2 * pairs,), F32),
                        pltpu.SMEM((1,), jnp.int32)],
        compiler_params=_cparams("parallel", "parallel", "arbitrary"),
        name="fox_prompt",
    )(q, k, v, c_rows)


def _sample_scan_body(clf_ref, lfr_ref, cc_ref, cnr_ref):
    cc_ref[...] = _scan(clf_ref[0], 1)
    cnr_ref[...] = _scan(lfr_ref[...], 1)


def _fox_sample_scan(clf, lf_row, *, layer):
    _, r, p_len = clf.shape
    full = lambda a: pl.BlockSpec(a.shape, lambda i: (0,) * a.ndim)
    return pl.pallas_call(
        _sample_scan_body,
        grid=(1,),
        in_specs=[pl.BlockSpec((1, r, p_len), lambda i: (layer, 0, 0)), full(lf_row)],
        out_specs=[pl.BlockSpec((r, p_len), lambda i: (0, 0)), full(lf_row)],
        out_shape=[jax.ShapeDtypeStruct((r, p_len), F32), jax.ShapeDtypeStruct(lf_row.shape, F32)],
        compiler_params=_cparams("arbitrary"),
        name="fox_sample_scan",
    )(clf, lf_row)


def _fox_sample_body(q_ref, k_ref, v_ref, ck_ref, cv_ref, cc_ref, cnr_ref, o_ref):
    t = q_ref.shape[1]
    nh, p_len = ck_ref.shape[2], ck_ref.shape[4]
    hd = FOX_HEAD_DIM
    lane = lax.broadcasted_iota(jnp.int32, (t, LANES), 1)
    row = lax.broadcasted_iota(jnp.int32, (t, t), 0)
    col = lax.broadcasted_iota(jnp.int32, (t, t), 1)
    c_cache = cc_ref[...]
    cn_row = cnr_ref[...]
    outs = []
    for h in range(nh):
        grp = slice((h // 2) * LANES, (h // 2 + 1) * LANES)
        lo = (h % 2) * hd
        qq = q_ref[0, :, grp]
        qm = jnp.where((lane >= lo) & (lane < lo + hd), qq, jnp.zeros_like(qq))
        q_h = qq[:, lo:lo + hd]
        kc_t = ck_ref[0, 0, h].astype(BF16)
        vc_t = cv_ref[0, 0, h].astype(BF16)
        c_h = c_cache[h:h + 1, :]
        tot = c_h[:, p_len - 1:p_len]
        s_c = _dot(q_h, kc_t) + (tot - c_h)
        s_s = _dot_nt(qm, k_ref[0, :, grp]) - cn_row[h:h + 1, :t]
        s_s = jnp.where(col <= row, s_s, -jnp.inf)
        m = jnp.maximum(jnp.max(s_c, axis=1, keepdims=True), jnp.max(s_s, axis=1, keepdims=True))
        p_c = jnp.exp(s_c - m)
        p_s = jnp.exp(s_s - m)
        den = jnp.sum(p_c, axis=1, keepdims=True) + jnp.sum(p_s, axis=1, keepdims=True)
        o = _dot_nt(p_c.astype(BF16), vc_t) + _dot(p_s.astype(BF16), v_ref[0, :, grp])[:, lo:lo + hd]
        outs.append(o / den)
    o_ref[0] = jnp.concatenate(outs, axis=1).astype(BF16)


def _fox_sample(q, k, v, cache_k, cache_v, c_cache, cn_row, *, layer):
    b, t, w = q.shape
    nh, hd, p_len = cache_k.shape[2:]
    tokq = pl.BlockSpec((1, t, w), lambda bi: (bi, 0, 0))
    cache = pl.BlockSpec((1, 1, nh, hd, p_len), lambda bi: (layer, bi, 0, 0, 0))
    return pl.pallas_call(
        _fox_sample_body,
        grid=(b,),
        in_specs=[tokq, tokq, tokq, cache, cache,
                  pl.BlockSpec((nh, p_len), lambda bi: (bi, 0)),
                  pl.BlockSpec((nh, LANES), lambda bi: (bi, 0))],
        out_specs=tokq,
        out_shape=jax.ShapeDtypeStruct((b, t, w), BF16),
        compiler_params=_cparams("parallel"),
        name="fox_sample",
    )(q, k, v, cache_k, cache_v, c_cache, cn_row)


def _hgrn_body(q_ref, lf_ref, k_ref, v_ref, s0_ref, o_ref, so_ref, st_ref, tmp_ref, *, c, n_chunks, hps, zero_init):
    ti = pl.program_id(2)
    d = HGRN_HEAD_DIM

    @pl.when(ti == 0)
    def _():
        for hh in range(hps):
            st_ref[hh] = jnp.zeros((d, d), F32) if zero_init else s0_ref[0, 0, hh].T

    row = lax.broadcasted_iota(jnp.int32, (c, c), 0)
    col = lax.broadcasted_iota(jnp.int32, (c, c), 1)
    rowc = lax.broadcasted_iota(jnp.int32, (c, 1), 0)
    half = c // 2
    b_all = _scan(lf_ref[0], 0, period=c)

    span = jnp.zeros((1, hps * d), F32)
    for ci in range(n_chunks):
        b = b_all[ci * c:(ci + 1) * c]
        b_mid = b[half - 1:half]
        span = jnp.maximum(span, jnp.maximum(b[0:1] - b_mid, b_mid - b[c - 1:c]))
    factorable = jnp.max(span) <= HGRN_FACTOR_LIMIT

    def carry_state(st, k, v, b):
        b_last = b[c - 1:c]
        kh = (k * jnp.exp(b_last - b)).astype(BF16)
        return st * jnp.exp(b_last) + _dot_tn(v, kh)

    @pl.when(factorable)
    def _():
        units = [(hh, ci) for hh in range(hps) for ci in range(n_chunks)]
        tile_of = {u: (slice(u[1] * c, (u[1] + 1) * c), slice(u[0] * d, (u[0] + 1) * d)) for u in units}
        sc, ds, qe, grow = {}, {}, {}, {}
        for u in units:
            rs, hs = tile_of[u]
            q = q_ref[0, rs, hs]
            k = k_ref[0, rs, hs]
            b = b_all[rs, hs]
            b_mid = b[half - 1:half]
            b_last = b[c - 1:c]
            qe[u] = (q * jnp.exp(b)).astype(BF16)
            grow[u] = jnp.exp(b_last)
            qt = (q * jnp.exp(b - b_mid)).astype(BF16)
            kt = (k * jnp.exp(b_mid - b)).astype(BF16)
            kh = (k * jnp.exp(b_last - b)).astype(BF16)
            sc[u] = _dot_nt(qt, kt)
            ds[u] = _dot_tn(v_ref[0, rs, hs], kh)
        st_in = {}
        for hh in range(hps):
            st = st_ref[hh]
            for ci in range(n_chunks):
                st_in[(hh, ci)] = st.astype(BF16)
                st = st * grow[(hh, ci)] + ds[(hh, ci)]
            st_ref[hh] = st
        for u in units:
            rs, hs = tile_of[u]
            intra = _dot(jnp.where(col <= row, sc[u], 0.0).astype(BF16), v_ref[0, rs, hs])
            o_ref[0, rs, hs] = _dot_nt(qe[u], st_in[u]) + intra

    @pl.when(jnp.logical_not(factorable))
    def _():
        for hh in range(hps):
            hs = slice(hh * d, (hh + 1) * d)

            def chunk(ci, carry, hh=hh, hs=hs):
                rs = pl.ds(pl.multiple_of(ci * c, c), c)
                q = q_ref[0, rs, hs]
                k = k_ref[0, rs, hs]
                v = v_ref[0, rs, hs]
                b = _scan(lf_ref[0, rs, hs], 0)
                st = st_ref[hh]
                tmp_ref[0] = b
                tmp_ref[1] = k
                tmp_ref[2] = v.astype(F32)

                def key_row(s, acc):
                    bs = tmp_ref[0, pl.ds(s, 1), :]
                    ks = tmp_ref[1, pl.ds(s, 1), :]
                    vs = tmp_ref[2, pl.ds(s, 1), :]
                    w = jnp.exp(jnp.minimum(b - bs, 0.0))
                    a = jnp.sum(q * ks * w, axis=1, keepdims=True)
                    return acc + jnp.where(rowc >= s, a, 0.0) * vs

                o = _dot_nt((q * jnp.exp(b)).astype(BF16), st.astype(BF16))
                o_ref[0, rs, hs] = lax.fori_loop(0, c, key_row, o)
                st_ref[hh] = carry_state(st, k, v, b)
                return carry

            lax.fori_loop(0, n_chunks, chunk, 0)

    @pl.when(ti == pl.num_programs(2) - 1)
    def _():
        for hh in range(hps):
            so_ref[0, 0, hh] = st_ref[hh].T


def _hgrn(hq, hlf, hk, hv, s0, carried, *, layer, c, tl, hps):
    b, l, w = hq.shape
    d = HGRN_HEAD_DIM
    nh = w // d
    tok = pl.BlockSpec((1, tl, hps * d), lambda bi, h, ti: (bi, ti, h))
    st = pl.BlockSpec((1, 1, hps, d, d), lambda bi, h, ti: (layer, bi, h, 0, 0))
    zero_init = s0 is None
    body = functools.partial(_hgrn_body, c=c, n_chunks=tl // c, hps=hps, zero_init=zero_init)
    args, in_specs = [hq, hlf, hk, hv], [tok, tok, tok, tok]
    if zero_init:
        core = body
        body = lambda q, lf, k, v, *rest: core(q, lf, k, v, None, *rest)
    else:
        args.append(s0)
        in_specs.append(st)
    aliases = {}
    if carried is not None:
        aliases[len(args)] = 1
        body = _drop_alias_refs(body, len(args), 1)
        args.append(carried)
        in_specs.append(_ANY)
    return pl.pallas_call(
        body,
        grid=(b, nh // hps, l // tl),
        in_specs=in_specs,
        out_specs=[tok, st],
        out_shape=[jax.ShapeDtypeStruct((b, l, w), F32), jax.ShapeDtypeStruct((DEPTH, b, nh, d, d), F32)],
        scratch_shapes=[pltpu.VMEM((hps, d, d), F32), pltpu.VMEM((3, c, d), F32)],
        input_output_aliases=aliases,
        compiler_params=_cparams("parallel", "parallel", "arbitrary"),
        name="hgrn",
    )(*args)


def kernel(x_prompt, x_sample, cache_k, cache_v, cache_logf, state_hgrn, norm_ffn1, ffn1_wi, ffn1_wo,
           norm_mix, w_in, b_fgate, hgrn_lb, hgrn_gnorm, w_out, norm_ffn2, ffn2_wi, ffn2_wo, norm_final):
    bp, lp, d = x_prompt.shape
    bs, ls, _ = x_sample.shape
    mp, ms = bp * lp, bs * ls
    p_len = cache_k.shape[3]
    f, w, h8 = FOX_WIDTH, HGRN_WIDTH, FOX_HEADS
    tm_ffn, tf_ffn, tm_proj, t_attn, c_hgrn, tl_hgrn = 512, 256, 512, 256, 64, 1024

    xp = x_prompt.reshape(mp, d)
    xs = x_sample.reshape(ms, d)
    gfin = norm_final.reshape(1, d)
    g1, g2, gm = (a.reshape(DEPTH, 1, d) for a in (norm_ffn1, norm_ffn2, norm_mix))
    gn = hgrn_gnorm.reshape(DEPTH, 1, HGRN_HEAD_DIM)
    clf = cache_logf.reshape(DEPTH, bs * h8, p_len)
    cache_kt = jnp.swapaxes(cache_k, 3, 4)
    cache_vt = jnp.swapaxes(cache_v, 3, 4)

    carry_p = carry_s = None
    st_p = st_s = None
    lfs = []
    for l in range(DEPTH):
        last = l == DEPTH - 1
        wl = w_in[l]
        wm = jnp.concatenate([wl[:, :3 * f], wl[:, 3 * f + h8:]], axis=1).astype(BF16)
        wf = jnp.pad(wl[:, 3 * f:3 * f + h8], ((0, 0), (0, LANES - h8))).astype(BF16)
        bfp = jnp.pad(b_fgate[l].reshape(1, h8), ((0, 0), (0, LANES - h8)))
        wi1, wo1 = _layer_bf16(ffn1_wi, layer=l, steps=4), _layer_bf16(ffn1_wo, layer=l, steps=2)
        wi2, wo2 = _layer_bf16(ffn2_wi, layer=l, steps=4), _layer_bf16(ffn2_wo, layer=l, steps=2)
        wout = _layer_bf16(w_out, layer=l, steps=2)

        xp = _ffn(xp, g1, wi1, wo1, gfin, layer=l, final_norm=False, tm=tm_ffn, tf=tf_ffn)
        xs = _ffn(xs, g1, wi1, wo1, gfin, layer=l, final_norm=False, tm=ms, tf=tf_ffn)

        (q, k, v, kh, vh, lfr, hq, hlf, hk, hv, hg) = _inproj(
            xp, gm, wm, wf, bfp, hgrn_lb, carry_p, layer=l, sample=False, batch=bp, tm=tm_proj)
        carry_p = (kh, vh, lfr)
        (q_s, k_s, v_s, kh_s, vh_s, lfr_s, hq_s, hlf_s, hk_s, hv_s, hg_s) = _inproj(
            xs, gm, wm, wf, bfp, hgrn_lb, carry_s, layer=l, sample=True, batch=bs, tm=ms)
        carry_s = (kh_s, vh_s)

        c_row = _fox_cumsum(lfr, layer=l)
        fo = _fox_prompt(q.reshape(bp, lp, f), k.reshape(bp, lp, f), v.reshape(bp, lp, f),
                         c_row.reshape(bp, h8, lp // t_attn, t_attn), t=t_attn, pairs=4)
        lfs_l = lfr_s.reshape(h8, bs, ls).transpose(1, 0, 2)
        lfs.append(lfs_l)
        lfs_pad = jnp.pad(lfs_l, ((0, 0), (0, 0), (0, LANES - ls))).reshape(bs * h8, LANES)
        c_cache, cn_row = _fox_sample_scan(clf, lfs_pad, layer=l)
        fo_s = _fox_sample(q_s.reshape(bs, ls, f), k_s.reshape(bs, ls, f), v_s.reshape(bs, ls, f),
                           cache_kt, cache_vt, c_cache, cn_row, layer=l)
        to3 = lambda a, b_, l_: a.reshape(b_, l_, w)
        ho, st_p = _hgrn(to3(hq, bp, lp), to3(hlf, bp, lp), to3(hk, bp, lp), to3(hv, bp, lp), None, st_p,
                         layer=l, c=c_hgrn, tl=tl_hgrn, hps=2)
        ho_s, st_s = _hgrn(to3(hq_s, bs, ls), to3(hlf_s, bs, ls), to3(hk_s, bs, ls), to3(hv_s, bs, ls),
                           state_hgrn, st_s, layer=l, c=ls, tl=ls, hps=HGRN_HEADS)

        xp = _ffn(xp, g2, wi2, wo2, gfin, (fo.reshape(mp, f), ho.reshape(mp, w), hg, gn, wout),
                  layer=l, final_norm=last, tm=tm_ffn, tf=tf_ffn)
        xs = _ffn(xs, g2, wi2, wo2, gfin, (fo_s.reshape(ms, f), ho_s.reshape(ms, w), hg_s, gn, wout),
                  layer=l, final_norm=last, tm=ms, tf=tf_ffn)

    k_prompt = jnp.swapaxes(carry_p[0], 3, 4)
    v_prompt = jnp.swapaxes(carry_p[1], 3, 4)
    return (xp.reshape(bp, lp, d), xs.reshape(bs, ls, d), k_prompt, v_prompt, carry_p[2], st_p,
            carry_s[0], carry_s[1], jnp.stack(lfs), st_s)
```

```python
import functools

import jax
import jax.numpy as jnp
from jax import lax
from jax.experimental import pallas as pl
from jax.experimental.pallas import tpu as pltpu

F32 = jnp.float32
BF16 = jnp.bfloat16

DEPTH = 2
FOX_HEADS = 8
FOX_HEAD_DIM = 64
FOX_WIDTH = FOX_HEADS * FOX_HEAD_DIM
HGRN_HEADS = 4
HGRN_HEAD_DIM = 128
HGRN_WIDTH = HGRN_HEADS * HGRN_HEAD_DIM
RMS_EPS = 1e-6
FOX_SCALE = FOX_HEAD_DIM ** -0.5

LANES = 128
VMEM_LIMIT_BYTES = 56 * 1024 * 1024
HGRN_FACTOR_LIMIT = 50.0
FOX_SKIP_GAP = 106.0
FOX_NORM_SLACK = 1.001
FOX_FIRST_ROUND_BLOCKS = 3

TM_FFN = 512
TF_FFN = 256
TM_PROJ = 512
T_ATTN = 256
FOX_PAIRS = 4
C_HGRN = 64
TL_HGRN = 1024
HGRN_PROMPT_HEADS = 2
HGRN_SAMPLE_STREAMS = 4
KEY_NORM_ROWS = 1024


def _cparams(*sem):
    return pltpu.CompilerParams(dimension_semantics=sem, vmem_limit_bytes=VMEM_LIMIT_BYTES)


def _rms(x, g):
    ms = jnp.mean(x * x, axis=-1, keepdims=True)
    return x * lax.rsqrt(ms + RMS_EPS) * g


def _silu(x):
    return x * jax.nn.sigmoid(x)


def _log1p_exp_neg(d):
    return jnp.log(1.0 + jnp.exp(-d))


def _log_sigmoid(z):
    return jnp.minimum(z, 0.0) - _log1p_exp_neg(jnp.abs(z))


def _gate_terms(z):
    e = jnp.exp(-jnp.abs(z))
    t = 1.0 + e
    return jnp.minimum(z, 0.0) - jnp.log(t), jnp.where(z >= 0.0, e, 1.0) / t


def _dot(a, b):
    return jnp.dot(a, b, preferred_element_type=F32)


def _dot_nt(a, b):
    return lax.dot_general(a, b, (((1,), (1,)), ((), ())), preferred_element_type=F32)


def _dot_tn(a, b):
    return lax.dot_general(a, b, (((0,), (0,)), ((), ())), preferred_element_type=F32)


def _scan(x, axis, period=None):
    n = x.shape[axis] if period is None else period
    idx = lax.broadcasted_iota(jnp.int32, x.shape, axis)
    if period is not None:
        idx = idx & (period - 1)
    s = 1
    while s < n:
        x = x + jnp.where(idx >= s, pltpu.roll(x, s, axis), 0.0)
        s *= 2
    return x


def _drop_alias_refs(body, n_in, n_alias):
    def wrapped(*refs):
        return body(*refs[:n_in], *refs[n_in + n_alias:])
    return wrapped


_ANY = pl.BlockSpec(memory_space=pl.ANY)


def _cast_body(w_ref, o_ref):
    o_ref[...] = w_ref[0].astype(o_ref.dtype)


def _layer_bf16(w, *, layer, steps):
    _, rows, cols = w.shape
    rb = rows // steps
    return pl.pallas_call(
        _cast_body,
        grid=(steps,),
        in_specs=[pl.BlockSpec((1, rb, cols), lambda i: (layer, i, 0))],
        out_specs=pl.BlockSpec((rb, cols), lambda i: (i, 0)),
        out_shape=jax.ShapeDtypeStruct((rows, cols), BF16),
        compiler_params=_cparams("parallel"),
        name="to_bf16",
    )(w)


def _mix(fo_ref, ho_ref, hg_ref, gn_ref):
    gn = gn_ref[0]
    d = HGRN_HEAD_DIM
    parts = [fo_ref[...]]
    for h in range(HGRN_HEADS):
        ho = ho_ref[:, h * d:(h + 1) * d]
        parts.append((_rms(ho, gn) * _silu(hg_ref[:, h * d:(h + 1) * d])).astype(BF16))
    return jnp.concatenate(parts, axis=1)


def _ffn_body(*refs, final_norm, with_outproj, tf):
    if with_outproj:
        (x_ref, fo_ref, ho_ref, hg_ref, gn_ref, wout_ref,
         g_ref, wi_ref, wo_ref, gf_ref, o_ref, acc_ref) = refs
        x = x_ref[...] + _dot(_mix(fo_ref, ho_ref, hg_ref, gn_ref), wout_ref[...])
    else:
        x_ref, g_ref, wi_ref, wo_ref, gf_ref, o_ref, acc_ref = refs
        x = x_ref[...]
    f = wo_ref.shape[0]
    xn = _rms(x, g_ref[0]).astype(BF16)
    for c in range(f // tf):
        a = _dot(xn, wi_ref[:, c * tf:(c + 1) * tf])
        b = _dot(xn, wi_ref[:, f + c * tf:f + (c + 1) * tf])
        part = _dot((_silu(a) * b).astype(BF16), wo_ref[c * tf:(c + 1) * tf, :])
        if c == 0:
            acc_ref[...] = part
        else:
            acc_ref[...] += part
    y = x + 0.5 * acc_ref[...]
    if final_norm:
        y = _rms(y, gf_ref[...])
    o_ref[...] = y


def _ffn(x, g, wi, wo, gf, outproj=None, *, layer, final_norm, tm, tf):
    m, d = x.shape
    tok = lambda width: pl.BlockSpec((tm, width), lambda i: (i, 0))
    held = lambda a: pl.BlockSpec(a.shape, lambda i: (0,) * a.ndim, pipeline_mode=pl.Buffered(1))
    args, in_specs = [x], [tok(d)]
    if outproj is not None:
        fo, ho, hg, gn, wout = outproj
        args += [fo, ho, hg, gn, wout]
        in_specs += [tok(fo.shape[1]), tok(ho.shape[1]), tok(hg.shape[1]),
                     pl.BlockSpec((1, 1, gn.shape[2]), lambda i: (layer, 0, 0)), held(wout)]
    args += [g, wi, wo, gf]
    in_specs += [pl.BlockSpec((1, 1, d), lambda i: (layer, 0, 0)), held(wi), held(wo),
                 pl.BlockSpec((1, d), lambda i: (0, 0))]
    return pl.pallas_call(
        functools.partial(_ffn_body, final_norm=final_norm, with_outproj=outproj is not None, tf=tf),
        grid=(m // tm,),
        in_specs=in_specs,
        out_specs=tok(d),
        out_shape=jax.ShapeDtypeStruct((m, d), F32),
        scratch_shapes=[pltpu.VMEM((tm, d), F32)],
        compiler_params=_cparams("parallel"),
        name="ffn",
    )(*args)


def _inproj_body(x_ref, g_ref, wm_ref, wf_ref, bf_ref, lbp_ref,
                 q_ref, k_ref, v_ref, kh_ref, vh_ref, lfr_ref,
                 hq_ref, hlf_ref, hk_ref, hv_ref, hg_ref, *, layer, sample):
    h = _rms(x_ref[...], g_ref[0]).astype(BF16)
    w, hd = FOX_WIDTH, FOX_HEAD_DIM
    wc = 2 * LANES
    heads_per_chunk = wc // hd

    def chunks(i):
        for n in range(w // wc):
            yield n, slice(n * wc, (n + 1) * wc), _dot(h, wm_ref[:, i * w + n * wc:i * w + (n + 1) * wc])

    def store_heads(ref, n, val):
        h0 = n * heads_per_chunk
        if sample:
            for hh in range(heads_per_chunk):
                piece = val[:, hh * hd:(hh + 1) * hd]
                ref[0, :, h0 + hh] = piece.reshape(ref.shape[1], ref.shape[3], hd)
        else:
            ref[0, 0, h0:h0 + heads_per_chunk] = val.T.reshape(heads_per_chunk, hd, val.shape[0])

    p = lbp_ref[...]
    e = jnp.exp(p - jnp.max(p, axis=0, keepdims=True))
    sm = e / jnp.sum(e, axis=0, keepdims=True)
    cs = sm[0:1]
    for r in range(1, layer + 1):
        cs = cs + sm[r:r + 1]
    lb = cs - sm[0:1]
    log_lb = jnp.log(lb)
    log_1m_lb = jnp.log1p(-lb)

    for n, cols, z in chunks(4):
        log_sig, sig_neg = _gate_terms(z)
        a = log_lb[:, cols]
        b = log_1m_lb[:, cols] + log_sig
        hlf_ref[:, cols] = jnp.maximum(a, b) + _log1p_exp_neg(jnp.abs(a - b))
        hk_ref[:, cols] = (1.0 - lb[:, cols]) * sig_neg
    for n, cols, hq in chunks(3):
        hq_ref[:, cols] = _silu(hq)

    lf = _log_sigmoid(_dot(h, wf_ref[...]) + bf_ref[...])
    lfr = lf.T[:FOX_HEADS, :]
    if sample:
        lfr_ref[...] = lfr
    else:
        lfr_ref[0, 0] = lfr

    for n, cols, q in chunks(0):
        q_ref[:, cols] = (q * FOX_SCALE).astype(BF16)
    for n, cols, k in chunks(1):
        store_heads(kh_ref, n, k)
        k_ref[:, cols] = k.astype(BF16)
    for n, cols, v in chunks(2):
        store_heads(vh_ref, n, v)
        v_ref[:, cols] = v.astype(BF16)
    for n, cols, hv in chunks(5):
        hv_ref[:, cols] = hv.astype(BF16)
    for n, cols, hg in chunks(6):
        hg_ref[:, cols] = hg


def _inproj(x, g, wm, wf, bfp, lbp, carried, *, layer, sample, batch, tm):
    m, d = x.shape
    w, hd, nh = FOX_WIDTH, FOX_HEAD_DIM, FOX_HEADS
    seq = m // batch
    tiles_per_seq = seq // tm if not sample else 1
    tok = lambda width: pl.BlockSpec((tm, width), lambda i: (i, 0))
    full = lambda a: pl.BlockSpec(a.shape, lambda i: (0,) * a.ndim)
    sds = jax.ShapeDtypeStruct
    if sample:
        heads_shape = (DEPTH, batch, nh, seq, hd)
        heads_spec = pl.BlockSpec((1, batch, nh, seq, hd), lambda i: (layer, 0, 0, 0, 0))
        lfr_shape, lfr_spec = (nh, m), pl.BlockSpec((nh, m), lambda i: (0, 0))
    else:
        heads_shape = (DEPTH, batch, nh, hd, seq)
        heads_spec = pl.BlockSpec((1, 1, nh, hd, tm),
                                  lambda i: (layer, i // tiles_per_seq, 0, 0, i % tiles_per_seq))
        lfr_shape = (DEPTH, batch, nh, seq)
        lfr_spec = pl.BlockSpec((1, 1, nh, tm), lambda i: (layer, i // tiles_per_seq, 0, i % tiles_per_seq))
    out_specs = [tok(w), tok(w), tok(w), heads_spec, heads_spec, lfr_spec,
                 tok(w), tok(w), tok(w), tok(w), tok(w)]
    out_shape = [sds((m, w), BF16), sds((m, w), BF16), sds((m, w), BF16),
                 sds(heads_shape, F32), sds(heads_shape, F32), sds(lfr_shape, F32),
                 sds((m, w), F32), sds((m, w), F32), sds((m, w), F32), sds((m, w), BF16), sds((m, w), F32)]
    body = functools.partial(_inproj_body, layer=layer, sample=sample)
    args = [x, g, wm, wf, bfp, lbp]
    in_specs = [tok(d), pl.BlockSpec((1, 1, d), lambda i: (layer, 0, 0)), full(wm), full(wf), full(bfp), full(lbp)]
    aliases = {}
    if carried is not None:
        out_index = (3, 4, 5)
        for n, buf in enumerate(carried):
            aliases[len(args)] = out_index[n]
            args.append(buf)
            in_specs.append(_ANY)
        body = _drop_alias_refs(body, 6, len(carried))
    return pl.pallas_call(
        body, grid=(m // tm,), in_specs=in_specs, out_specs=out_specs, out_shape=out_shape,
        input_output_aliases=aliases, compiler_params=_cparams("arbitrary"), name="inproj",
    )(*args)


def _cumsum_body(lfr_ref, cr_ref):
    cr_ref[0] = _scan(lfr_ref[0, 0], 1)


def _fox_cumsum(lf_rows, *, layer):
    _, b, h, l = lf_rows.shape
    return pl.pallas_call(
        _cumsum_body,
        grid=(b,),
        in_specs=[pl.BlockSpec((1, 1, h, l), lambda i: (layer, i, 0, 0))],
        out_specs=pl.BlockSpec((1, h, l), lambda i: (i, 0, 0)),
        out_shape=jax.ShapeDtypeStruct((b, h, l), F32),
        compiler_params=_cparams("parallel"),
        name="fox_cumsum",
    )(lf_rows)


def _fox_prompt_body(q_ref, k_ref, v_ref, cr_ref, o_ref, m_ref, l_ref, acc_ref, kn_ref, trips_ref, *, t, pairs):
    i = pl.program_id(2)
    nk = cr_ref.shape[2]
    hd = FOX_HEAD_DIM
    lane = lax.broadcasted_iota(jnp.int32, (t, LANES), 1)
    row = lax.broadcasted_iota(jnp.int32, (t, t), 0)
    col = lax.broadcasted_iota(jnp.int32, (t, t), 1)
    head_lanes = [lane < hd, lane >= hd]
    lanes_of = [slice(p * LANES, (p + 1) * LANES) for p in range(pairs)]

    @pl.when(i == 0)
    def _():
        step = KEY_NORM_ROWS
        lane_s = lax.broadcasted_iota(jnp.int32, (step, LANES), 1)
        for p in range(pairs):
            def chunk(ci, mx, p=p):
                kf = k_ref[0, pl.ds(pl.multiple_of(ci * step, step), step), lanes_of[p]].astype(F32)
                k2 = kf * kf
                n0 = jnp.max(jnp.sum(jnp.where(lane_s < hd, k2, 0.0), axis=1, keepdims=True))
                n1 = jnp.max(jnp.sum(jnp.where(lane_s >= hd, k2, 0.0), axis=1, keepdims=True))
                return jnp.maximum(mx[0], n0), jnp.maximum(mx[1], n1)

            n0, n1 = lax.fori_loop(0, k_ref.shape[1] // step, chunk, (jnp.float32(0.0), jnp.float32(0.0)))
            kn_ref[2 * p] = jnp.sqrt(n0)
            kn_ref[2 * p + 1] = jnp.sqrt(n1)

    causal = jnp.concatenate([col <= row, col <= row], axis=0)
    qs, qn = [], []
    for p in range(pairs):
        qq = q_ref[0, :, lanes_of[p]]
        zero = jnp.zeros_like(qq)
        qs.append(jnp.concatenate([jnp.where(head_lanes[h], qq, zero) for h in range(2)], axis=0))
        q2 = qq.astype(F32) * qq.astype(F32)
        qn.append(jnp.concatenate(
            [jnp.sqrt(jnp.sum(jnp.where(head_lanes[h], q2, 0.0), axis=1, keepdims=True)) for h in range(2)],
            axis=0))

    def scores(p, j, diag):
        kk = k_ref[0, pl.ds(pl.multiple_of(j * t, t), t), lanes_of[p]]
        s = _dot_nt(qs[p], kk)
        s = jnp.concatenate(
            [s[h * t:(h + 1) * t] - cr_ref[0, 2 * p + h, pl.ds(j, 1), :] for h in range(2)], axis=0)
        return jnp.where(causal, s, -jnp.inf) if diag else s

    def values(p, j):
        return v_ref[0, pl.ds(pl.multiple_of(j * t, t), t), lanes_of[p]]

    def write_out(p, acc, l):
        o = acc / l
        o_ref[0, :, lanes_of[p]] = jnp.where(head_lanes[0], o[:t], o[t:]).astype(BF16)

    def first_round(js, j_next):
        state = []
        for p in range(pairs):
            m = l = acc = None
            for n, j in enumerate(js):
                s = scores(p, j, n == len(js) - 1)
                m_blk = jnp.max(s, axis=1, keepdims=True)
                if m is None:
                    m_next = m_blk
                    pr = jnp.exp(s - m_next)
                    l = jnp.sum(pr, axis=1, keepdims=True)
                    acc = _dot(pr.astype(BF16), values(p, j))
                else:
                    m_next = jnp.maximum(m, m_blk)
                    alpha = jnp.exp(m - m_next)
                    pr = jnp.exp(s - m_next)
                    l = alpha * l + jnp.sum(pr, axis=1, keepdims=True)
                    acc = alpha * acc + _dot(pr.astype(BF16), values(p, j))
                m = m_next
            state.append((m, l, acc))

        jidx = lax.broadcasted_iota(jnp.int32, (nk, 1), 0)
        first_needed = jnp.full((nk, 1), nk, jnp.int32)
        for p in range(pairs):
            m = state[p][0]
            for h in range(2):
                rows = slice(h * t, (h + 1) * t)
                gap = qn[p][rows] * (kn_ref[2 * p + h] * FOX_NORM_SLACK) - m[rows]
                reach = jnp.max(gap, axis=0, keepdims=True)
                c_min = jnp.min(cr_ref[0, 2 * p + h], axis=1, keepdims=True)
                first_needed = jnp.minimum(first_needed, jnp.where(reach - c_min >= -FOX_SKIP_GAP, jidx, nk))
        trips = j_next + 1 - jnp.minimum(jnp.min(first_needed), j_next + 1)
        trips_ref[0] = trips

        @pl.when(trips == 0)
        def _():
            for p in range(pairs):
                write_out(p, state[p][2], state[p][1])

        @pl.when(trips > 0)
        def _():
            for p in range(pairs):
                m, l, acc = state[p]
                m_ref[p] = jnp.broadcast_to(m, m_ref.shape[1:])
                l_ref[p] = jnp.broadcast_to(l, l_ref.shape[1:])
                acc_ref[p] = acc

    def block(j):
        for p in range(pairs):
            s = scores(p, j, False)
            m_prev = m_ref[p]
            m_next = jnp.maximum(m_prev, jnp.max(s, axis=1, keepdims=True))
            pr = jnp.exp(s - jnp.concatenate([m_next] * (t // LANES), axis=1))
            alpha = jnp.exp(m_prev - m_next)
            l_ref[p] = alpha * l_ref[p] + jnp.sum(pr, axis=1, keepdims=True)
            m_ref[p] = m_next
            acc_ref[p] = alpha * acc_ref[p] + _dot(pr.astype(BF16), values(p, j))

    window = FOX_FIRST_ROUND_BLOCKS

    @pl.when(i >= window - 1)
    def _():
        first_round([i - (window - 1) + n for n in range(window)], i - window)

    @pl.when(i < window - 1)
    def _():
        first_round([i], i - 1)

    trips = trips_ref[0]

    @pl.when(trips > 0)
    def _():
        j_next = jnp.where(i >= window - 1, i - window, i - 1)

        def earlier(n, carry):
            block(j_next - n)
            return carry

        lax.fori_loop(0, trips, earlier, 0)
        for p in range(pairs):
            write_out(p, acc_ref[p], l_ref[p])


def _fox_prompt(q, k, v, c_rows, *, t, pairs):
    b, l, w = q.shape
    ngrp = w // (pairs * LANES)
    nk = l // t
    wide = pairs * LANES
    return pl.pallas_call(
        functools.partial(_fox_prompt_body, t=t, pairs=pairs),
        grid=(b, ngrp, l // t),
        in_specs=[
            pl.BlockSpec((1, t, wide), lambda bi, g, i: (bi, i, g)),
            pl.BlockSpec((1, l, wide), lambda bi, g, i: (bi, 0, g), pipeline_mode=pl.Buffered(1)),
            pl.BlockSpec((1, l, wide), lambda bi, g, i: (bi, 0, g), pipeline_mode=pl.Buffered(1)),
            pl.BlockSpec((1, 2 * pairs, nk, t), lambda bi, g, i: (bi, g, 0, 0)),
        ],
        out_specs=pl.BlockSpec((1, t, wide), lambda bi, g, i: (bi, i, g)),
        out_shape=jax.ShapeDtypeStruct((b, l, w), BF16),
        scratch_shapes=[pltpu.VMEM((pairs, 2 * t, LANES), F32), pltpu.VMEM((pairs, 2 * t, LANES), F32),
                        pltpu.VMEM((pairs, 2 * t, LANES), F32), pltpu.SMEM((2 * pairs,), F32),
                        pltpu.SMEM((1,), jnp.int32)],
        compiler_params=_cparams("parallel", "parallel", "arbitrary"),
        name="fox_prompt",
    )(q, k, v, c_rows)


def _sample_scan_body(clf_ref, lfr_ref, cc_ref, cnr_ref):
    cc_ref[...] = _scan(clf_ref[0], 1)
    cnr_ref[...] = _scan(lfr_ref[...], 1)


def _fox_sample_scan(clf, lf_row, *, layer):
    _, r, p_len = clf.shape
    full = lambda a: pl.BlockSpec(a.shape, lambda i: (0,) * a.ndim)
    return pl.pallas_call(
        _sample_scan_body,
        grid=(1,),
        in_specs=[pl.BlockSpec((1, r, p_len), lambda i: (layer, 0, 0)), full(lf_row)],
        out_specs=[pl.BlockSpec((r, p_len), lambda i: (0, 0)), full(lf_row)],
        out_shape=[jax.ShapeDtypeStruct((r, p_len), F32), jax.ShapeDtypeStruct(lf_row.shape, F32)],
        compiler_params=_cparams("arbitrary"),
        name="fox_sample_scan",
    )(clf, lf_row)


def _fox_sample_body(q_ref, k_ref, v_ref, ck_ref, cv_ref, cc_ref, cnr_ref, o_ref):
    t = q_ref.shape[1]
    nh, p_len = ck_ref.shape[2], ck_ref.shape[4]
    hd = FOX_HEAD_DIM
    lane = lax.broadcasted_iota(jnp.int32, (t, LANES), 1)
    row = lax.broadcasted_iota(jnp.int32, (t, t), 0)
    col = lax.broadcasted_iota(jnp.int32, (t, t), 1)
    c_cache = cc_ref[...]
    cn_row = cnr_ref[...]
    outs = []
    for h in range(nh):
        grp = slice((h // 2) * LANES, (h // 2 + 1) * LANES)
        lo = (h % 2) * hd
        qq = q_ref[0, :, grp]
        qm = jnp.where((lane >= lo) & (lane < lo + hd), qq, jnp.zeros_like(qq))
        q_h = qq[:, lo:lo + hd]
        kc_t = ck_ref[0, 0, h].astype(BF16)
        vc_t = cv_ref[0, 0, h].astype(BF16)
        c_h = c_cache[h:h + 1, :]
        tot = c_h[:, p_len - 1:p_len]
        s_c = _dot(q_h, kc_t) + (tot - c_h)
        s_s = _dot_nt(qm, k_ref[0, :, grp]) - cn_row[h:h + 1, :t]
        s_s = jnp.where(col <= row, s_s, -jnp.inf)
        m = jnp.maximum(jnp.max(s_c, axis=1, keepdims=True), jnp.max(s_s, axis=1, keepdims=True))
        p_c = jnp.exp(s_c - m)
        p_s = jnp.exp(s_s - m)
        den = jnp.sum(p_c, axis=1, keepdims=True) + jnp.sum(p_s, axis=1, keepdims=True)
        o = _dot_nt(p_c.astype(BF16), vc_t) + _dot(p_s.astype(BF16), v_ref[0, :, grp])[:, lo:lo + hd]
        outs.append(o / den)
    o_ref[0] = jnp.concatenate(outs, axis=1).astype(BF16)


def _fox_sample(q, k, v, cache_k, cache_v, c_cache, cn_row, *, layer):
    b, t, w = q.shape
    nh, hd, p_len = cache_k.shape[2:]
    tokq = pl.BlockSpec((1, t, w), lambda bi: (bi, 0, 0))
    cache = pl.BlockSpec((1, 1, nh, hd, p_len), lambda bi: (layer, bi, 0, 0, 0))
    return pl.pallas_call(
        _fox_sample_body,
        grid=(b,),
        in_specs=[tokq, tokq, tokq, cache, cache,
                  pl.BlockSpec((nh, p_len), lambda bi: (bi, 0)),
                  pl.BlockSpec((nh, LANES), lambda bi: (bi, 0))],
        out_specs=tokq,
        out_shape=jax.ShapeDtypeStruct((b, t, w), BF16),
        compiler_params=_cparams("parallel"),
        name="fox_sample",
    )(q, k, v, cache_k, cache_v, c_cache, cn_row)


def _hgrn_body(q_ref, lf_ref, k_ref, v_ref, s0_ref, o_ref, so_ref, st_ref, tmp_ref, *,
               c, n_chunks, sps, hps, zero_init):
    ti = pl.program_id(2)
    d = HGRN_HEAD_DIM
    chains = [(s, hh) for s in range(sps) for hh in range(hps)]
    slot = {ch: n for n, ch in enumerate(chains)}
    lanes_of = [slice(hh * d, (hh + 1) * d) for hh in range(hps)]

    @pl.when(ti == 0)
    def _():
        for s, hh in chains:
            st_ref[slot[(s, hh)]] = jnp.zeros((d, d), F32) if zero_init else s0_ref[0, s, hh].T

    row = lax.broadcasted_iota(jnp.int32, (c, c), 0)
    col = lax.broadcasted_iota(jnp.int32, (c, c), 1)
    rowc = lax.broadcasted_iota(jnp.int32, (c, 1), 0)
    half = c // 2
    b_all = [_scan(lf_ref[s], 0, period=c) for s in range(sps)]

    span = jnp.zeros((1, hps * d), F32)
    for s in range(sps):
        for ci in range(n_chunks):
            b = b_all[s][ci * c:(ci + 1) * c]
            b_mid = b[half - 1:half]
            span = jnp.maximum(span, jnp.maximum(b[0:1] - b_mid, b_mid - b[c - 1:c]))
    factorable = jnp.max(span) <= HGRN_FACTOR_LIMIT

    def carry_state(st, k, v, b):
        b_last = b[c - 1:c]
        kh = (k * jnp.exp(b_last - b)).astype(BF16)
        return st * jnp.exp(b_last) + _dot_tn(v, kh)

    @pl.when(factorable)
    def _():
        units = [(ch, ci) for ch in chains for ci in range(n_chunks)]
        tile_of = {u: (u[0][0], slice(u[1] * c, (u[1] + 1) * c), lanes_of[u[0][1]]) for u in units}
        sc, ds, qe, grow = {}, {}, {}, {}
        for u in units:
            s, rs, hs = tile_of[u]
            q = q_ref[s, rs, hs]
            k = k_ref[s, rs, hs]
            b = b_all[s][rs, hs]
            b_mid = b[half - 1:half]
            b_last = b[c - 1:c]
            qe[u] = (q * jnp.exp(b)).astype(BF16)
            grow[u] = jnp.exp(b_last)
            qt = (q * jnp.exp(b - b_mid)).astype(BF16)
            kt = (k * jnp.exp(b_mid - b)).astype(BF16)
            kh = (k * jnp.exp(b_last - b)).astype(BF16)
            sc[u] = _dot_nt(qt, kt)
            ds[u] = _dot_tn(v_ref[s, rs, hs], kh)
        st_in = {}
        for ch in chains:
            st = st_ref[slot[ch]]
            for ci in range(n_chunks):
                st_in[(ch, ci)] = st.astype(BF16)
                st = st * grow[(ch, ci)] + ds[(ch, ci)]
            st_ref[slot[ch]] = st
        for u in units:
            s, rs, hs = tile_of[u]
            intra = _dot(jnp.where(col <= row, sc[u], 0.0).astype(BF16), v_ref[s, rs, hs])
            o_ref[s, rs, hs] = _dot_nt(qe[u], st_in[u]) + intra

    @pl.when(jnp.logical_not(factorable))
    def _():
        for s, hh in chains:
            hs = lanes_of[hh]

            def chunk(ci, carry, s=s, hs=hs, sl=slot[(s, hh)]):
                rs = pl.ds(pl.multiple_of(ci * c, c), c)
                q = q_ref[s, rs, hs]
                k = k_ref[s, rs, hs]
                v = v_ref[s, rs, hs]
                b = _scan(lf_ref[s, rs, hs], 0)
                st = st_ref[sl]
                tmp_ref[0] = b
                tmp_ref[1] = k
                tmp_ref[2] = v.astype(F32)

                def key_row(r, acc):
                    bs = tmp_ref[0, pl.ds(r, 1), :]
                    ks = tmp_ref[1, pl.ds(r, 1), :]
                    vs = tmp_ref[2, pl.ds(r, 1), :]
                    w = jnp.exp(jnp.minimum(b - bs, 0.0))
                    a = jnp.sum(q * ks * w, axis=1, keepdims=True)
                    return acc + jnp.where(rowc >= r, a, 0.0) * vs

                o = _dot_nt((q * jnp.exp(b)).astype(BF16), st.astype(BF16))
                o_ref[s, rs, hs] = lax.fori_loop(0, c, key_row, o)
                st_ref[sl] = carry_state(st, k, v, b)
                return carry

            lax.fori_loop(0, n_chunks, chunk, 0)

    @pl.when(ti == pl.num_programs(2) - 1)
    def _():
        for s, hh in chains:
            so_ref[0, s, hh] = st_ref[slot[(s, hh)]].T


def _hgrn(hq, hlf, hk, hv, s0, carried, *, layer, c, tl, sps, hps):
    b, l, w = hq.shape
    d = HGRN_HEAD_DIM
    nh = w // d
    tok = pl.BlockSpec((sps, tl, hps * d), lambda bi, h, ti: (bi, ti, h))
    st = pl.BlockSpec((1, sps, hps, d, d), lambda bi, h, ti: (layer, bi, h, 0, 0))
    zero_init = s0 is None
    body = functools.partial(_hgrn_body, c=c, n_chunks=tl // c, sps=sps, hps=hps, zero_init=zero_init)
    args, in_specs = [hq, hlf, hk, hv], [tok, tok, tok, tok]
    if zero_init:
        core = body
        body = lambda q, lf, k, v, *rest: core(q, lf, k, v, None, *rest)
    else:
        args.append(s0)
        in_specs.append(st)
    aliases = {}
    if carried is not None:
        aliases[len(args)] = 1
        body = _drop_alias_refs(body, len(args), 1)
        args.append(carried)
        in_specs.append(_ANY)
    return pl.pallas_call(
        body,
        grid=(b // sps, nh // hps, l // tl),
        in_specs=in_specs,
        out_specs=[tok, st],
        out_shape=[jax.ShapeDtypeStruct((b, l, w), F32), jax.ShapeDtypeStruct((DEPTH, b, nh, d, d), F32)],
        scratch_shapes=[pltpu.VMEM((sps * hps, d, d), F32), pltpu.VMEM((3, c, d), F32)],
        input_output_aliases=aliases,
        compiler_params=_cparams("parallel", "parallel", "arbitrary"),
        name="hgrn",
    )(*args)


def kernel(x_prompt, x_sample, cache_k, cache_v, cache_logf, state_hgrn, norm_ffn1, ffn1_wi, ffn1_wo,
           norm_mix, w_in, b_fgate, hgrn_lb, hgrn_gnorm, w_out, norm_ffn2, ffn2_wi, ffn2_wo, norm_final):
    bp, lp, d = x_prompt.shape
    bs, ls, _ = x_sample.shape
    mp, ms = bp * lp, bs * ls
    p_len = cache_k.shape[3]
    f, w, h8 = FOX_WIDTH, HGRN_WIDTH, FOX_HEADS

    xp = x_prompt.reshape(mp, d)
    xs = x_sample.reshape(ms, d)
    gfin = norm_final.reshape(1, d)
    g1, g2, gm = (a.reshape(DEPTH, 1, d) for a in (norm_ffn1, norm_ffn2, norm_mix))
    gn = hgrn_gnorm.reshape(DEPTH, 1, HGRN_HEAD_DIM)
    clf = cache_logf.reshape(DEPTH, bs * h8, p_len)
    cache_kt = jnp.swapaxes(cache_k, 3, 4)
    cache_vt = jnp.swapaxes(cache_v, 3, 4)

    carry_p = carry_s = None
    st_p = st_s = None
    lfs = []
    for l in range(DEPTH):
        last = l == DEPTH - 1
        wl = w_in[l]
        wm = jnp.concatenate([wl[:, :3 * f], wl[:, 3 * f + h8:]], axis=1).astype(BF16)
        wf = jnp.pad(wl[:, 3 * f:3 * f + h8], ((0, 0), (0, LANES - h8))).astype(BF16)
        bfp = jnp.pad(b_fgate[l].reshape(1, h8), ((0, 0), (0, LANES - h8)))
        wi1, wo1 = _layer_bf16(ffn1_wi, layer=l, steps=4), _layer_bf16(ffn1_wo, layer=l, steps=2)
        wi2, wo2 = _layer_bf16(ffn2_wi, layer=l, steps=4), _layer_bf16(ffn2_wo, layer=l, steps=2)
        wout = _layer_bf16(w_out, layer=l, steps=2)

        xp = _ffn(xp, g1, wi1, wo1, gfin, layer=l, final_norm=False, tm=TM_FFN, tf=TF_FFN)
        xs = _ffn(xs, g1, wi1, wo1, gfin, layer=l, final_norm=False, tm=ms, tf=TF_FFN)

        (q, k, v, kh, vh, lfr, hq, hlf, hk, hv, hg) = _inproj(
            xp, gm, wm, wf, bfp, hgrn_lb, carry_p, layer=l, sample=False, batch=bp, tm=TM_PROJ)
        carry_p = (kh, vh, lfr)
        (q_s, k_s, v_s, kh_s, vh_s, lfr_s, hq_s, hlf_s, hk_s, hv_s, hg_s) = _inproj(
            xs, gm, wm, wf, bfp, hgrn_lb, carry_s, layer=l, sample=True, batch=bs, tm=ms)
        carry_s = (kh_s, vh_s)

        c_row = _fox_cumsum(lfr, layer=l)
        fo = _fox_prompt(q.reshape(bp, lp, f), k.reshape(bp, lp, f), v.reshape(bp, lp, f),
                         c_row.reshape(bp, h8, lp // T_ATTN, T_ATTN), t=T_ATTN, pairs=FOX_PAIRS)
        lfs_l = lfr_s.reshape(h8, bs, ls).transpose(1, 0, 2)
        lfs.append(lfs_l)
        lfs_pad = jnp.pad(lfs_l, ((0, 0), (0, 0), (0, LANES - ls))).reshape(bs * h8, LANES)
        c_cache, cn_row = _fox_sample_scan(clf, lfs_pad, layer=l)
        fo_s = _fox_sample(q_s.reshape(bs, ls, f), k_s.reshape(bs, ls, f), v_s.reshape(bs, ls, f),
                           cache_kt, cache_vt, c_cache, cn_row, layer=l)
        to3 = lambda a, b_, l_: a.reshape(b_, l_, w)
        ho, st_p = _hgrn(to3(hq, bp, lp), to3(hlf, bp, lp), to3(hk, bp, lp), to3(hv, bp, lp), None, st_p,
                         layer=l, c=C_HGRN, tl=TL_HGRN, sps=1, hps=HGRN_PROMPT_HEADS)
        ho_s, st_s = _hgrn(to3(hq_s, bs, ls), to3(hlf_s, bs, ls), to3(hk_s, bs, ls), to3(hv_s, bs, ls),
                           state_hgrn, st_s, layer=l, c=ls, tl=ls, sps=HGRN_SAMPLE_STREAMS, hps=HGRN_HEADS)

        xp = _ffn(xp, g2, wi2, wo2, gfin, (fo.reshape(mp, f), ho.reshape(mp, w), hg, gn, wout),
                  layer=l, final_norm=last, tm=TM_FFN, tf=TF_FFN)
        xs = _ffn(xs, g2, wi2, wo2, gfin, (fo_s.reshape(ms, f), ho_s.reshape(ms, w), hg_s, gn, wout),
                  layer=l, final_norm=last, tm=ms, tf=TF_FFN)

    k_prompt = jnp.swapaxes(carry_p[0], 3, 4)
    v_prompt = jnp.swapaxes(carry_p[1], 3, 4)
    return (xp.reshape(bp, lp, d), xs.reshape(bs, ls, d), k_prompt, v_prompt, carry_p[2], st_p,
            carry_s[0], carry_s[1], jnp.stack(lfs), st_s)
```

```python
import functools

import jax
import jax.numpy as jnp
from jax import lax
from jax.experimental import pallas as pl
from jax.experimental.pallas import tpu as pltpu

F32 = jnp.float32
BF16 = jnp.bfloat16

DEPTH = 2
FOX_HEADS = 8
FOX_HEAD_DIM = 64
FOX_WIDTH = FOX_HEADS * FOX_HEAD_DIM
HGRN_HEADS = 4
HGRN_HEAD_DIM = 128
HGRN_WIDTH = HGRN_HEADS * HGRN_HEAD_DIM
RMS_EPS = 1e-6
FOX_SCALE = FOX_HEAD_DIM ** -0.5

LANES = 128
VMEM_LIMIT_BYTES = 56 * 1024 * 1024
HGRN_FACTOR_LIMIT = 50.0
FOX_SKIP_GAP = 106.0
FOX_NORM_SLACK = 1.001
FOX_FIRST_ROUND_BLOCKS = 3

TM_FFN = 512
TF_FFN = 256
TM_PROJ = 512
T_ATTN = 256
FOX_PAIRS = 4
C_HGRN = 64
TL_HGRN = 1024
HGRN_PROMPT_HEADS = 2
HGRN_SAMPLE_STREAMS = 4
KEY_NORM_ROWS = 1024


def _cparams(*sem):
    return pltpu.CompilerParams(dimension_semantics=sem, vmem_limit_bytes=VMEM_LIMIT_BYTES)


def _rms(x, g):
    ms = jnp.mean(x * x, axis=-1, keepdims=True)
    return x * lax.rsqrt(ms + RMS_EPS) * g


def _silu(x):
    return x * jax.nn.sigmoid(x)


def _log1p_exp_neg(d):
    return jnp.log(1.0 + jnp.exp(-d))


def _log_sigmoid(z):
    return jnp.minimum(z, 0.0) - _log1p_exp_neg(jnp.abs(z))


def _gate_terms(z):
    e = jnp.exp(-jnp.abs(z))
    t = 1.0 + e
    return jnp.minimum(z, 0.0) - jnp.log(t), jnp.where(z >= 0.0, e, 1.0) / t


def _dot(a, b):
    return jnp.dot(a, b, preferred_element_type=F32)


def _dot_nt(a, b):
    return lax.dot_general(a, b, (((1,), (1,)), ((), ())), preferred_element_type=F32)


def _dot_tn(a, b):
    return lax.dot_general(a, b, (((0,), (0,)), ((), ())), preferred_element_type=F32)


def _scan(x, axis, period=None):
    n = x.shape[axis] if period is None else period
    idx = lax.broadcasted_iota(jnp.int32, x.shape, axis)
    if period is not None:
        idx = idx & (period - 1)
    s = 1
    while s < n:
        x = x + jnp.where(idx >= s, pltpu.roll(x, s, axis), 0.0)
        s *= 2
    return x


def _drop_alias_refs(body, n_in, n_alias):
    def wrapped(*refs):
        return body(*refs[:n_in], *refs[n_in + n_alias:])
    return wrapped


_ANY = pl.BlockSpec(memory_space=pl.ANY)


def _mix(fo_ref, ho_ref, hg_ref, gn_ref):
    gn = gn_ref[0]
    d = HGRN_HEAD_DIM
    parts = [fo_ref[...]]
    for h in range(HGRN_HEADS):
        ho = ho_ref[:, h * d:(h + 1) * d]
        parts.append((_rms(ho, gn) * _silu(hg_ref[:, h * d:(h + 1) * d])).astype(BF16))
    return jnp.concatenate(parts, axis=1)


def _ffn_body(*refs, final_norm, with_outproj, tf):
    if with_outproj:
        (x_ref, fo_ref, ho_ref, hg_ref, gn_ref, wout_ref,
         g_ref, wa_ref, wb_ref, wo_ref, gf_ref, o_ref, acc_ref) = refs
        x = x_ref[...] + _dot(_mix(fo_ref, ho_ref, hg_ref, gn_ref), wout_ref[...])
    else:
        x_ref, g_ref, wa_ref, wb_ref, wo_ref, gf_ref, o_ref, acc_ref = refs
        x = x_ref[...]
    f = wo_ref.shape[0]
    xn = _rms(x, g_ref[0]).astype(BF16)
    for c in range(f // tf):
        a = _dot(xn, wa_ref[:, c * tf:(c + 1) * tf])
        b = _dot(xn, wb_ref[:, c * tf:(c + 1) * tf])
        part = _dot((_silu(a) * b).astype(BF16), wo_ref[c * tf:(c + 1) * tf, :])
        if c == 0:
            acc_ref[...] = part
        else:
            acc_ref[...] += part
    y = x + 0.5 * acc_ref[...]
    if final_norm:
        y = _rms(y, gf_ref[...])
    o_ref[...] = y


def _ffn(x, g, wa, wb, wo, gf, outproj=None, *, layer, final_norm, tm, tf):
    m, d = x.shape
    tok = lambda width: pl.BlockSpec((tm, width), lambda i: (i, 0))
    held = lambda a: pl.BlockSpec(a.shape, lambda i: (0,) * a.ndim, pipeline_mode=pl.Buffered(1))
    args, in_specs = [x], [tok(d)]
    if outproj is not None:
        fo, ho, hg, gn, wout = outproj
        args += [fo, ho, hg, gn, wout]
        in_specs += [tok(fo.shape[1]), tok(ho.shape[1]), tok(hg.shape[1]),
                     pl.BlockSpec((1, 1, gn.shape[2]), lambda i: (layer, 0, 0)), held(wout)]
    args += [g, wa, wb, wo, gf]
    in_specs += [pl.BlockSpec((1, 1, d), lambda i: (layer, 0, 0)), held(wa), held(wb), held(wo),
                 pl.BlockSpec((1, d), lambda i: (0, 0))]
    return pl.pallas_call(
        functools.partial(_ffn_body, final_norm=final_norm, with_outproj=outproj is not None, tf=tf),
        grid=(m // tm,),
        in_specs=in_specs,
        out_specs=tok(d),
        out_shape=jax.ShapeDtypeStruct((m, d), F32),
        scratch_shapes=[pltpu.VMEM((tm, d), F32)],
        compiler_params=_cparams("parallel"),
        name="ffn",
    )(*args)


def _ffn_stream_body(*refs, final_norm, with_outproj):
    if with_outproj:
        (x_ref, fo_ref, ho_ref, hg_ref, gn_ref, wout_ref, g_ref, wa_ref, wb_ref, wo_ref, gf_ref,
         o_ref, wab_ref, wbb_ref, wob_ref, woutb_ref, xr_ref, xn_ref, acc_ref) = refs
    else:
        (x_ref, g_ref, wa_ref, wb_ref, wo_ref, gf_ref,
         o_ref, wab_ref, wbb_ref, wob_ref, xr_ref, xn_ref, acc_ref) = refs
    j = pl.program_id(0)

    @pl.when(j == 0)
    def _():
        x = x_ref[...]
        if with_outproj:
            woutb_ref[...] = wout_ref[0].astype(BF16)
            x = x + _dot(_mix(fo_ref, ho_ref, hg_ref, gn_ref), woutb_ref[...])
        xr_ref[...] = x
        xn_ref[...] = _rms(x, g_ref[0]).astype(BF16)
        acc_ref[...] = jnp.zeros_like(acc_ref)

    wab_ref[...] = wa_ref[0].astype(BF16)
    wbb_ref[...] = wb_ref[0].astype(BF16)
    wob_ref[...] = wo_ref[0].astype(BF16)
    xn = xn_ref[...]
    h = _silu(_dot(xn, wab_ref[...])) * _dot(xn, wbb_ref[...])
    acc_ref[...] += _dot(h.astype(BF16), wob_ref[...])

    @pl.when(j == pl.num_programs(0) - 1)
    def _():
        y = xr_ref[...] + 0.5 * acc_ref[...]
        if final_norm:
            y = _rms(y, gf_ref[...])
        o_ref[...] = y


def _ffn_stream(x, g, wi, wo, gf, outproj=None, *, layer, final_norm, tf):
    m, d = x.shape
    f = wo.shape[1]
    nf = f // tf
    whole = lambda rows, width: pl.BlockSpec((rows, width), lambda j: (0, 0))
    args, in_specs = [x], [whole(m, d)]
    out_specs = [whole(m, d), pl.BlockSpec((d, tf), lambda j: (0, j)), pl.BlockSpec((d, tf), lambda j: (0, j)),
                 pl.BlockSpec((tf, d), lambda j: (j, 0))]
    out_shape = [jax.ShapeDtypeStruct((m, d), F32), jax.ShapeDtypeStruct((d, f), BF16),
                 jax.ShapeDtypeStruct((d, f), BF16), jax.ShapeDtypeStruct((f, d), BF16)]
    if outproj is not None:
        fo, ho, hg, gn, wout = outproj
        dm = wout.shape[1]
        args += [fo, ho, hg, gn, wout]
        in_specs += [whole(m, fo.shape[1]), whole(m, ho.shape[1]), whole(m, hg.shape[1]),
                     pl.BlockSpec((1, 1, gn.shape[2]), lambda j: (layer, 0, 0)),
                     pl.BlockSpec((1, dm, d), lambda j: (layer, 0, 0))]
        out_specs.append(whole(dm, d))
        out_shape.append(jax.ShapeDtypeStruct((dm, d), BF16))
    args += [g, wi, wi, wo, gf]
    in_specs += [pl.BlockSpec((1, 1, d), lambda j: (layer, 0, 0)),
                 pl.BlockSpec((1, d, tf), lambda j: (layer, 0, j)),
                 pl.BlockSpec((1, d, tf), lambda j: (layer, 0, j + nf)),
                 pl.BlockSpec((1, tf, d), lambda j: (layer, j, 0)),
                 whole(1, d)]
    return pl.pallas_call(
        functools.partial(_ffn_stream_body, final_norm=final_norm, with_outproj=outproj is not None),
        grid=(nf,),
        in_specs=in_specs,
        out_specs=out_specs,
        out_shape=out_shape,
        scratch_shapes=[pltpu.VMEM((m, d), F32), pltpu.VMEM((m, d), BF16), pltpu.VMEM((m, d), F32)],
        compiler_params=_cparams("arbitrary"),
        name="ffn_stream",
    )(*args)


def _inproj_body(x_ref, g_ref, wm_ref, wf_ref, bf_ref, lbp_ref,
                 q_ref, k_ref, v_ref, kh_ref, vh_ref, lfr_ref,
                 hq_ref, hlf_ref, hk_ref, hv_ref, hg_ref, *, layer, sample):
    h = _rms(x_ref[...], g_ref[0]).astype(BF16)
    w, hd = FOX_WIDTH, FOX_HEAD_DIM
    wc = 2 * LANES
    heads_per_chunk = wc // hd

    def chunks(i):
        for n in range(w // wc):
            yield n, slice(n * wc, (n + 1) * wc), _dot(h, wm_ref[:, i * w + n * wc:i * w + (n + 1) * wc])

    def store_heads(ref, n, val):
        h0 = n * heads_per_chunk
        if sample:
            for hh in range(heads_per_chunk):
                piece = val[:, hh * hd:(hh + 1) * hd]
                ref[0, :, h0 + hh] = piece.reshape(ref.shape[1], ref.shape[3], hd)
        else:
            ref[0, 0, h0:h0 + heads_per_chunk] = val.T.reshape(heads_per_chunk, hd, val.shape[0])

    p = lbp_ref[...]
    e = jnp.exp(p - jnp.max(p, axis=0, keepdims=True))
    sm = e / jnp.sum(e, axis=0, keepdims=True)
    cs = sm[0:1]
    for r in range(1, layer + 1):
        cs = cs + sm[r:r + 1]
    lb = cs - sm[0:1]
    log_lb = jnp.log(lb)
    log_1m_lb = jnp.log1p(-lb)

    for n, cols, z in chunks(4):
        log_sig, sig_neg = _gate_terms(z)
        a = log_lb[:, cols]
        b = log_1m_lb[:, cols] + log_sig
        hlf_ref[:, cols] = jnp.maximum(a, b) + _log1p_exp_neg(jnp.abs(a - b))
        hk_ref[:, cols] = (1.0 - lb[:, cols]) * sig_neg
    for n, cols, hq in chunks(3):
        hq_ref[:, cols] = _silu(hq)

    lf = _log_sigmoid(_dot(h, wf_ref[...]) + bf_ref[...])
    lfr = lf.T[:FOX_HEADS, :]
    if sample:
        lfr_ref[...] = lfr
    else:
        lfr_ref[0, 0] = lfr

    for n, cols, q in chunks(0):
        q_ref[:, cols] = (q * FOX_SCALE).astype(BF16)
    for n, cols, k in chunks(1):
        store_heads(kh_ref, n, k)
        k_ref[:, cols] = k.astype(BF16)
    for n, cols, v in chunks(2):
        store_heads(vh_ref, n, v)
        v_ref[:, cols] = v.astype(BF16)
    for n, cols, hv in chunks(5):
        hv_ref[:, cols] = hv.astype(BF16)
    for n, cols, hg in chunks(6):
        hg_ref[:, cols] = hg


def _inproj(x, g, wm, wf, bfp, lbp, carried, *, layer, sample, batch, tm):
    m, d = x.shape
    w, hd, nh = FOX_WIDTH, FOX_HEAD_DIM, FOX_HEADS
    seq = m // batch
    tiles_per_seq = seq // tm if not sample else 1
    tok = lambda width: pl.BlockSpec((tm, width), lambda i: (i, 0))
    full = lambda a: pl.BlockSpec(a.shape, lambda i: (0,) * a.ndim)
    sds = jax.ShapeDtypeStruct
    if sample:
        heads_shape = (DEPTH, batch, nh, seq, hd)
        heads_spec = pl.BlockSpec((1, batch, nh, seq, hd), lambda i: (layer, 0, 0, 0, 0))
        lfr_shape, lfr_spec = (nh, m), pl.BlockSpec((nh, m), lambda i: (0, 0))
    else:
        heads_shape = (DEPTH, batch, nh, hd, seq)
        heads_spec = pl.BlockSpec((1, 1, nh, hd, tm),
                                  lambda i: (layer, i // tiles_per_seq, 0, 0, i % tiles_per_seq))
        lfr_shape = (DEPTH, batch, nh, seq)
        lfr_spec = pl.BlockSpec((1, 1, nh, tm), lambda i: (layer, i // tiles_per_seq, 0, i % tiles_per_seq))
    out_specs = [tok(w), tok(w), tok(w), heads_spec, heads_spec, lfr_spec,
                 tok(w), tok(w), tok(w), tok(w), tok(w)]
    out_shape = [sds((m, w), BF16), sds((m, w), BF16), sds((m, w), BF16),
                 sds(heads_shape, F32), sds(heads_shape, F32), sds(lfr_shape, F32),
                 sds((m, w), F32), sds((m, w), F32), sds((m, w), F32), sds((m, w), BF16), sds((m, w), F32)]
    body = functools.partial(_inproj_body, layer=layer, sample=sample)
    args = [x, g, wm, wf, bfp, lbp]
    in_specs = [tok(d), pl.BlockSpec((1, 1, d), lambda i: (layer, 0, 0)), full(wm), full(wf), full(bfp), full(lbp)]
    aliases = {}
    if carried is not None:
        out_index = (3, 4, 5)
        for n, buf in enumerate(carried):
            aliases[len(args)] = out_index[n]
            args.append(buf)
            in_specs.append(_ANY)
        body = _drop_alias_refs(body, 6, len(carried))
    return pl.pallas_call(
        body, grid=(m // tm,), in_specs=in_specs, out_specs=out_specs, out_shape=out_shape,
        input_output_aliases=aliases, compiler_params=_cparams("arbitrary"), name="inproj",
    )(*args)


def _cumsum_body(lfr_ref, cr_ref):
    cr_ref[0] = _scan(lfr_ref[0, 0], 1)


def _fox_cumsum(lf_rows, *, layer):
    _, b, h, l = lf_rows.shape
    return pl.pallas_call(
        _cumsum_body,
        grid=(b,),
        in_specs=[pl.BlockSpec((1, 1, h, l), lambda i: (layer, i, 0, 0))],
        out_specs=pl.BlockSpec((1, h, l), lambda i: (i, 0, 0)),
        out_shape=jax.ShapeDtypeStruct((b, h, l), F32),
        compiler_params=_cparams("parallel"),
        name="fox_cumsum",
    )(lf_rows)


def _fox_prompt_body(q_ref, k_ref, v_ref, cr_ref, o_ref, m_ref, l_ref, acc_ref, kn_ref, trips_ref, *, t, pairs):
    i = pl.program_id(2)
    nk = cr_ref.shape[2]
    hd = FOX_HEAD_DIM
    lane = lax.broadcasted_iota(jnp.int32, (t, LANES), 1)
    row = lax.broadcasted_iota(jnp.int32, (t, t), 0)
    col = lax.broadcasted_iota(jnp.int32, (t, t), 1)
    head_lanes = [lane < hd, lane >= hd]
    lanes_of = [slice(p * LANES, (p + 1) * LANES) for p in range(pairs)]

    @pl.when(i == 0)
    def _():
        step = KEY_NORM_ROWS
        lane_s = lax.broadcasted_iota(jnp.int32, (step, LANES), 1)
        for p in range(pairs):
            def chunk(ci, mx, p=p):
                kf = k_ref[0, pl.ds(pl.multiple_of(ci * step, step), step), lanes_of[p]].astype(F32)
                k2 = kf * kf
                n0 = jnp.max(jnp.sum(jnp.where(lane_s < hd, k2, 0.0), axis=1, keepdims=True))
                n1 = jnp.max(jnp.sum(jnp.where(lane_s >= hd, k2, 0.0), axis=1, keepdims=True))
                return jnp.maximum(mx[0], n0), jnp.maximum(mx[1], n1)

            n0, n1 = lax.fori_loop(0, k_ref.shape[1] // step, chunk, (jnp.float32(0.0), jnp.float32(0.0)))
            kn_ref[2 * p] = jnp.sqrt(n0)
            kn_ref[2 * p + 1] = jnp.sqrt(n1)

    causal = jnp.concatenate([col <= row, col <= row], axis=0)
    qs, qn = [], []
    for p in range(pairs):
        qq = q_ref[0, :, lanes_of[p]]
        zero = jnp.zeros_like(qq)
        qs.append(jnp.concatenate([jnp.where(head_lanes[h], qq, zero) for h in range(2)], axis=0))
        q2 = qq.astype(F32) * qq.astype(F32)
        qn.append(jnp.concatenate(
            [jnp.sqrt(jnp.sum(jnp.where(head_lanes[h], q2, 0.0), axis=1, keepdims=True)) for h in range(2)],
            axis=0))

    def scores(p, j, diag):
        kk = k_ref[0, pl.ds(pl.multiple_of(j * t, t), t), lanes_of[p]]
        s = _dot_nt(qs[p], kk)
        s = jnp.concatenate(
            [s[h * t:(h + 1) * t] - cr_ref[0, 2 * p + h, pl.ds(j, 1), :] for h in range(2)], axis=0)
        return jnp.where(causal, s, -jnp.inf) if diag else s

    def values(p, j):
        return v_ref[0, pl.ds(pl.multiple_of(j * t, t), t), lanes_of[p]]

    def write_out(p, acc, l):
        o = acc / l
        o_ref[0, :, lanes_of[p]] = jnp.where(head_lanes[0], o[:t], o[t:]).astype(BF16)

    def first_round(js, j_next):
        state = []
        for p in range(pairs):
            m = l = acc = None
            for n, j in enumerate(js):
                s = scores(p, j, n == len(js) - 1)
                m_blk = jnp.max(s, axis=1, keepdims=True)
                if m is None:
                    m_next = m_blk
                    pr = jnp.exp(s - m_next)
                    l = jnp.sum(pr, axis=1, keepdims=True)
                    acc = _dot(pr.astype(BF16), values(p, j))
                else:
                    m_next = jnp.maximum(m, m_blk)
                    alpha = jnp.exp(m - m_next)
                    pr = jnp.exp(s - m_next)
                    l = alpha * l + jnp.sum(pr, axis=1, keepdims=True)
                    acc = alpha * acc + _dot(pr.astype(BF16), values(p, j))
                m = m_next
            state.append((m, l, acc))

        jidx = lax.broadcasted_iota(jnp.int32, (nk, 1), 0)
        first_needed = jnp.full((nk, 1), nk, jnp.int32)
        for p in range(pairs):
            m = state[p][0]
            for h in range(2):
                rows = slice(h * t, (h + 1) * t)
                gap = qn[p][rows] * (kn_ref[2 * p + h] * FOX_NORM_SLACK) - m[rows]
                reach = jnp.max(gap, axis=0, keepdims=True)
                c_min = jnp.min(cr_ref[0, 2 * p + h], axis=1, keepdims=True)
                first_needed = jnp.minimum(first_needed, jnp.where(reach - c_min >= -FOX_SKIP_GAP, jidx, nk))
        trips = j_next + 1 - jnp.minimum(jnp.min(first_needed), j_next + 1)
        trips_ref[0] = trips

        @pl.when(trips == 0)
        def _():
            for p in range(pairs):
                write_out(p, state[p][2], state[p][1])

        @pl.when(trips > 0)
        def _():
            for p in range(pairs):
                m, l, acc = state[p]
                m_ref[p] = jnp.broadcast_to(m, m_ref.shape[1:])
                l_ref[p] = jnp.broadcast_to(l, l_ref.shape[1:])
                acc_ref[p] = acc

    def block(j):
        for p in range(pairs):
            s = scores(p, j, False)
            m_prev = m_ref[p]
            m_next = jnp.maximum(m_prev, jnp.max(s, axis=1, keepdims=True))
            pr = jnp.exp(s - jnp.concatenate([m_next] * (t // LANES), axis=1))
            alpha = jnp.exp(m_prev - m_next)
            l_ref[p] = alpha * l_ref[p] + jnp.sum(pr, axis=1, keepdims=True)
            m_ref[p] = m_next
            acc_ref[p] = alpha * acc_ref[p] + _dot(pr.astype(BF16), values(p, j))

    window = FOX_FIRST_ROUND_BLOCKS

    @pl.when(i >= window - 1)
    def _():
        first_round([i - (window - 1) + n for n in range(window)], i - window)

    @pl.when(i < window - 1)
    def _():
        first_round([i], i - 1)

    trips = trips_ref[0]

    @pl.when(trips > 0)
    def _():
        j_next = jnp.where(i >= window - 1, i - window, i - 1)

        def earlier(n, carry):
            block(j_next - n)
            return carry

        lax.fori_loop(0, trips, earlier, 0)
        for p in range(pairs):
            write_out(p, acc_ref[p], l_ref[p])


def _fox_prompt(q, k, v, c_rows, *, t, pairs):
    b, l, w = q.shape
    ngrp = w // (pairs * LANES)
    nk = l // t
    wide = pairs * LANES
    return pl.pallas_call(
        functools.partial(_fox_prompt_body, t=t, pairs=pairs),
        grid=(b, ngrp, l // t),
        in_specs=[
            pl.BlockSpec((1, t, wide), lambda bi, g, i: (bi, i, g)),
            pl.BlockSpec((1, l, wide), lambda bi, g, i: (bi, 0, g), pipeline_mode=pl.Buffered(1)),
            pl.BlockSpec((1, l, wide), lambda bi, g, i: (bi, 0, g), pipeline_mode=pl.Buffered(1)),
            pl.BlockSpec((1, 2 * pairs, nk, t), lambda bi, g, i: (bi, g, 0, 0)),
        ],
        out_specs=pl.BlockSpec((1, t, wide), lambda bi, g, i: (bi, i, g)),
        out_shape=jax.ShapeDtypeStruct((b, l, w), BF16),
        scratch_shapes=[pltpu.VMEM((pairs, 2 * t, LANES), F32), pltpu.VMEM((pairs, 2 * t, LANES), F32),
                        pltpu.VMEM((pairs, 2 * t, LANES), F32), pltpu.SMEM((2 * pairs,), F32),
                        pltpu.SMEM((1,), jnp.int32)],
        compiler_params=_cparams("parallel", "parallel", "arbitrary"),
        name="fox_prompt",
    )(q, k, v, c_rows)


def _sample_scan_body(clf_ref, lfr_ref, cc_ref, cnr_ref):
    cc_ref[...] = _scan(clf_ref[0], 1)
    cnr_ref[...] = _scan(lfr_ref[...], 1)


def _fox_sample_scan(clf, lf_row, *, layer):
    _, r, p_len = clf.shape
    full = lambda a: pl.BlockSpec(a.shape, lambda i: (0,) * a.ndim)
    return pl.pallas_call(
        _sample_scan_body,
        grid=(1,),
        in_specs=[pl.BlockSpec((1, r, p_len), lambda i: (layer, 0, 0)), full(lf_row)],
        out_specs=[pl.BlockSpec((r, p_len), lambda i: (0, 0)), full(lf_row)],
        out_shape=[jax.ShapeDtypeStruct((r, p_len), F32), jax.ShapeDtypeStruct(lf_row.shape, F32)],
        compiler_params=_cparams("arbitrary"),
        name="fox_sample_scan",
    )(clf, lf_row)


def _fox_sample_body(q_ref, k_ref, v_ref, ck_ref, cv_ref, cc_ref, cnr_ref, o_ref):
    t = q_ref.shape[1]
    nh, p_len = ck_ref.shape[2], ck_ref.shape[4]
    hd = FOX_HEAD_DIM
    lane = lax.broadcasted_iota(jnp.int32, (t, LANES), 1)
    row = lax.broadcasted_iota(jnp.int32, (t, t), 0)
    col = lax.broadcasted_iota(jnp.int32, (t, t), 1)
    c_cache = cc_ref[...]
    cn_row = cnr_ref[...]
    outs = []
    for h in range(nh):
        grp = slice((h // 2) * LANES, (h // 2 + 1) * LANES)
        lo = (h % 2) * hd
        qq = q_ref[0, :, grp]
        qm = jnp.where((lane >= lo) & (lane < lo + hd), qq, jnp.zeros_like(qq))
        q_h = qq[:, lo:lo + hd]
        kc_t = ck_ref[0, 0, h].astype(BF16)
        vc_t = cv_ref[0, 0, h].astype(BF16)
        c_h = c_cache[h:h + 1, :]
        tot = c_h[:, p_len - 1:p_len]
        s_c = _dot(q_h, kc_t) + (tot - c_h)
        s_s = _dot_nt(qm, k_ref[0, :, grp]) - cn_row[h:h + 1, :t]
        s_s = jnp.where(col <= row, s_s, -jnp.inf)
        m = jnp.maximum(jnp.max(s_c, axis=1, keepdims=True), jnp.max(s_s, axis=1, keepdims=True))
        p_c = jnp.exp(s_c - m)
        p_s = jnp.exp(s_s - m)
        den = jnp.sum(p_c, axis=1, keepdims=True) + jnp.sum(p_s, axis=1, keepdims=True)
        o = _dot_nt(p_c.astype(BF16), vc_t) + _dot(p_s.astype(BF16), v_ref[0, :, grp])[:, lo:lo + hd]
        outs.append(o / den)
    o_ref[0] = jnp.concatenate(outs, axis=1).astype(BF16)


def _fox_sample(q, k, v, cache_k, cache_v, c_cache, cn_row, *, layer):
    b, t, w = q.shape
    nh, hd, p_len = cache_k.shape[2:]
    tokq = pl.BlockSpec((1, t, w), lambda bi: (bi, 0, 0))
    cache = pl.BlockSpec((1, 1, nh, hd, p_len), lambda bi: (layer, bi, 0, 0, 0))
    return pl.pallas_call(
        _fox_sample_body,
        grid=(b,),
        in_specs=[tokq, tokq, tokq, cache, cache,
                  pl.BlockSpec((nh, p_len), lambda bi: (bi, 0)),
                  pl.BlockSpec((nh, LANES), lambda bi: (bi, 0))],
        out_specs=tokq,
        out_shape=jax.ShapeDtypeStruct((b, t, w), BF16),
        compiler_params=_cparams("parallel"),
        name="fox_sample",
    )(q, k, v, cache_k, cache_v, c_cache, cn_row)


def _hgrn_body(q_ref, lf_ref, k_ref, v_ref, s0_ref, o_ref, so_ref, st_ref, tmp_ref, *,
               c, n_chunks, sps, hps, zero_init):
    ti = pl.program_id(2)
    d = HGRN_HEAD_DIM
    chains = [(s, hh) for s in range(sps) for hh in range(hps)]
    slot = {ch: n for n, ch in enumerate(chains)}
    lanes_of = [slice(hh * d, (hh + 1) * d) for hh in range(hps)]

    @pl.when(ti == 0)
    def _():
        for s, hh in chains:
            st_ref[slot[(s, hh)]] = jnp.zeros((d, d), F32) if zero_init else s0_ref[0, s, hh].T

    row = lax.broadcasted_iota(jnp.int32, (c, c), 0)
    col = lax.broadcasted_iota(jnp.int32, (c, c), 1)
    rowc = lax.broadcasted_iota(jnp.int32, (c, 1), 0)
    half = c // 2
    b_all = [_scan(lf_ref[s], 0, period=c) for s in range(sps)]

    span = jnp.zeros((1, hps * d), F32)
    for s in range(sps):
        for ci in range(n_chunks):
            b = b_all[s][ci * c:(ci + 1) * c]
            b_mid = b[half - 1:half]
            span = jnp.maximum(span, jnp.maximum(b[0:1] - b_mid, b_mid - b[c - 1:c]))
    factorable = jnp.max(span) <= HGRN_FACTOR_LIMIT

    def carry_state(st, k, v, b):
        b_last = b[c - 1:c]
        kh = (k * jnp.exp(b_last - b)).astype(BF16)
        return st * jnp.exp(b_last) + _dot_tn(v, kh)

    @pl.when(factorable)
    def _():
        units = [(ch, ci) for ch in chains for ci in range(n_chunks)]
        tile_of = {u: (u[0][0], slice(u[1] * c, (u[1] + 1) * c), lanes_of[u[0][1]]) for u in units}
        sc, ds, qe, grow = {}, {}, {}, {}
        for u in units:
            s, rs, hs = tile_of[u]
            q = q_ref[s, rs, hs]
            k = k_ref[s, rs, hs]
            b = b_all[s][rs, hs]
            b_mid = b[half - 1:half]
            b_last = b[c - 1:c]
            qe[u] = (q * jnp.exp(b)).astype(BF16)
            grow[u] = jnp.exp(b_last)
            qt = (q * jnp.exp(b - b_mid)).astype(BF16)
            kt = (k * jnp.exp(b_mid - b)).astype(BF16)
            kh = (k * jnp.exp(b_last - b)).astype(BF16)
            sc[u] = _dot_nt(qt, kt)
            ds[u] = _dot_tn(v_ref[s, rs, hs], kh)
        st_in = {}
        for ch in chains:
            st = st_ref[slot[ch]]
            for ci in range(n_chunks):
                st_in[(ch, ci)] = st.astype(BF16)
                st = st * grow[(ch, ci)] + ds[(ch, ci)]
            st_ref[slot[ch]] = st
        for u in units:
            s, rs, hs = tile_of[u]
            intra = _dot(jnp.where(col <= row, sc[u], 0.0).astype(BF16), v_ref[s, rs, hs])
            o_ref[s, rs, hs] = _dot_nt(qe[u], st_in[u]) + intra

    @pl.when(jnp.logical_not(factorable))
    def _():
        for s, hh in chains:
            hs = lanes_of[hh]

            def chunk(ci, carry, s=s, hs=hs, sl=slot[(s, hh)]):
                rs = pl.ds(pl.multiple_of(ci * c, c), c)
                q = q_ref[s, rs, hs]
                k = k_ref[s, rs, hs]
                v = v_ref[s, rs, hs]
                b = _scan(lf_ref[s, rs, hs], 0)
                st = st_ref[sl]
                tmp_ref[0] = b
                tmp_ref[1] = k
                tmp_ref[2] = v.astype(F32)

                def key_row(r, acc):
                    bs = tmp_ref[0, pl.ds(r, 1), :]
                    ks = tmp_ref[1, pl.ds(r, 1), :]
                    vs = tmp_ref[2, pl.ds(r, 1), :]
                    w = jnp.exp(jnp.minimum(b - bs, 0.0))
                    a = jnp.sum(q * ks * w, axis=1, keepdims=True)
                    return acc + jnp.where(rowc >= r, a, 0.0) * vs

                o = _dot_nt((q * jnp.exp(b)).astype(BF16), st.astype(BF16))
                o_ref[s, rs, hs] = lax.fori_loop(0, c, key_row, o)
                st_ref[sl] = carry_state(st, k, v, b)
                return carry

            lax.fori_loop(0, n_chunks, chunk, 0)

    @pl.when(ti == pl.num_programs(2) - 1)
    def _():
        for s, hh in chains:
            so_ref[0, s, hh] = st_ref[slot[(s, hh)]].T


def _hgrn(hq, hlf, hk, hv, s0, carried, *, layer, c, tl, sps, hps):
    b, l, w = hq.shape
    d = HGRN_HEAD_DIM
    nh = w // d
    tok = pl.BlockSpec((sps, tl, hps * d), lambda bi, h, ti: (bi, ti, h))
    st = pl.BlockSpec((1, sps, hps, d, d), lambda bi, h, ti: (layer, bi, h, 0, 0))
    zero_init = s0 is None
    body = functools.partial(_hgrn_body, c=c, n_chunks=tl // c, sps=sps, hps=hps, zero_init=zero_init)
    args, in_specs = [hq, hlf, hk, hv], [tok, tok, tok, tok]
    if zero_init:
        core = body
        body = lambda q, lf, k, v, *rest: core(q, lf, k, v, None, *rest)
    else:
        args.append(s0)
        in_specs.append(st)
    aliases = {}
    if carried is not None:
        aliases[len(args)] = 1
        body = _drop_alias_refs(body, len(args), 1)
        args.append(carried)
        in_specs.append(_ANY)
    return pl.pallas_call(
        body,
        grid=(b // sps, nh // hps, l // tl),
        in_specs=in_specs,
        out_specs=[tok, st],
        out_shape=[jax.ShapeDtypeStruct((b, l, w), F32), jax.ShapeDtypeStruct((DEPTH, b, nh, d, d), F32)],
        scratch_shapes=[pltpu.VMEM((sps * hps, d, d), F32), pltpu.VMEM((3, c, d), F32)],
        input_output_aliases=aliases,
        compiler_params=_cparams("parallel", "parallel", "arbitrary"),
        name="hgrn",
    )(*args)


def kernel(x_prompt, x_sample, cache_k, cache_v, cache_logf, state_hgrn, norm_ffn1, ffn1_wi, ffn1_wo,
           norm_mix, w_in, b_fgate, hgrn_lb, hgrn_gnorm, w_out, norm_ffn2, ffn2_wi, ffn2_wo, norm_final):
    bp, lp, d = x_prompt.shape
    bs, ls, _ = x_sample.shape
    mp, ms = bp * lp, bs * ls
    p_len = cache_k.shape[3]
    f, w, h8 = FOX_WIDTH, HGRN_WIDTH, FOX_HEADS

    xp = x_prompt.reshape(mp, d)
    xs = x_sample.reshape(ms, d)
    gfin = norm_final.reshape(1, d)
    g1, g2, gm = (a.reshape(DEPTH, 1, d) for a in (norm_ffn1, norm_ffn2, norm_mix))
    gn = hgrn_gnorm.reshape(DEPTH, 1, HGRN_HEAD_DIM)
    clf = cache_logf.reshape(DEPTH, bs * h8, p_len)
    cache_kt = jnp.swapaxes(cache_k, 3, 4)
    cache_vt = jnp.swapaxes(cache_v, 3, 4)

    carry_p = carry_s = None
    st_p = st_s = None
    lfs = []
    for l in range(DEPTH):
        last = l == DEPTH - 1
        wl = w_in[l]
        wm = jnp.concatenate([wl[:, :3 * f], wl[:, 3 * f + h8:]], axis=1).astype(BF16)
        wf = jnp.pad(wl[:, 3 * f:3 * f + h8], ((0, 0), (0, LANES - h8))).astype(BF16)
        bfp = jnp.pad(b_fgate[l].reshape(1, h8), ((0, 0), (0, LANES - h8)))

        xs, wa, wb, wo = _ffn_stream(xs, g1, ffn1_wi, ffn1_wo, gfin, layer=l, final_norm=False, tf=TF_FFN)
        xp = _ffn(xp, g1, wa, wb, wo, gfin, layer=l, final_norm=False, tm=TM_FFN, tf=TF_FFN)

        (q, k, v, kh, vh, lfr, hq, hlf, hk, hv, hg) = _inproj(
            xp, gm, wm, wf, bfp, hgrn_lb, carry_p, layer=l, sample=False, batch=bp, tm=TM_PROJ)
        carry_p = (kh, vh, lfr)
        (q_s, k_s, v_s, kh_s, vh_s, lfr_s, hq_s, hlf_s, hk_s, hv_s, hg_s) = _inproj(
            xs, gm, wm, wf, bfp, hgrn_lb, carry_s, layer=l, sample=True, batch=bs, tm=ms)
        carry_s = (kh_s, vh_s)

        c_row = _fox_cumsum(lfr, layer=l)
        fo = _fox_prompt(q.reshape(bp, lp, f), k.reshape(bp, lp, f), v.reshape(bp, lp, f),
                         c_row.reshape(bp, h8, lp // T_ATTN, T_ATTN), t=T_ATTN, pairs=FOX_PAIRS)
        lfs_l = lfr_s.reshape(h8, bs, ls).transpose(1, 0, 2)
        lfs.append(lfs_l)
        lfs_pad = jnp.pad(lfs_l, ((0, 0), (0, 0), (0, LANES - ls))).reshape(bs * h8, LANES)
        c_cache, cn_row = _fox_sample_scan(clf, lfs_pad, layer=l)
        fo_s = _fox_sample(q_s.reshape(bs, ls, f), k_s.reshape(bs, ls, f), v_s.reshape(bs, ls, f),
                           cache_kt, cache_vt, c_cache, cn_row, layer=l)
        to3 = lambda a, b_, l_: a.reshape(b_, l_, w)
        ho, st_p = _hgrn(to3(hq, bp, lp), to3(hlf, bp, lp), to3(hk, bp, lp), to3(hv, bp, lp), None, st_p,
                         layer=l, c=C_HGRN, tl=TL_HGRN, sps=1, hps=HGRN_PROMPT_HEADS)
        ho_s, st_s = _hgrn(to3(hq_s, bs, ls), to3(hlf_s, bs, ls), to3(hk_s, bs, ls), to3(hv_s, bs, ls),
                           state_hgrn, st_s, layer=l, c=ls, tl=ls, sps=HGRN_SAMPLE_STREAMS, hps=HGRN_HEADS)

        xs, wa, wb, wo, wout = _ffn_stream(
            xs, g2, ffn2_wi, ffn2_wo, gfin, (fo_s.reshape(ms, f), ho_s.reshape(ms, w), hg_s, gn, w_out),
            layer=l, final_norm=last, tf=TF_FFN)
        xp = _ffn(xp, g2, wa, wb, wo, gfin, (fo.reshape(mp, f), ho.reshape(mp, w), hg, gn, wout),
                  layer=l, final_norm=last, tm=TM_FFN, tf=TF_FFN)

    k_prompt = jnp.swapaxes(carry_p[0], 3, 4)
    v_prompt = jnp.swapaxes(carry_p[1], 3, 4)
    return (xp.reshape(bp, lp, d), xs.reshape(bs, ls, d), k_prompt, v_prompt, carry_p[2], st_p,
            carry_s[0], carry_s[1], jnp.stack(lfs), st_s)
```

```python
import functools

import jax
import jax.numpy as jnp
from jax import lax
from jax.experimental import pallas as pl
from jax.experimental.pallas import tpu as pltpu

F32 = jnp.float32
BF16 = jnp.bfloat16

DEPTH = 2
FOX_HEADS = 8
FOX_HEAD_DIM = 64
FOX_WIDTH = FOX_HEADS * FOX_HEAD_DIM
HGRN_HEADS = 4
HGRN_HEAD_DIM = 128
HGRN_WIDTH = HGRN_HEADS * HGRN_HEAD_DIM
RMS_EPS = 1e-6
FOX_SCALE = FOX_HEAD_DIM ** -0.5
LOG2E = 1.4426950408889634

LANES = 128
VMEM_LIMIT_BYTES = 56 * 1024 * 1024
HGRN_FACTOR_LIMIT = 50.0
FOX_SKIP_GAP = 153.0
FOX_NORM_SLACK = 1.001
FOX_FIRST_ROUND_BLOCKS = 3

TM_FFN = 512
TF_FFN = 256
TM_PROJ = 512
T_ATTN = 256
FOX_PAIRS = 4
C_HGRN = 64
TL_HGRN = 1024
HGRN_PROMPT_HEADS = 2
HGRN_SAMPLE_STREAMS = 4
KEY_NORM_ROWS = 1024


def _cparams(*sem):
    return pltpu.CompilerParams(dimension_semantics=sem, vmem_limit_bytes=VMEM_LIMIT_BYTES)


def _rms(x, g):
    ms = jnp.mean(x * x, axis=-1, keepdims=True)
    return x * lax.rsqrt(ms + RMS_EPS) * g


def _silu(x):
    return x * jax.nn.sigmoid(x)


def _log1p_exp_neg(d):
    return jnp.log(1.0 + jnp.exp(-d))


def _log_sigmoid(z):
    return jnp.minimum(z, 0.0) - _log1p_exp_neg(jnp.abs(z))


def _gate_terms(z):
    e = jnp.exp(-jnp.abs(z))
    t = 1.0 + e
    return jnp.minimum(z, 0.0) - jnp.log(t), jnp.where(z >= 0.0, e, 1.0) / t


def _dot(a, b):
    return jnp.dot(a, b, preferred_element_type=F32)


def _dot_nt(a, b):
    return lax.dot_general(a, b, (((1,), (1,)), ((), ())), preferred_element_type=F32)


def _dot_tn(a, b):
    return lax.dot_general(a, b, (((0,), (0,)), ((), ())), preferred_element_type=F32)


def _scan(x, axis, period=None):
    n = x.shape[axis] if period is None else period
    idx = lax.broadcasted_iota(jnp.int32, x.shape, axis)
    if period is not None:
        idx = idx & (period - 1)
    s = 1
    while s < n:
        x = x + jnp.where(idx >= s, pltpu.roll(x, s, axis), 0.0)
        s *= 2
    return x


def _drop_alias_refs(body, n_in, n_alias):
    def wrapped(*refs):
        return body(*refs[:n_in], *refs[n_in + n_alias:])
    return wrapped


_ANY = pl.BlockSpec(memory_space=pl.ANY)


def _mix(fo_ref, ho_ref, hg_ref, gn_ref):
    gn = gn_ref[0]
    d = HGRN_HEAD_DIM
    parts = [fo_ref[...]]
    for h in range(HGRN_HEADS):
        ho = ho_ref[:, h * d:(h + 1) * d]
        parts.append((_rms(ho, gn) * _silu(hg_ref[:, h * d:(h + 1) * d])).astype(BF16))
    return jnp.concatenate(parts, axis=1)


def _ffn_body(*refs, final_norm, with_outproj, tf):
    if with_outproj:
        (x_ref, fo_ref, ho_ref, hg_ref, gn_ref, wout_ref,
         g_ref, wa_ref, wb_ref, wo_ref, gf_ref, o_ref, acc_ref) = refs
        x = x_ref[...] + _dot(_mix(fo_ref, ho_ref, hg_ref, gn_ref), wout_ref[...])
    else:
        x_ref, g_ref, wa_ref, wb_ref, wo_ref, gf_ref, o_ref, acc_ref = refs
        x = x_ref[...]
    f = wo_ref.shape[0]
    xn = _rms(x, g_ref[0]).astype(BF16)
    for c in range(f // tf):
        a = _dot(xn, wa_ref[:, c * tf:(c + 1) * tf])
        b = _dot(xn, wb_ref[:, c * tf:(c + 1) * tf])
        part = _dot((_silu(a) * b).astype(BF16), wo_ref[c * tf:(c + 1) * tf, :])
        if c == 0:
            acc_ref[...] = part
        else:
            acc_ref[...] += part
    y = x + 0.5 * acc_ref[...]
    if final_norm:
        y = _rms(y, gf_ref[...])
    o_ref[...] = y


def _ffn(x, g, wa, wb, wo, gf, outproj=None, *, layer, final_norm, tm, tf):
    m, d = x.shape
    tok = lambda width: pl.BlockSpec((tm, width), lambda i: (i, 0))
    held = lambda a: pl.BlockSpec(a.shape, lambda i: (0,) * a.ndim, pipeline_mode=pl.Buffered(1))
    args, in_specs = [x], [tok(d)]
    if outproj is not None:
        fo, ho, hg, gn, wout = outproj
        args += [fo, ho, hg, gn, wout]
        in_specs += [tok(fo.shape[1]), tok(ho.shape[1]), tok(hg.shape[1]),
                     pl.BlockSpec((1, 1, gn.shape[2]), lambda i: (layer, 0, 0)), held(wout)]
    args += [g, wa, wb, wo, gf]
    in_specs += [pl.BlockSpec((1, 1, d), lambda i: (layer, 0, 0)), held(wa), held(wb), held(wo),
                 pl.BlockSpec((1, d), lambda i: (0, 0))]
    return pl.pallas_call(
        functools.partial(_ffn_body, final_norm=final_norm, with_outproj=outproj is not None, tf=tf),
        grid=(m // tm,),
        in_specs=in_specs,
        out_specs=tok(d),
        out_shape=jax.ShapeDtypeStruct((m, d), F32),
        scratch_shapes=[pltpu.VMEM((tm, d), F32)],
        compiler_params=_cparams("parallel"),
        name="ffn",
    )(*args)


def _ffn_stream_body(*refs, final_norm, with_outproj):
    if with_outproj:
        (x_ref, fo_ref, ho_ref, hg_ref, gn_ref, wout_ref, g_ref, wa_ref, wb_ref, wo_ref, gf_ref,
         o_ref, wab_ref, wbb_ref, wob_ref, woutb_ref, xr_ref, xn_ref, acc_ref) = refs
    else:
        (x_ref, g_ref, wa_ref, wb_ref, wo_ref, gf_ref,
         o_ref, wab_ref, wbb_ref, wob_ref, xr_ref, xn_ref, acc_ref) = refs
    j = pl.program_id(0)

    @pl.when(j == 0)
    def _():
        x = x_ref[...]
        if with_outproj:
            woutb_ref[...] = wout_ref[0].astype(BF16)
            x = x + _dot(_mix(fo_ref, ho_ref, hg_ref, gn_ref), woutb_ref[...])
        xr_ref[...] = x
        xn_ref[...] = _rms(x, g_ref[0]).astype(BF16)
        acc_ref[...] = jnp.zeros_like(acc_ref)

    wab_ref[...] = wa_ref[0].astype(BF16)
    wbb_ref[...] = wb_ref[0].astype(BF16)
    wob_ref[...] = wo_ref[0].astype(BF16)
    xn = xn_ref[...]
    h = _silu(_dot(xn, wab_ref[...])) * _dot(xn, wbb_ref[...])
    acc_ref[...] += _dot(h.astype(BF16), wob_ref[...])

    @pl.when(j == pl.num_programs(0) - 1)
    def _():
        y = xr_ref[...] + 0.5 * acc_ref[...]
        if final_norm:
            y = _rms(y, gf_ref[...])
        o_ref[...] = y


def _ffn_stream(x, g, wi, wo, gf, outproj=None, *, layer, final_norm, tf):
    m, d = x.shape
    f = wo.shape[1]
    nf = f // tf
    whole = lambda rows, width: pl.BlockSpec((rows, width), lambda j: (0, 0))
    args, in_specs = [x], [whole(m, d)]
    out_specs = [whole(m, d), pl.BlockSpec((d, tf), lambda j: (0, j)), pl.BlockSpec((d, tf), lambda j: (0, j)),
                 pl.BlockSpec((tf, d), lambda j: (j, 0))]
    out_shape = [jax.ShapeDtypeStruct((m, d), F32), jax.ShapeDtypeStruct((d, f), BF16),
                 jax.ShapeDtypeStruct((d, f), BF16), jax.ShapeDtypeStruct((f, d), BF16)]
    if outproj is not None:
        fo, ho, hg, gn, wout = outproj
        dm = wout.shape[1]
        args += [fo, ho, hg, gn, wout]
        in_specs += [whole(m, fo.shape[1]), whole(m, ho.shape[1]), whole(m, hg.shape[1]),
                     pl.BlockSpec((1, 1, gn.shape[2]), lambda j: (layer, 0, 0)),
                     pl.BlockSpec((1, dm, d), lambda j: (layer, 0, 0))]
        out_specs.append(whole(dm, d))
        out_shape.append(jax.ShapeDtypeStruct((dm, d), BF16))
    args += [g, wi, wi, wo, gf]
    in_specs += [pl.BlockSpec((1, 1, d), lambda j: (layer, 0, 0)),
                 pl.BlockSpec((1, d, tf), lambda j: (layer, 0, j)),
                 pl.BlockSpec((1, d, tf), lambda j: (layer, 0, j + nf)),
                 pl.BlockSpec((1, tf, d), lambda j: (layer, j, 0)),
                 whole(1, d)]
    return pl.pallas_call(
        functools.partial(_ffn_stream_body, final_norm=final_norm, with_outproj=outproj is not None),
        grid=(nf,),
        in_specs=in_specs,
        out_specs=out_specs,
        out_shape=out_shape,
        scratch_shapes=[pltpu.VMEM((m, d), F32), pltpu.VMEM((m, d), BF16), pltpu.VMEM((m, d), F32)],
        compiler_params=_cparams("arbitrary"),
        name="ffn_stream",
    )(*args)


def _inproj_body(x_ref, g_ref, wm_ref, wf_ref, bf_ref, lbp_ref,
                 q_ref, k_ref, v_ref, kh_ref, vh_ref, lfr_ref,
                 hq_ref, hlf_ref, hk_ref, hv_ref, hg_ref, *, layer, sample):
    h = _rms(x_ref[...], g_ref[0]).astype(BF16)
    w, hd = FOX_WIDTH, FOX_HEAD_DIM
    wc = 2 * LANES
    heads_per_chunk = wc // hd

    def chunks(i):
        for n in range(w // wc):
            yield n, slice(n * wc, (n + 1) * wc), _dot(h, wm_ref[:, i * w + n * wc:i * w + (n + 1) * wc])

    def store_heads(ref, n, val):
        h0 = n * heads_per_chunk
        if sample:
            for hh in range(heads_per_chunk):
                piece = val[:, hh * hd:(hh + 1) * hd]
                ref[0, :, h0 + hh] = piece.reshape(ref.shape[1], ref.shape[3], hd)
        else:
            ref[0, 0, h0:h0 + heads_per_chunk] = val.T.reshape(heads_per_chunk, hd, val.shape[0])

    p = lbp_ref[...]
    e = jnp.exp(p - jnp.max(p, axis=0, keepdims=True))
    sm = e / jnp.sum(e, axis=0, keepdims=True)
    cs = sm[0:1]
    for r in range(1, layer + 1):
        cs = cs + sm[r:r + 1]
    lb = cs - sm[0:1]
    log_lb = jnp.log(lb)
    log_1m_lb = jnp.log1p(-lb)

    for n, cols, z in chunks(4):
        log_sig, sig_neg = _gate_terms(z)
        a = log_lb[:, cols]
        b = log_1m_lb[:, cols] + log_sig
        hlf_ref[:, cols] = jnp.maximum(a, b) + _log1p_exp_neg(jnp.abs(a - b))
        hk_ref[:, cols] = (1.0 - lb[:, cols]) * sig_neg
    for n, cols, hq in chunks(3):
        hq_ref[:, cols] = _silu(hq)

    lf = _log_sigmoid(_dot(h, wf_ref[...]) + bf_ref[...])
    lfr = lf.T[:FOX_HEADS, :]
    if sample:
        lfr_ref[...] = lfr
    else:
        lfr_ref[0, 0] = lfr

    for n, cols, q in chunks(0):
        q_ref[:, cols] = (q * (FOX_SCALE * LOG2E)).astype(BF16)
    for n, cols, k in chunks(1):
        store_heads(kh_ref, n, k)
        k_ref[:, cols] = k.astype(BF16)
    for n, cols, v in chunks(2):
        store_heads(vh_ref, n, v)
        v_ref[:, cols] = v.astype(BF16)
    for n, cols, hv in chunks(5):
        hv_ref[:, cols] = hv.astype(BF16)
    for n, cols, hg in chunks(6):
        hg_ref[:, cols] = hg


def _inproj(x, g, wm, wf, bfp, lbp, carried, *, layer, sample, batch, tm):
    m, d = x.shape
    w, hd, nh = FOX_WIDTH, FOX_HEAD_DIM, FOX_HEADS
    seq = m // batch
    tiles_per_seq = seq // tm if not sample else 1
    tok = lambda width: pl.BlockSpec((tm, width), lambda i: (i, 0))
    full = lambda a: pl.BlockSpec(a.shape, lambda i: (0,) * a.ndim)
    sds = jax.ShapeDtypeStruct
    if sample:
        heads_shape = (DEPTH, batch, nh, seq, hd)
        heads_spec = pl.BlockSpec((1, batch, nh, seq, hd), lambda i: (layer, 0, 0, 0, 0))
        lfr_shape, lfr_spec = (nh, m), pl.BlockSpec((nh, m), lambda i: (0, 0))
    else:
        heads_shape = (DEPTH, batch, nh, hd, seq)
        heads_spec = pl.BlockSpec((1, 1, nh, hd, tm),
                                  lambda i: (layer, i // tiles_per_seq, 0, 0, i % tiles_per_seq))
        lfr_shape = (DEPTH, batch, nh, seq)
        lfr_spec = pl.BlockSpec((1, 1, nh, tm), lambda i: (layer, i // tiles_per_seq, 0, i % tiles_per_seq))
    out_specs = [tok(w), tok(w), tok(w), heads_spec, heads_spec, lfr_spec,
                 tok(w), tok(w), tok(w), tok(w), tok(w)]
    out_shape = [sds((m, w), BF16), sds((m, w), BF16), sds((m, w), BF16),
                 sds(heads_shape, F32), sds(heads_shape, F32), sds(lfr_shape, F32),
                 sds((m, w), F32), sds((m, w), F32), sds((m, w), F32), sds((m, w), BF16), sds((m, w), F32)]
    body = functools.partial(_inproj_body, layer=layer, sample=sample)
    args = [x, g, wm, wf, bfp, lbp]
    in_specs = [tok(d), pl.BlockSpec((1, 1, d), lambda i: (layer, 0, 0)), full(wm), full(wf), full(bfp), full(lbp)]
    aliases = {}
    if carried is not None:
        out_index = (3, 4, 5)
        for n, buf in enumerate(carried):
            aliases[len(args)] = out_index[n]
            args.append(buf)
            in_specs.append(_ANY)
        body = _drop_alias_refs(body, 6, len(carried))
    return pl.pallas_call(
        body, grid=(m // tm,), in_specs=in_specs, out_specs=out_specs, out_shape=out_shape,
        input_output_aliases=aliases, compiler_params=_cparams("arbitrary"), name="inproj",
    )(*args)


def _cumsum_body(lfr_ref, cr_ref):
    cr_ref[0] = _scan(lfr_ref[0, 0], 1) * LOG2E


def _fox_cumsum(lf_rows, *, layer):
    _, b, h, l = lf_rows.shape
    return pl.pallas_call(
        _cumsum_body,
        grid=(b,),
        in_specs=[pl.BlockSpec((1, 1, h, l), lambda i: (layer, i, 0, 0))],
        out_specs=pl.BlockSpec((1, h, l), lambda i: (i, 0, 0)),
        out_shape=jax.ShapeDtypeStruct((b, h, l), F32),
        compiler_params=_cparams("parallel"),
        name="fox_cumsum",
    )(lf_rows)


def _fox_prompt_body(q_ref, k_ref, v_ref, cr_ref, o_ref, m_ref, l_ref, acc_ref, kn_ref, trips_ref, *, t, pairs):
    i = pl.program_id(2)
    nk = cr_ref.shape[2]
    hd = FOX_HEAD_DIM
    lane = lax.broadcasted_iota(jnp.int32, (t, LANES), 1)
    row = lax.broadcasted_iota(jnp.int32, (t, t), 0)
    col = lax.broadcasted_iota(jnp.int32, (t, t), 1)
    head_lanes = [lane < hd, lane >= hd]
    lanes_of = [slice(p * LANES, (p + 1) * LANES) for p in range(pairs)]

    @pl.when(i == 0)
    def _():
        step = KEY_NORM_ROWS
        lane_s = lax.broadcasted_iota(jnp.int32, (step, LANES), 1)
        for p in range(pairs):
            def chunk(ci, mx, p=p):
                kf = k_ref[0, pl.ds(pl.multiple_of(ci * step, step), step), lanes_of[p]].astype(F32)
                k2 = kf * kf
                n0 = jnp.max(jnp.sum(jnp.where(lane_s < hd, k2, 0.0), axis=1, keepdims=True))
                n1 = jnp.max(jnp.sum(jnp.where(lane_s >= hd, k2, 0.0), axis=1, keepdims=True))
                return jnp.maximum(mx[0], n0), jnp.maximum(mx[1], n1)

            n0, n1 = lax.fori_loop(0, k_ref.shape[1] // step, chunk, (jnp.float32(0.0), jnp.float32(0.0)))
            kn_ref[2 * p] = jnp.sqrt(n0)
            kn_ref[2 * p + 1] = jnp.sqrt(n1)

    causal = jnp.concatenate([col <= row, col <= row], axis=0)
    qs, qn = [], []
    for p in range(pairs):
        qq = q_ref[0, :, lanes_of[p]]
        zero = jnp.zeros_like(qq)
        qs.append(jnp.concatenate([jnp.where(head_lanes[h], qq, zero) for h in range(2)], axis=0))
        q2 = qq.astype(F32) * qq.astype(F32)
        qn.append(jnp.concatenate(
            [jnp.sqrt(jnp.sum(jnp.where(head_lanes[h], q2, 0.0), axis=1, keepdims=True)) for h in range(2)],
            axis=0))

    def scores(p, j, diag):
        kk = k_ref[0, pl.ds(pl.multiple_of(j * t, t), t), lanes_of[p]]
        s = _dot_nt(qs[p], kk)
        s = jnp.concatenate(
            [s[h * t:(h + 1) * t] - cr_ref[0, 2 * p + h, pl.ds(j, 1), :] for h in range(2)], axis=0)
        return jnp.where(causal, s, -jnp.inf) if diag else s

    def values(p, j):
        return v_ref[0, pl.ds(pl.multiple_of(j * t, t), t), lanes_of[p]]

    def write_out(p, acc, l):
        o = acc / l
        o_ref[0, :, lanes_of[p]] = jnp.where(head_lanes[0], o[:t], o[t:]).astype(BF16)

    def first_round(js, j_next):
        state = []
        for p in range(pairs):
            m = l = acc = None
            for n, j in enumerate(js):
                s = scores(p, j, n == len(js) - 1)
                m_blk = jnp.max(s, axis=1, keepdims=True)
                if m is None:
                    m_next = m_blk
                    pr = jnp.exp2(s - m_next)
                    l = jnp.sum(pr, axis=1, keepdims=True)
                    acc = _dot(pr.astype(BF16), values(p, j))
                else:
                    m_next = jnp.maximum(m, m_blk)
                    alpha = jnp.exp2(m - m_next)
                    pr = jnp.exp2(s - m_next)
                    l = alpha * l + jnp.sum(pr, axis=1, keepdims=True)
                    acc = alpha * acc + _dot(pr.astype(BF16), values(p, j))
                m = m_next
            state.append((m, l, acc))

        jidx = lax.broadcasted_iota(jnp.int32, (nk, 1), 0)
        first_needed = jnp.full((nk, 1), nk, jnp.int32)
        for p in range(pairs):
            m = state[p][0]
            for h in range(2):
                rows = slice(h * t, (h + 1) * t)
                gap = qn[p][rows] * (kn_ref[2 * p + h] * FOX_NORM_SLACK) - m[rows]
                reach = jnp.max(gap, axis=0, keepdims=True)
                c_min = jnp.min(cr_ref[0, 2 * p + h], axis=1, keepdims=True)
                first_needed = jnp.minimum(first_needed, jnp.where(reach - c_min >= -FOX_SKIP_GAP, jidx, nk))
        trips = j_next + 1 - jnp.minimum(jnp.min(first_needed), j_next + 1)
        trips_ref[0] = trips

        @pl.when(trips == 0)
        def _():
            for p in range(pairs):
                write_out(p, state[p][2], state[p][1])

        @pl.when(trips > 0)
        def _():
            for p in range(pairs):
                m, l, acc = state[p]
                m_ref[p] = jnp.broadcast_to(m, m_ref.shape[1:])
                l_ref[p] = jnp.broadcast_to(l, l_ref.shape[1:])
                acc_ref[p] = acc

    def block(j):
        for p in range(pairs):
            s = scores(p, j, False)
            m_prev = m_ref[p]
            m_next = jnp.maximum(m_prev, jnp.max(s, axis=1, keepdims=True))
            pr = jnp.exp2(s - jnp.concatenate([m_next] * (t // LANES), axis=1))
            alpha = jnp.exp2(m_prev - m_next)
            l_ref[p] = alpha * l_ref[p] + jnp.sum(pr, axis=1, keepdims=True)
            m_ref[p] = m_next
            acc_ref[p] = alpha * acc_ref[p] + _dot(pr.astype(BF16), values(p, j))

    window = FOX_FIRST_ROUND_BLOCKS

    @pl.when(i >= window - 1)
    def _():
        first_round([i - (window - 1) + n for n in range(window)], i - window)

    @pl.when(i < window - 1)
    def _():
        first_round([i], i - 1)

    trips = trips_ref[0]

    @pl.when(trips > 0)
    def _():
        j_next = jnp.where(i >= window - 1, i - window, i - 1)

        def earlier(n, carry):
            block(j_next - n)
            return carry

        lax.fori_loop(0, trips, earlier, 0)
        for p in range(pairs):
            write_out(p, acc_ref[p], l_ref[p])


def _fox_prompt(q, k, v, c_rows, *, t, pairs):
    b, l, w = q.shape
    ngrp = w // (pairs * LANES)
    nk = l // t
    wide = pairs * LANES
    return pl.pallas_call(
        functools.partial(_fox_prompt_body, t=t, pairs=pairs),
        grid=(b, ngrp, l // t),
        in_specs=[
            pl.BlockSpec((1, t, wide), lambda bi, g, i: (bi, i, g)),
            pl.BlockSpec((1, l, wide), lambda bi, g, i: (bi, 0, g), pipeline_mode=pl.Buffered(1)),
            pl.BlockSpec((1, l, wide), lambda bi, g, i: (bi, 0, g), pipeline_mode=pl.Buffered(1)),
            pl.BlockSpec((1, 2 * pairs, nk, t), lambda bi, g, i: (bi, g, 0, 0)),
        ],
        out_specs=pl.BlockSpec((1, t, wide), lambda bi, g, i: (bi, i, g)),
        out_shape=jax.ShapeDtypeStruct((b, l, w), BF16),
        scratch_shapes=[pltpu.VMEM((pairs, 2 * t, LANES), F32), pltpu.VMEM((pairs, 2 * t, LANES), F32),
                        pltpu.VMEM((pairs, 2 * t, LANES), F32), pltpu.SMEM((2 * pairs,), F32),
                        pltpu.SMEM((1,), jnp.int32)],
        compiler_params=_cparams("parallel", "parallel", "arbitrary"),
        name="fox_prompt",
    )(q, k, v, c_rows)


def _sample_scan_body(clf_ref, lfr_ref, cc_ref, cnr_ref):
    cc_ref[...] = _scan(clf_ref[0], 1) * LOG2E
    cnr_ref[...] = _scan(lfr_ref[...], 1) * LOG2E


def _fox_sample_scan(clf, lf_row, *, layer):
    _, r, p_len = clf.shape
    full = lambda a: pl.BlockSpec(a.shape, lambda i: (0,) * a.ndim)
    return pl.pallas_call(
        _sample_scan_body,
        grid=(1,),
        in_specs=[pl.BlockSpec((1, r, p_len), lambda i: (layer, 0, 0)), full(lf_row)],
        out_specs=[pl.BlockSpec((r, p_len), lambda i: (0, 0)), full(lf_row)],
        out_shape=[jax.ShapeDtypeStruct((r, p_len), F32), jax.ShapeDtypeStruct(lf_row.shape, F32)],
        compiler_params=_cparams("arbitrary"),
        name="fox_sample_scan",
    )(clf, lf_row)


def _fox_sample_body(q_ref, k_ref, v_ref, ck_ref, cv_ref, cc_ref, cnr_ref, o_ref):
    t = q_ref.shape[1]
    nh, p_len = ck_ref.shape[2], ck_ref.shape[4]
    hd = FOX_HEAD_DIM
    lane = lax.broadcasted_iota(jnp.int32, (t, LANES), 1)
    row = lax.broadcasted_iota(jnp.int32, (t, t), 0)
    col = lax.broadcasted_iota(jnp.int32, (t, t), 1)
    c_cache = cc_ref[...]
    cn_row = cnr_ref[...]
    outs = []
    for h in range(nh):
        grp = slice((h // 2) * LANES, (h // 2 + 1) * LANES)
        lo = (h % 2) * hd
        qq = q_ref[0, :, grp]
        qm = jnp.where((lane >= lo) & (lane < lo + hd), qq, jnp.zeros_like(qq))
        q_h = qq[:, lo:lo + hd]
        kc_t = ck_ref[0, 0, h].astype(BF16)
        vc_t = cv_ref[0, 0, h].astype(BF16)
        c_h = c_cache[h:h + 1, :]
        tot = c_h[:, p_len - 1:p_len]
        s_c = _dot(q_h, kc_t) + (tot - c_h)
        s_s = _dot_nt(qm, k_ref[0, :, grp]) - cn_row[h:h + 1, :t]
        s_s = jnp.where(col <= row, s_s, -jnp.inf)
        m = jnp.maximum(jnp.max(s_c, axis=1, keepdims=True), jnp.max(s_s, axis=1, keepdims=True))
        p_c = jnp.exp2(s_c - m)
        p_s = jnp.exp2(s_s - m)
        den = jnp.sum(p_c, axis=1, keepdims=True) + jnp.sum(p_s, axis=1, keepdims=True)
        o = _dot_nt(p_c.astype(BF16), vc_t) + _dot(p_s.astype(BF16), v_ref[0, :, grp])[:, lo:lo + hd]
        outs.append(o / den)
    o_ref[0] = jnp.concatenate(outs, axis=1).astype(BF16)


def _fox_sample(q, k, v, cache_k, cache_v, c_cache, cn_row, *, layer):
    b, t, w = q.shape
    nh, hd, p_len = cache_k.shape[2:]
    tokq = pl.BlockSpec((1, t, w), lambda bi: (bi, 0, 0))
    cache = pl.BlockSpec((1, 1, nh, hd, p_len), lambda bi: (layer, bi, 0, 0, 0))
    return pl.pallas_call(
        _fox_sample_body,
        grid=(b,),
        in_specs=[tokq, tokq, tokq, cache, cache,
                  pl.BlockSpec((nh, p_len), lambda bi: (bi, 0)),
                  pl.BlockSpec((nh, LANES), lambda bi: (bi, 0))],
        out_specs=tokq,
        out_shape=jax.ShapeDtypeStruct((b, t, w), BF16),
        compiler_params=_cparams("parallel"),
        name="fox_sample",
    )(q, k, v, cache_k, cache_v, c_cache, cn_row)


def _hgrn_body(q_ref, lf_ref, k_ref, v_ref, s0_ref, o_ref, so_ref, st_ref, tmp_ref, *,
               c, n_chunks, sps, hps, zero_init):
    ti = pl.program_id(2)
    d = HGRN_HEAD_DIM
    chains = [(s, hh) for s in range(sps) for hh in range(hps)]
    slot = {ch: n for n, ch in enumerate(chains)}
    lanes_of = [slice(hh * d, (hh + 1) * d) for hh in range(hps)]

    @pl.when(ti == 0)
    def _():
        for s, hh in chains:
            st_ref[slot[(s, hh)]] = jnp.zeros((d, d), F32) if zero_init else s0_ref[0, s, hh].T

    row = lax.broadcasted_iota(jnp.int32, (c, c), 0)
    col = lax.broadcasted_iota(jnp.int32, (c, c), 1)
    rowc = lax.broadcasted_iota(jnp.int32, (c, 1), 0)
    half = c // 2
    b_all = [_scan(lf_ref[s], 0, period=c) for s in range(sps)]

    span = jnp.zeros((1, hps * d), F32)
    for s in range(sps):
        for ci in range(n_chunks):
            b = b_all[s][ci * c:(ci + 1) * c]
            b_mid = b[half - 1:half]
            span = jnp.maximum(span, jnp.maximum(b[0:1] - b_mid, b_mid - b[c - 1:c]))
    factorable = jnp.max(span) <= HGRN_FACTOR_LIMIT

    def carry_state(st, k, v, b):
        b_last = b[c - 1:c]
        kh = (k * jnp.exp(b_last - b)).astype(BF16)
        return st * jnp.exp(b_last) + _dot_tn(v, kh)

    @pl.when(factorable)
    def _():
        units = [(ch, ci) for ch in chains for ci in range(n_chunks)]
        tile_of = {u: (u[0][0], slice(u[1] * c, (u[1] + 1) * c), lanes_of[u[0][1]]) for u in units}
        sc, ds, qe, grow = {}, {}, {}, {}
        for u in units:
            s, rs, hs = tile_of[u]
            q = q_ref[s, rs, hs]
            k = k_ref[s, rs, hs]
            b = b_all[s][rs, hs]
            b_mid = b[half - 1:half]
            b_last = b[c - 1:c]
            qe[u] = (q * jnp.exp(b)).astype(BF16)
            grow[u] = jnp.exp(b_last)
            qt = (q * jnp.exp(b - b_mid)).astype(BF16)
            kt = (k * jnp.exp(b_mid - b)).astype(BF16)
            kh = (k * jnp.exp(b_last - b)).astype(BF16)
            sc[u] = _dot_nt(qt, kt)
            ds[u] = _dot_tn(v_ref[s, rs, hs], kh)
        st_in = {}
        for ch in chains:
            st = st_ref[slot[ch]]
            for ci in range(n_chunks):
                st_in[(ch, ci)] = st.astype(BF16)
                st = st * grow[(ch, ci)] + ds[(ch, ci)]
            st_ref[slot[ch]] = st
        for u in units:
            s, rs, hs = tile_of[u]
            intra = _dot(jnp.where(col <= row, sc[u], 0.0).astype(BF16), v_ref[s, rs, hs])
            o_ref[s, rs, hs] = _dot_nt(qe[u], st_in[u]) + intra

    @pl.when(jnp.logical_not(factorable))
    def _():
        for s, hh in chains:
            hs = lanes_of[hh]

            def chunk(ci, carry, s=s, hs=hs, sl=slot[(s, hh)]):
                rs = pl.ds(pl.multiple_of(ci * c, c), c)
                q = q_ref[s, rs, hs]
                k = k_ref[s, rs, hs]
                v = v_ref[s, rs, hs]
                b = _scan(lf_ref[s, rs, hs], 0)
                st = st_ref[sl]
                tmp_ref[0] = b
                tmp_ref[1] = k
                tmp_ref[2] = v.astype(F32)

                def key_row(r, acc):
                    bs = tmp_ref[0, pl.ds(r, 1), :]
                    ks = tmp_ref[1, pl.ds(r, 1), :]
                    vs = tmp_ref[2, pl.ds(r, 1), :]
                    w = jnp.exp(jnp.minimum(b - bs, 0.0))
                    a = jnp.sum(q * ks * w, axis=1, keepdims=True)
                    return acc + jnp.where(rowc >= r, a, 0.0) * vs

                o = _dot_nt((q * jnp.exp(b)).astype(BF16), st.astype(BF16))
                o_ref[s, rs, hs] = lax.fori_loop(0, c, key_row, o)
                st_ref[sl] = carry_state(st, k, v, b)
                return carry

            lax.fori_loop(0, n_chunks, chunk, 0)

    @pl.when(ti == pl.num_programs(2) - 1)
    def _():
        for s, hh in chains:
            so_ref[0, s, hh] = st_ref[slot[(s, hh)]].T


def _hgrn(hq, hlf, hk, hv, s0, carried, *, layer, c, tl, sps, hps):
    b, l, w = hq.shape
    d = HGRN_HEAD_DIM
    nh = w // d
    tok = pl.BlockSpec((sps, tl, hps * d), lambda bi, h, ti: (bi, ti, h))
    st = pl.BlockSpec((1, sps, hps, d, d), lambda bi, h, ti: (layer, bi, h, 0, 0))
    zero_init = s0 is None
    body = functools.partial(_hgrn_body, c=c, n_chunks=tl // c, sps=sps, hps=hps, zero_init=zero_init)
    args, in_specs = [hq, hlf, hk, hv], [tok, tok, tok, tok]
    if zero_init:
        core = body
        body = lambda q, lf, k, v, *rest: core(q, lf, k, v, None, *rest)
    else:
        args.append(s0)
        in_specs.append(st)
    aliases = {}
    if carried is not None:
        aliases[len(args)] = 1
        body = _drop_alias_refs(body, len(args), 1)
        args.append(carried)
        in_specs.append(_ANY)
    return pl.pallas_call(
        body,
        grid=(b // sps, nh // hps, l // tl),
        in_specs=in_specs,
        out_specs=[tok, st],
        out_shape=[jax.ShapeDtypeStruct((b, l, w), F32), jax.ShapeDtypeStruct((DEPTH, b, nh, d, d), F32)],
        scratch_shapes=[pltpu.VMEM((sps * hps, d, d), F32), pltpu.VMEM((3, c, d), F32)],
        input_output_aliases=aliases,
        compiler_params=_cparams("parallel", "parallel", "arbitrary"),
        name="hgrn",
    )(*args)


def kernel(x_prompt, x_sample, cache_k, cache_v, cache_logf, state_hgrn, norm_ffn1, ffn1_wi, ffn1_wo,
           norm_mix, w_in, b_fgate, hgrn_lb, hgrn_gnorm, w_out, norm_ffn2, ffn2_wi, ffn2_wo, norm_final):
    bp, lp, d = x_prompt.shape
    bs, ls, _ = x_sample.shape
    mp, ms = bp * lp, bs * ls
    p_len = cache_k.shape[3]
    f, w, h8 = FOX_WIDTH, HGRN_WIDTH, FOX_HEADS

    xp = x_prompt.reshape(mp, d)
    xs = x_sample.reshape(ms, d)
    gfin = norm_final.reshape(1, d)
    g1, g2, gm = (a.reshape(DEPTH, 1, d) for a in (norm_ffn1, norm_ffn2, norm_mix))
    gn = hgrn_gnorm.reshape(DEPTH, 1, HGRN_HEAD_DIM)
    clf = cache_logf.reshape(DEPTH, bs * h8, p_len)
    cache_kt = jnp.swapaxes(cache_k, 3, 4)
    cache_vt = jnp.swapaxes(cache_v, 3, 4)

    carry_p = carry_s = None
    st_p = st_s = None
    lfs = []
    for l in range(DEPTH):
        last = l == DEPTH - 1
        wl = w_in[l]
        wm = jnp.concatenate([wl[:, :3 * f], wl[:, 3 * f + h8:]], axis=1).astype(BF16)
        wf = jnp.pad(wl[:, 3 * f:3 * f + h8], ((0, 0), (0, LANES - h8))).astype(BF16)
        bfp = jnp.pad(b_fgate[l].reshape(1, h8), ((0, 0), (0, LANES - h8)))

        xs, wa, wb, wo = _ffn_stream(xs, g1, ffn1_wi, ffn1_wo, gfin, layer=l, final_norm=False, tf=TF_FFN)
        xp = _ffn(xp, g1, wa, wb, wo, gfin, layer=l, final_norm=False, tm=TM_FFN, tf=TF_FFN)

        (q, k, v, kh, vh, lfr, hq, hlf, hk, hv, hg) = _inproj(
            xp, gm, wm, wf, bfp, hgrn_lb, carry_p, layer=l, sample=False, batch=bp, tm=TM_PROJ)
        carry_p = (kh, vh, lfr)
        (q_s, k_s, v_s, kh_s, vh_s, lfr_s, hq_s, hlf_s, hk_s, hv_s, hg_s) = _inproj(
            xs, gm, wm, wf, bfp, hgrn_lb, carry_s, layer=l, sample=True, batch=bs, tm=ms)
        carry_s = (kh_s, vh_s)

        c_row = _fox_cumsum(lfr, layer=l)
        fo = _fox_prompt(q.reshape(bp, lp, f), k.reshape(bp, lp, f), v.reshape(bp, lp, f),
                         c_row.reshape(bp, h8, lp // T_ATTN, T_ATTN), t=T_ATTN, pairs=FOX_PAIRS)
        lfs_l = lfr_s.reshape(h8, bs, ls).transpose(1, 0, 2)
        lfs.append(lfs_l)
        lfs_pad = jnp.pad(lfs_l, ((0, 0), (0, 0), (0, LANES - ls))).reshape(bs * h8, LANES)
        c_cache, cn_row = _fox_sample_scan(clf, lfs_pad, layer=l)
        fo_s = _fox_sample(q_s.reshape(bs, ls, f), k_s.reshape(bs, ls, f), v_s.reshape(bs, ls, f),
                           cache_kt, cache_vt, c_cache, cn_row, layer=l)
        to3 = lambda a, b_, l_: a.reshape(b_, l_, w)
        ho, st_p = _hgrn(to3(hq, bp, lp), to3(hlf, bp, lp), to3(hk, bp, lp), to3(hv, bp, lp), None, st_p,
                         layer=l, c=C_HGRN, tl=TL_HGRN, sps=1, hps=HGRN_PROMPT_HEADS)
        ho_s, st_s = _hgrn(to3(hq_s, bs, ls), to3(hlf_s, bs, ls), to3(hk_s, bs, ls), to3(hv_s, bs, ls),
                           state_hgrn, st_s, layer=l, c=ls, tl=ls, sps=HGRN_SAMPLE_STREAMS, hps=HGRN_HEADS)

        xs, wa, wb, wo, wout = _ffn_stream(
            xs, g2, ffn2_wi, ffn2_wo, gfin, (fo_s.reshape(ms, f), ho_s.reshape(ms, w), hg_s, gn, w_out),
            layer=l, final_norm=last, tf=TF_FFN)
        xp = _ffn(xp, g2, wa, wb, wo, gfin, (fo.reshape(mp, f), ho.reshape(mp, w), hg, gn, wout),
                  layer=l, final_norm=last, tm=TM_FFN, tf=TF_FFN)

    k_prompt = jnp.swapaxes(carry_p[0], 3, 4)
    v_prompt = jnp.swapaxes(carry_p[1], 3, 4)
    return (xp.reshape(bp, lp, d), xs.reshape(bs, ls, d), k_prompt, v_prompt, carry_p[2], st_p,
            carry_s[0], carry_s[1], jnp.stack(lfs), st_s)
```

```python
import functools

import jax
import jax.numpy as jnp
from jax import lax
from jax.experimental import pallas as pl
from jax.experimental.pallas import tpu as pltpu

F32 = jnp.float32
BF16 = jnp.bfloat16

DEPTH = 2
FOX_HEADS = 8
FOX_HEAD_DIM = 64
FOX_WIDTH = FOX_HEADS * FOX_HEAD_DIM
HGRN_HEADS = 4
HGRN_HEAD_DIM = 128
HGRN_WIDTH = HGRN_HEADS * HGRN_HEAD_DIM
RMS_EPS = 1e-6
FOX_SCALE = FOX_HEAD_DIM ** -0.5
LOG2E = 1.4426950408889634

LANES = 128
VMEM_LIMIT_BYTES = 56 * 1024 * 1024
HGRN_FACTOR_LIMIT = 50.0
FOX_SKIP_GAP = 153.0
FOX_NORM_SLACK = 1.001
FOX_FIRST_ROUND_BLOCKS = 3

TM_FFN = 512
TF_FFN = 256
TM_PROJ = 512
T_ATTN = 256
FOX_PAIRS = 4
C_HGRN = 64
TL_HGRN = 1024
HGRN_PROMPT_HEADS = 2
HGRN_SAMPLE_STREAMS = 4
KEY_NORM_ROWS = 1024


def _cparams(*sem):
    return pltpu.CompilerParams(dimension_semantics=sem, vmem_limit_bytes=VMEM_LIMIT_BYTES)


def _rms(x, g):
    ms = jnp.mean(x * x, axis=-1, keepdims=True)
    return x * lax.rsqrt(ms + RMS_EPS) * g


def _silu(x):
    return x * jax.nn.sigmoid(x)


def _log1p_exp_neg(d):
    return jnp.log(1.0 + jnp.exp(-d))


def _log_sigmoid(z):
    return jnp.minimum(z, 0.0) - _log1p_exp_neg(jnp.abs(z))


def _gate_terms(z):
    e = jnp.exp(-jnp.abs(z))
    t = 1.0 + e
    return jnp.minimum(z, 0.0) - jnp.log(t), jnp.where(z >= 0.0, e, 1.0) / t


def _dot(a, b):
    return jnp.dot(a, b, preferred_element_type=F32)


def _dot_nt(a, b):
    return lax.dot_general(a, b, (((1,), (1,)), ((), ())), preferred_element_type=F32)


def _dot_tn(a, b):
    return lax.dot_general(a, b, (((0,), (0,)), ((), ())), preferred_element_type=F32)


def _scan(x, axis, period=None):
    n = x.shape[axis] if period is None else period
    idx = lax.broadcasted_iota(jnp.int32, x.shape, axis)
    if period is not None:
        idx = idx & (period - 1)
    s = 1
    while s < n:
        x = x + jnp.where(idx >= s, pltpu.roll(x, s, axis), 0.0)
        s *= 2
    return x


def _drop_alias_refs(body, n_in, n_alias):
    def wrapped(*refs):
        return body(*refs[:n_in], *refs[n_in + n_alias:])
    return wrapped


_ANY = pl.BlockSpec(memory_space=pl.ANY)


def _mix(fo_ref, ho_ref, hg_ref, gn_ref):
    gn = gn_ref[0]
    d = HGRN_HEAD_DIM
    parts = [fo_ref[...]]
    for h in range(HGRN_HEADS):
        ho = ho_ref[:, h * d:(h + 1) * d]
        parts.append((_rms(ho, gn) * _silu(hg_ref[:, h * d:(h + 1) * d])).astype(BF16))
    return jnp.concatenate(parts, axis=1)


def _ffn_body(*refs, final_norm, with_outproj, tf):
    if with_outproj:
        (x_ref, fo_ref, ho_ref, hg_ref, gn_ref, wout_ref,
         g_ref, wa_ref, wb_ref, wo_ref, gf_ref, o_ref, acc_ref) = refs
        x = x_ref[...] + _dot(_mix(fo_ref, ho_ref, hg_ref, gn_ref), wout_ref[...])
    else:
        x_ref, g_ref, wa_ref, wb_ref, wo_ref, gf_ref, o_ref, acc_ref = refs
        x = x_ref[...]
    f = wo_ref.shape[0]
    xn = _rms(x, g_ref[0]).astype(BF16)
    for c in range(f // tf):
        a = _dot(xn, wa_ref[:, c * tf:(c + 1) * tf])
        b = _dot(xn, wb_ref[:, c * tf:(c + 1) * tf])
        part = _dot((_silu(a) * b).astype(BF16), wo_ref[c * tf:(c + 1) * tf, :])
        if c == 0:
            acc_ref[...] = part
        else:
            acc_ref[...] += part
    y = x + 0.5 * acc_ref[...]
    if final_norm:
        y = _rms(y, gf_ref[...])
    o_ref[...] = y


def _ffn(x, g, wa, wb, wo, gf, outproj=None, *, layer, final_norm, tm, tf):
    m, d = x.shape
    tok = lambda width: pl.BlockSpec((tm, width), lambda i: (i, 0))
    held = lambda a: pl.BlockSpec(a.shape, lambda i: (0,) * a.ndim, pipeline_mode=pl.Buffered(1))
    args, in_specs = [x], [tok(d)]
    if outproj is not None:
        fo, ho, hg, gn, wout = outproj
        args += [fo, ho, hg, gn, wout]
        in_specs += [tok(fo.shape[1]), tok(ho.shape[1]), tok(hg.shape[1]),
                     pl.BlockSpec((1, 1, gn.shape[2]), lambda i: (layer, 0, 0)), held(wout)]
    args += [g, wa, wb, wo, gf]
    in_specs += [pl.BlockSpec((1, 1, d), lambda i: (layer, 0, 0)), held(wa), held(wb), held(wo),
                 pl.BlockSpec((1, d), lambda i: (0, 0))]
    return pl.pallas_call(
        functools.partial(_ffn_body, final_norm=final_norm, with_outproj=outproj is not None, tf=tf),
        grid=(m // tm,),
        in_specs=in_specs,
        out_specs=tok(d),
        out_shape=jax.ShapeDtypeStruct((m, d), F32),
        scratch_shapes=[pltpu.VMEM((tm, d), F32)],
        compiler_params=_cparams("parallel"),
        name="ffn",
    )(*args)


def _ffn_stream_body(*refs, final_norm, with_outproj):
    if with_outproj:
        (x_ref, fo_ref, ho_ref, hg_ref, gn_ref, wout_ref, g_ref, wa_ref, wb_ref, wo_ref, gf_ref,
         o_ref, wab_ref, wbb_ref, wob_ref, woutb_ref, xr_ref, xn_ref, acc_ref) = refs
    else:
        (x_ref, g_ref, wa_ref, wb_ref, wo_ref, gf_ref,
         o_ref, wab_ref, wbb_ref, wob_ref, xr_ref, xn_ref, acc_ref) = refs
    j = pl.program_id(0)

    @pl.when(j == 0)
    def _():
        x = x_ref[...]
        if with_outproj:
            woutb_ref[...] = wout_ref[0].astype(BF16)
            x = x + _dot(_mix(fo_ref, ho_ref, hg_ref, gn_ref), woutb_ref[...])
        xr_ref[...] = x
        xn_ref[...] = _rms(x, g_ref[0]).astype(BF16)
        acc_ref[...] = jnp.zeros_like(acc_ref)

    wab_ref[...] = wa_ref[0].astype(BF16)
    wbb_ref[...] = wb_ref[0].astype(BF16)
    wob_ref[...] = wo_ref[0].astype(BF16)
    xn = xn_ref[...]
    h = _silu(_dot(xn, wab_ref[...])) * _dot(xn, wbb_ref[...])
    acc_ref[...] += _dot(h.astype(BF16), wob_ref[...])

    @pl.when(j == pl.num_programs(0) - 1)
    def _():
        y = xr_ref[...] + 0.5 * acc_ref[...]
        if final_norm:
            y = _rms(y, gf_ref[...])
        o_ref[...] = y


def _ffn_stream(x, g, wi, wo, gf, outproj=None, *, layer, final_norm, tf):
    m, d = x.shape
    f = wo.shape[1]
    nf = f // tf
    whole = lambda rows, width: pl.BlockSpec((rows, width), lambda j: (0, 0))
    args, in_specs = [x], [whole(m, d)]
    out_specs = [whole(m, d), pl.BlockSpec((d, tf), lambda j: (0, j)), pl.BlockSpec((d, tf), lambda j: (0, j)),
                 pl.BlockSpec((tf, d), lambda j: (j, 0))]
    out_shape = [jax.ShapeDtypeStruct((m, d), F32), jax.ShapeDtypeStruct((d, f), BF16),
                 jax.ShapeDtypeStruct((d, f), BF16), jax.ShapeDtypeStruct((f, d), BF16)]
    if outproj is not None:
        fo, ho, hg, gn, wout = outproj
        dm = wout.shape[1]
        args += [fo, ho, hg, gn, wout]
        in_specs += [whole(m, fo.shape[1]), whole(m, ho.shape[1]), whole(m, hg.shape[1]),
                     pl.BlockSpec((1, 1, gn.shape[2]), lambda j: (layer, 0, 0)),
                     pl.BlockSpec((1, dm, d), lambda j: (layer, 0, 0))]
        out_specs.append(whole(dm, d))
        out_shape.append(jax.ShapeDtypeStruct((dm, d), BF16))
    args += [g, wi, wi, wo, gf]
    in_specs += [pl.BlockSpec((1, 1, d), lambda j: (layer, 0, 0)),
                 pl.BlockSpec((1, d, tf), lambda j: (layer, 0, j)),
                 pl.BlockSpec((1, d, tf), lambda j: (layer, 0, j + nf)),
                 pl.BlockSpec((1, tf, d), lambda j: (layer, j, 0)),
                 whole(1, d)]
    return pl.pallas_call(
        functools.partial(_ffn_stream_body, final_norm=final_norm, with_outproj=outproj is not None),
        grid=(nf,),
        in_specs=in_specs,
        out_specs=out_specs,
        out_shape=out_shape,
        scratch_shapes=[pltpu.VMEM((m, d), F32), pltpu.VMEM((m, d), BF16), pltpu.VMEM((m, d), F32)],
        compiler_params=_cparams("arbitrary"),
        name="ffn_stream",
    )(*args)


def _inproj_body(x_ref, g_ref, wm_ref, wf_ref, bf_ref, lbp_ref,
                 q_ref, k_ref, v_ref, kh_ref, vh_ref, lfr_ref,
                 hq_ref, hlf_ref, hk_ref, hv_ref, hg_ref, *, layer, sample):
    h = _rms(x_ref[...], g_ref[0]).astype(BF16)
    w, hd = FOX_WIDTH, FOX_HEAD_DIM
    wc = 2 * LANES
    heads_per_chunk = wc // hd

    def chunks(i):
        for n in range(w // wc):
            yield n, slice(n * wc, (n + 1) * wc), _dot(h, wm_ref[:, i * w + n * wc:i * w + (n + 1) * wc])

    def store_heads(ref, n, val):
        h0 = n * heads_per_chunk
        if sample:
            for hh in range(heads_per_chunk):
                piece = val[:, hh * hd:(hh + 1) * hd]
                ref[0, :, h0 + hh] = piece.reshape(ref.shape[1], ref.shape[3], hd)
        else:
            ref[0, 0, h0:h0 + heads_per_chunk] = val.T.reshape(heads_per_chunk, hd, val.shape[0])

    p = lbp_ref[...]
    e = jnp.exp(p - jnp.max(p, axis=0, keepdims=True))
    sm = e / jnp.sum(e, axis=0, keepdims=True)
    cs = sm[0:1]
    for r in range(1, layer + 1):
        cs = cs + sm[r:r + 1]
    lb = cs - sm[0:1]
    log_lb = jnp.log(lb)
    log_1m_lb = jnp.log1p(-lb)

    for n, cols, z in chunks(4):
        log_sig, sig_neg = _gate_terms(z)
        a = log_lb[:, cols]
        b = log_1m_lb[:, cols] + log_sig
        hlf_ref[:, cols] = jnp.maximum(a, b) + _log1p_exp_neg(jnp.abs(a - b))
        hk_ref[:, cols] = (1.0 - lb[:, cols]) * sig_neg
    for n, cols, hq in chunks(3):
        hq_ref[:, cols] = _silu(hq)

    lf = _log_sigmoid(_dot(h, wf_ref[...]) + bf_ref[...])
    lfr = lf.T[:FOX_HEADS, :]
    if sample:
        lfr_ref[...] = lfr
    else:
        lfr_ref[0, 0] = lfr

    for n, cols, q in chunks(0):
        q_ref[:, cols] = (q * (FOX_SCALE * LOG2E)).astype(BF16)
    for n, cols, k in chunks(1):
        store_heads(kh_ref, n, k)
        k_ref[:, cols] = k.astype(BF16)
    for n, cols, v in chunks(2):
        store_heads(vh_ref, n, v)
        v_ref[:, cols] = v.astype(BF16)
    for n, cols, hv in chunks(5):
        hv_ref[:, cols] = hv.astype(BF16)
    for n, cols, hg in chunks(6):
        hg_ref[:, cols] = hg


def _inproj(x, g, wm, wf, bfp, lbp, carried, *, layer, sample, batch, tm):
    m, d = x.shape
    w, hd, nh = FOX_WIDTH, FOX_HEAD_DIM, FOX_HEADS
    seq = m // batch
    tiles_per_seq = seq // tm if not sample else 1
    tok = lambda width: pl.BlockSpec((tm, width), lambda i: (i, 0))
    full = lambda a: pl.BlockSpec(a.shape, lambda i: (0,) * a.ndim)
    sds = jax.ShapeDtypeStruct
    if sample:
        heads_shape = (DEPTH, batch, nh, seq, hd)
        heads_spec = pl.BlockSpec((1, batch, nh, seq, hd), lambda i: (layer, 0, 0, 0, 0))
        lfr_shape, lfr_spec = (nh, m), pl.BlockSpec((nh, m), lambda i: (0, 0))
    else:
        heads_shape = (DEPTH, batch, nh, hd, seq)
        heads_spec = pl.BlockSpec((1, 1, nh, hd, tm),
                                  lambda i: (layer, i // tiles_per_seq, 0, 0, i % tiles_per_seq))
        lfr_shape = (DEPTH, batch, nh, seq)
        lfr_spec = pl.BlockSpec((1, 1, nh, tm), lambda i: (layer, i // tiles_per_seq, 0, i % tiles_per_seq))
    out_specs = [tok(w), tok(w), tok(w), heads_spec, heads_spec, lfr_spec,
                 tok(w), tok(w), tok(w), tok(w), tok(w)]
    out_shape = [sds((m, w), BF16), sds((m, w), BF16), sds((m, w), BF16),
                 sds(heads_shape, F32), sds(heads_shape, F32), sds(lfr_shape, F32),
                 sds((m, w), F32), sds((m, w), F32), sds((m, w), F32), sds((m, w), BF16), sds((m, w), F32)]
    body = functools.partial(_inproj_body, layer=layer, sample=sample)
    args = [x, g, wm, wf, bfp, lbp]
    in_specs = [tok(d), pl.BlockSpec((1, 1, d), lambda i: (layer, 0, 0)), full(wm), full(wf), full(bfp), full(lbp)]
    aliases = {}
    if carried is not None:
        out_index = (3, 4, 5)
        for n, buf in enumerate(carried):
            aliases[len(args)] = out_index[n]
            args.append(buf)
            in_specs.append(_ANY)
        body = _drop_alias_refs(body, 6, len(carried))
    return pl.pallas_call(
        body, grid=(m // tm,), in_specs=in_specs, out_specs=out_specs, out_shape=out_shape,
        input_output_aliases=aliases, compiler_params=_cparams("arbitrary"), name="inproj",
    )(*args)


def _cumsum_body(lfr_ref, cr_ref):
    cr_ref[0] = _scan(lfr_ref[0, 0], 1) * LOG2E


def _fox_cumsum(lf_rows, *, layer):
    _, b, h, l = lf_rows.shape
    return pl.pallas_call(
        _cumsum_body,
        grid=(b,),
        in_specs=[pl.BlockSpec((1, 1, h, l), lambda i: (layer, i, 0, 0))],
        out_specs=pl.BlockSpec((1, h, l), lambda i: (i, 0, 0)),
        out_shape=jax.ShapeDtypeStruct((b, h, l), F32),
        compiler_params=_cparams("parallel"),
        name="fox_cumsum",
    )(lf_rows)


def _fox_prompt_body(q_ref, k_ref, v_ref, cr_ref, o_ref, m_ref, l_ref, acc_ref, kn_ref, trips_ref, *, t, pairs):
    i = pl.program_id(2)
    nk = cr_ref.shape[2]
    hd = FOX_HEAD_DIM
    lane = lax.broadcasted_iota(jnp.int32, (t, LANES), 1)
    row = lax.broadcasted_iota(jnp.int32, (t, t), 0)
    col = lax.broadcasted_iota(jnp.int32, (t, t), 1)
    head_lanes = [lane < hd, lane >= hd]
    lanes_of = [slice(p * LANES, (p + 1) * LANES) for p in range(pairs)]

    @pl.when(i == 0)
    def _():
        step = KEY_NORM_ROWS
        lane_s = lax.broadcasted_iota(jnp.int32, (step, LANES), 1)
        for p in range(pairs):
            def chunk(ci, mx, p=p):
                kf = k_ref[0, pl.ds(pl.multiple_of(ci * step, step), step), lanes_of[p]].astype(F32)
                k2 = kf * kf
                n0 = jnp.max(jnp.sum(jnp.where(lane_s < hd, k2, 0.0), axis=1, keepdims=True))
                n1 = jnp.max(jnp.sum(jnp.where(lane_s >= hd, k2, 0.0), axis=1, keepdims=True))
                return jnp.maximum(mx[0], n0), jnp.maximum(mx[1], n1)

            n0, n1 = lax.fori_loop(0, k_ref.shape[1] // step, chunk, (jnp.float32(0.0), jnp.float32(0.0)))
            kn_ref[2 * p] = jnp.sqrt(n0)
            kn_ref[2 * p + 1] = jnp.sqrt(n1)

    causal = jnp.concatenate([col <= row, col <= row], axis=0)
    qs, qn = [], []
    for p in range(pairs):
        qq = q_ref[0, :, lanes_of[p]]
        zero = jnp.zeros_like(qq)
        qs.append(jnp.concatenate([jnp.where(head_lanes[h], qq, zero) for h in range(2)], axis=0))
        q2 = qq.astype(F32) * qq.astype(F32)
        qn.append(jnp.concatenate(
            [jnp.sqrt(jnp.sum(jnp.where(head_lanes[h], q2, 0.0), axis=1, keepdims=True)) for h in range(2)],
            axis=0))

    def scores(p, j, diag):
        kk = k_ref[0, pl.ds(pl.multiple_of(j * t, t), t), lanes_of[p]]
        s = _dot_nt(qs[p], kk)
        s = jnp.concatenate(
            [s[h * t:(h + 1) * t] - cr_ref[0, 2 * p + h, pl.ds(j, 1), :] for h in range(2)], axis=0)
        return jnp.where(causal, s, -jnp.inf) if diag else s

    def values(p, j):
        return v_ref[0, pl.ds(pl.multiple_of(j * t, t), t), lanes_of[p]]

    def write_out(p, acc, l):
        o = acc / l
        o_ref[0, :, lanes_of[p]] = jnp.where(head_lanes[0], o[:t], o[t:]).astype(BF16)

    def first_round(js, j_next):
        def lane_groups(pr):
            return functools.reduce(jnp.add, [pr[:, g * LANES:(g + 1) * LANES] for g in range(t // LANES)])

        state = []
        for p in range(pairs):
            m = l = acc = None
            for n, j in enumerate(js):
                s = scores(p, j, n == len(js) - 1)
                m_blk = jnp.max(s, axis=1, keepdims=True)
                if m is None:
                    m_next = m_blk
                    pr = jnp.exp2(s - m_next)
                    l = lane_groups(pr)
                    acc = _dot(pr.astype(BF16), values(p, j))
                else:
                    m_next = jnp.maximum(m, m_blk)
                    alpha = jnp.exp2(m - m_next)
                    pr = jnp.exp2(s - m_next)
                    l = alpha * l + lane_groups(pr)
                    acc = alpha * acc + _dot(pr.astype(BF16), values(p, j))
                m = m_next
            state.append((m, jnp.sum(l, axis=1, keepdims=True), acc))

        jidx = lax.broadcasted_iota(jnp.int32, (nk, 1), 0)
        first_needed = jnp.full((nk, 1), nk, jnp.int32)
        for p in range(pairs):
            m = state[p][0]
            for h in range(2):
                rows = slice(h * t, (h + 1) * t)
                gap = qn[p][rows] * (kn_ref[2 * p + h] * FOX_NORM_SLACK) - m[rows]
                reach = jnp.max(gap, axis=0, keepdims=True)
                c_min = jnp.min(cr_ref[0, 2 * p + h], axis=1, keepdims=True)
                first_needed = jnp.minimum(first_needed, jnp.where(reach - c_min >= -FOX_SKIP_GAP, jidx, nk))
        trips = j_next + 1 - jnp.minimum(jnp.min(first_needed), j_next + 1)
        trips_ref[0] = trips

        @pl.when(trips == 0)
        def _():
            for p in range(pairs):
                write_out(p, state[p][2], state[p][1])

        @pl.when(trips > 0)
        def _():
            for p in range(pairs):
                m, l, acc = state[p]
                m_ref[p] = jnp.broadcast_to(m, m_ref.shape[1:])
                l_ref[p] = jnp.broadcast_to(l, l_ref.shape[1:])
                acc_ref[p] = acc

    def block(j):
        for p in range(pairs):
            s = scores(p, j, False)
            m_prev = m_ref[p]
            m_next = jnp.maximum(m_prev, jnp.max(s, axis=1, keepdims=True))
            pr = jnp.exp2(s - jnp.concatenate([m_next] * (t // LANES), axis=1))
            alpha = jnp.exp2(m_prev - m_next)
            l_ref[p] = alpha * l_ref[p] + jnp.sum(pr, axis=1, keepdims=True)
            m_ref[p] = m_next
            acc_ref[p] = alpha * acc_ref[p] + _dot(pr.astype(BF16), values(p, j))

    window = FOX_FIRST_ROUND_BLOCKS

    @pl.when(i >= window - 1)
    def _():
        first_round([i - (window - 1) + n for n in range(window)], i - window)

    @pl.when(i < window - 1)
    def _():
        first_round([i], i - 1)

    trips = trips_ref[0]

    @pl.when(trips > 0)
    def _():
        j_next = jnp.where(i >= window - 1, i - window, i - 1)

        def earlier(n, carry):
            block(j_next - n)
            return carry

        lax.fori_loop(0, trips, earlier, 0)
        for p in range(pairs):
            write_out(p, acc_ref[p], l_ref[p])


def _fox_prompt(q, k, v, c_rows, *, t, pairs):
    b, l, w = q.shape
    ngrp = w // (pairs * LANES)
    nk = l // t
    wide = pairs * LANES
    return pl.pallas_call(
        functools.partial(_fox_prompt_body, t=t, pairs=pairs),
        grid=(b, ngrp, l // t),
        in_specs=[
            pl.BlockSpec((1, t, wide), lambda bi, g, i: (bi, i, g)),
            pl.BlockSpec((1, l, wide), lambda bi, g, i: (bi, 0, g), pipeline_mode=pl.Buffered(1)),
            pl.BlockSpec((1, l, wide), lambda bi, g, i: (bi, 0, g), pipeline_mode=pl.Buffered(1)),
            pl.BlockSpec((1, 2 * pairs, nk, t), lambda bi, g, i: (bi, g, 0, 0)),
        ],
        out_specs=pl.BlockSpec((1, t, wide), lambda bi, g, i: (bi, i, g)),
        out_shape=jax.ShapeDtypeStruct((b, l, w), BF16),
        scratch_shapes=[pltpu.VMEM((pairs, 2 * t, LANES), F32), pltpu.VMEM((pairs, 2 * t, LANES), F32),
                        pltpu.VMEM((pairs, 2 * t, LANES), F32), pltpu.SMEM((2 * pairs,), F32),
                        pltpu.SMEM((1,), jnp.int32)],
        compiler_params=_cparams("parallel", "parallel", "arbitrary"),
        name="fox_prompt",
    )(q, k, v, c_rows)


def _sample_scan_body(clf_ref, lfr_ref, cc_ref, cnr_ref):
    cc_ref[...] = _scan(clf_ref[0], 1) * LOG2E
    cnr_ref[...] = _scan(lfr_ref[...], 1) * LOG2E


def _fox_sample_scan(clf, lf_row, *, layer):
    _, r, p_len = clf.shape
    full = lambda a: pl.BlockSpec(a.shape, lambda i: (0,) * a.ndim)
    return pl.pallas_call(
        _sample_scan_body,
        grid=(1,),
        in_specs=[pl.BlockSpec((1, r, p_len), lambda i: (layer, 0, 0)), full(lf_row)],
        out_specs=[pl.BlockSpec((r, p_len), lambda i: (0, 0)), full(lf_row)],
        out_shape=[jax.ShapeDtypeStruct((r, p_len), F32), jax.ShapeDtypeStruct(lf_row.shape, F32)],
        compiler_params=_cparams("arbitrary"),
        name="fox_sample_scan",
    )(clf, lf_row)


def _fox_sample_body(q_ref, k_ref, v_ref, ck_ref, cv_ref, cc_ref, cnr_ref, o_ref):
    t = q_ref.shape[1]
    nh, p_len = ck_ref.shape[2], ck_ref.shape[4]
    hd = FOX_HEAD_DIM
    lane = lax.broadcasted_iota(jnp.int32, (t, LANES), 1)
    row = lax.broadcasted_iota(jnp.int32, (t, t), 0)
    col = lax.broadcasted_iota(jnp.int32, (t, t), 1)
    c_cache = cc_ref[...]
    cn_row = cnr_ref[...]
    outs = []
    for h in range(nh):
        grp = slice((h // 2) * LANES, (h // 2 + 1) * LANES)
        lo = (h % 2) * hd
        qq = q_ref[0, :, grp]
        qm = jnp.where((lane >= lo) & (lane < lo + hd), qq, jnp.zeros_like(qq))
        q_h = qq[:, lo:lo + hd]
        kc_t = ck_ref[0, 0, h].astype(BF16)
        vc_t = cv_ref[0, 0, h].astype(BF16)
        c_h = c_cache[h:h + 1, :]
        tot = c_h[:, p_len - 1:p_len]
        s_c = _dot(q_h, kc_t) + (tot - c_h)
        s_s = _dot_nt(qm, k_ref[0, :, grp]) - cn_row[h:h + 1, :t]
        s_s = jnp.where(col <= row, s_s, -jnp.inf)
        m = jnp.maximum(jnp.max(s_c, axis=1, keepdims=True), jnp.max(s_s, axis=1, keepdims=True))
        p_c = jnp.exp2(s_c - m)
        p_s = jnp.exp2(s_s - m)
        den = jnp.sum(p_c, axis=1, keepdims=True) + jnp.sum(p_s, axis=1, keepdims=True)
        o = _dot_nt(p_c.astype(BF16), vc_t) + _dot(p_s.astype(BF16), v_ref[0, :, grp])[:, lo:lo + hd]
        outs.append(o / den)
    o_ref[0] = jnp.concatenate(outs, axis=1).astype(BF16)


def _fox_sample(q, k, v, cache_k, cache_v, c_cache, cn_row, *, layer):
    b, t, w = q.shape
    nh, hd, p_len = cache_k.shape[2:]
    tokq = pl.BlockSpec((1, t, w), lambda bi: (bi, 0, 0))
    cache = pl.BlockSpec((1, 1, nh, hd, p_len), lambda bi: (layer, bi, 0, 0, 0))
    return pl.pallas_call(
        _fox_sample_body,
        grid=(b,),
        in_specs=[tokq, tokq, tokq, cache, cache,
                  pl.BlockSpec((nh, p_len), lambda bi: (bi, 0)),
                  pl.BlockSpec((nh, LANES), lambda bi: (bi, 0))],
        out_specs=tokq,
        out_shape=jax.ShapeDtypeStruct((b, t, w), BF16),
        compiler_params=_cparams("parallel"),
        name="fox_sample",
    )(q, k, v, cache_k, cache_v, c_cache, cn_row)


def _hgrn_body(q_ref, lf_ref, k_ref, v_ref, s0_ref, o_ref, so_ref, st_ref, tmp_ref, *,
               c, n_chunks, sps, hps, zero_init):
    ti = pl.program_id(2)
    d = HGRN_HEAD_DIM
    chains = [(s, hh) for s in range(sps) for hh in range(hps)]
    slot = {ch: n for n, ch in enumerate(chains)}
    lanes_of = [slice(hh * d, (hh + 1) * d) for hh in range(hps)]

    @pl.when(ti == 0)
    def _():
        for s, hh in chains:
            st_ref[slot[(s, hh)]] = jnp.zeros((d, d), F32) if zero_init else s0_ref[0, s, hh].T

    row = lax.broadcasted_iota(jnp.int32, (c, c), 0)
    col = lax.broadcasted_iota(jnp.int32, (c, c), 1)
    rowc = lax.broadcasted_iota(jnp.int32, (c, 1), 0)
    half = c // 2
    b_all = [_scan(lf_ref[s], 0, period=c) for s in range(sps)]

    span = jnp.zeros((1, hps * d), F32)
    for s in range(sps):
        for ci in range(n_chunks):
            b = b_all[s][ci * c:(ci + 1) * c]
            b_mid = b[half - 1:half]
            span = jnp.maximum(span, jnp.maximum(b[0:1] - b_mid, b_mid - b[c - 1:c]))
    factorable = jnp.max(span) <= HGRN_FACTOR_LIMIT

    def carry_state(st, k, v, b):
        b_last = b[c - 1:c]
        kh = (k * jnp.exp(b_last - b)).astype(BF16)
        return st * jnp.exp(b_last) + _dot_tn(v, kh)

    @pl.when(factorable)
    def _():
        units = [(ch, ci) for ch in chains for ci in range(n_chunks)]
        tile_of = {u: (u[0][0], slice(u[1] * c, (u[1] + 1) * c), lanes_of[u[0][1]]) for u in units}
        sc, ds, qe, grow = {}, {}, {}, {}
        for u in units:
            s, rs, hs = tile_of[u]
            q = q_ref[s, rs, hs]
            k = k_ref[s, rs, hs]
            b = b_all[s][rs, hs]
            b_mid = b[half - 1:half]
            b_last = b[c - 1:c]
            qe[u] = (q * jnp.exp(b)).astype(BF16)
            grow[u] = jnp.exp(b_last)
            qt = (q * jnp.exp(b - b_mid)).astype(BF16)
            kt = (k * jnp.exp(b_mid - b)).astype(BF16)
            kh = (k * jnp.exp(b_last - b)).astype(BF16)
            sc[u] = _dot_nt(qt, kt)
            ds[u] = _dot_tn(v_ref[s, rs, hs], kh)
        st_in = {}
        for ch in chains:
            st = st_ref[slot[ch]]
            for ci in range(n_chunks):
                st_in[(ch, ci)] = st.astype(BF16)
                st = st * grow[(ch, ci)] + ds[(ch, ci)]
            st_ref[slot[ch]] = st
        for u in units:
            s, rs, hs = tile_of[u]
            intra = _dot(jnp.where(col <= row, sc[u], 0.0).astype(BF16), v_ref[s, rs, hs])
            o_ref[s, rs, hs] = _dot_nt(qe[u], st_in[u]) + intra

    @pl.when(jnp.logical_not(factorable))
    def _():
        for s, hh in chains:
            hs = lanes_of[hh]

            def chunk(ci, carry, s=s, hs=hs, sl=slot[(s, hh)]):
                rs = pl.ds(pl.multiple_of(ci * c, c), c)
                q = q_ref[s, rs, hs]
                k = k_ref[s, rs, hs]
                v = v_ref[s, rs, hs]
                b = _scan(lf_ref[s, rs, hs], 0)
                st = st_ref[sl]
                tmp_ref[0] = b
                tmp_ref[1] = k
                tmp_ref[2] = v.astype(F32)

                def key_row(r, acc):
                    bs = tmp_ref[0, pl.ds(r, 1), :]
                    ks = tmp_ref[1, pl.ds(r, 1), :]
                    vs = tmp_ref[2, pl.ds(r, 1), :]
                    w = jnp.exp(jnp.minimum(b - bs, 0.0))
                    a = jnp.sum(q * ks * w, axis=1, keepdims=True)
                    return acc + jnp.where(rowc >= r, a, 0.0) * vs

                o = _dot_nt((q * jnp.exp(b)).astype(BF16), st.astype(BF16))
                o_ref[s, rs, hs] = lax.fori_loop(0, c, key_row, o)
                st_ref[sl] = carry_state(st, k, v, b)
                return carry

            lax.fori_loop(0, n_chunks, chunk, 0)

    @pl.when(ti == pl.num_programs(2) - 1)
    def _():
        for s, hh in chains:
            so_ref[0, s, hh] = st_ref[slot[(s, hh)]].T


def _hgrn(hq, hlf, hk, hv, s0, carried, *, layer, c, tl, sps, hps):
    b, l, w = hq.shape
    d = HGRN_HEAD_DIM
    nh = w // d
    tok = pl.BlockSpec((sps, tl, hps * d), lambda bi, h, ti: (bi, ti, h))
    st = pl.BlockSpec((1, sps, hps, d, d), lambda bi, h, ti: (layer, bi, h, 0, 0))
    zero_init = s0 is None
    body = functools.partial(_hgrn_body, c=c, n_chunks=tl // c, sps=sps, hps=hps, zero_init=zero_init)
    args, in_specs = [hq, hlf, hk, hv], [tok, tok, tok, tok]
    if zero_init:
        core = body
        body = lambda q, lf, k, v, *rest: core(q, lf, k, v, None, *rest)
    else:
        args.append(s0)
        in_specs.append(st)
    aliases = {}
    if carried is not None:
        aliases[len(args)] = 1
        body = _drop_alias_refs(body, len(args), 1)
        args.append(carried)
        in_specs.append(_ANY)
    return pl.pallas_call(
        body,
        grid=(b // sps, nh // hps, l // tl),
        in_specs=in_specs,
        out_specs=[tok, st],
        out_shape=[jax.ShapeDtypeStruct((b, l, w), F32), jax.ShapeDtypeStruct((DEPTH, b, nh, d, d), F32)],
        scratch_shapes=[pltpu.VMEM((sps * hps, d, d), F32), pltpu.VMEM((3, c, d), F32)],
        input_output_aliases=aliases,
        compiler_params=_cparams("parallel", "parallel", "arbitrary"),
        name="hgrn",
    )(*args)


def kernel(x_prompt, x_sample, cache_k, cache_v, cache_logf, state_hgrn, norm_ffn1, ffn1_wi, ffn1_wo,
           norm_mix, w_in, b_fgate, hgrn_lb, hgrn_gnorm, w_out, norm_ffn2, ffn2_wi, ffn2_wo, norm_final):
    bp, lp, d = x_prompt.shape
    bs, ls, _ = x_sample.shape
    mp, ms = bp * lp, bs * ls
    p_len = cache_k.shape[3]
    f, w, h8 = FOX_WIDTH, HGRN_WIDTH, FOX_HEADS

    xp = x_prompt.reshape(mp, d)
    xs = x_sample.reshape(ms, d)
    gfin = norm_final.reshape(1, d)
    g1, g2, gm = (a.reshape(DEPTH, 1, d) for a in (norm_ffn1, norm_ffn2, norm_mix))
    gn = hgrn_gnorm.reshape(DEPTH, 1, HGRN_HEAD_DIM)
    clf = cache_logf.reshape(DEPTH, bs * h8, p_len)
    cache_kt = jnp.swapaxes(cache_k, 3, 4)
    cache_vt = jnp.swapaxes(cache_v, 3, 4)

    carry_p = carry_s = None
    st_p = st_s = None
    lfs = []
    for l in range(DEPTH):
        last = l == DEPTH - 1
        wl = w_in[l]
        wm = jnp.concatenate([wl[:, :3 * f], wl[:, 3 * f + h8:]], axis=1).astype(BF16)
        wf = jnp.pad(wl[:, 3 * f:3 * f + h8], ((0, 0), (0, LANES - h8))).astype(BF16)
        bfp = jnp.pad(b_fgate[l].reshape(1, h8), ((0, 0), (0, LANES - h8)))

        xs, wa, wb, wo = _ffn_stream(xs, g1, ffn1_wi, ffn1_wo, gfin, layer=l, final_norm=False, tf=TF_FFN)
        xp = _ffn(xp, g1, wa, wb, wo, gfin, layer=l, final_norm=False, tm=TM_FFN, tf=TF_FFN)

        (q, k, v, kh, vh, lfr, hq, hlf, hk, hv, hg) = _inproj(
            xp, gm, wm, wf, bfp, hgrn_lb, carry_p, layer=l, sample=False, batch=bp, tm=TM_PROJ)
        carry_p = (kh, vh, lfr)
        (q_s, k_s, v_s, kh_s, vh_s, lfr_s, hq_s, hlf_s, hk_s, hv_s, hg_s) = _inproj(
            xs, gm, wm, wf, bfp, hgrn_lb, carry_s, layer=l, sample=True, batch=bs, tm=ms)
        carry_s = (kh_s, vh_s)

        c_row = _fox_cumsum(lfr, layer=l)
        fo = _fox_prompt(q.reshape(bp, lp, f), k.reshape(bp, lp, f), v.reshape(bp, lp, f),
                         c_row.reshape(bp, h8, lp // T_ATTN, T_ATTN), t=T_ATTN, pairs=FOX_PAIRS)
        lfs_l = lfr_s.reshape(h8, bs, ls).transpose(1, 0, 2)
        lfs.append(lfs_l)
        lfs_pad = jnp.pad(lfs_l, ((0, 0), (0, 0), (0, LANES - ls))).reshape(bs * h8, LANES)
        c_cache, cn_row = _fox_sample_scan(clf, lfs_pad, layer=l)
        fo_s = _fox_sample(q_s.reshape(bs, ls, f), k_s.reshape(bs, ls, f), v_s.reshape(bs, ls, f),
                           cache_kt, cache_vt, c_cache, cn_row, layer=l)
        to3 = lambda a, b_, l_: a.reshape(b_, l_, w)
        ho, st_p = _hgrn(to3(hq, bp, lp), to3(hlf, bp, lp), to3(hk, bp, lp), to3(hv, bp, lp), None, st_p,
                         layer=l, c=C_HGRN, tl=TL_HGRN, sps=1, hps=HGRN_PROMPT_HEADS)
        ho_s, st_s = _hgrn(to3(hq_s, bs, ls), to3(hlf_s, bs, ls), to3(hk_s, bs, ls), to3(hv_s, bs, ls),
                           state_hgrn, st_s, layer=l, c=ls, tl=ls, sps=HGRN_SAMPLE_STREAMS, hps=HGRN_HEADS)

        xs, wa, wb, wo, wout = _ffn_stream(
            xs, g2, ffn2_wi, ffn2_wo, gfin, (fo_s.reshape(ms, f), ho_s.reshape(ms, w), hg_s, gn, w_out),
            layer=l, final_norm=last, tf=TF_FFN)
        xp = _ffn(xp, g2, wa, wb, wo, gfin, (fo.reshape(mp, f), ho.reshape(mp, w), hg, gn, wout),
                  layer=l, final_norm=last, tm=TM_FFN, tf=TF_FFN)

    k_prompt = jnp.swapaxes(carry_p[0], 3, 4)
    v_prompt = jnp.swapaxes(carry_p[1], 3, 4)
    return (xp.reshape(bp, lp, d), xs.reshape(bs, ls, d), k_prompt, v_prompt, carry_p[2], st_p,
            carry_s[0], carry_s[1], jnp.stack(lfs), st_s)
```

```python
import functools

import jax
import jax.numpy as jnp
from jax import lax
from jax.experimental import pallas as pl
from jax.experimental.pallas import tpu as pltpu

F32 = jnp.float32
BF16 = jnp.bfloat16

DEPTH = 2
FOX_HEADS = 8
FOX_HEAD_DIM = 64
FOX_WIDTH = FOX_HEADS * FOX_HEAD_DIM
HGRN_HEADS = 4
HGRN_HEAD_DIM = 128
HGRN_WIDTH = HGRN_HEADS * HGRN_HEAD_DIM
RMS_EPS = 1e-6
FOX_SCALE = FOX_HEAD_DIM ** -0.5
LOG2E = 1.4426950408889634

LANES = 128
VMEM_LIMIT_BYTES = 56 * 1024 * 1024
HGRN_FACTOR_LIMIT = 50.0
FOX_SKIP_GAP = 153.0
FOX_NORM_SLACK = 1.001
FOX_FIRST_ROUND_BLOCKS = 3

TM_FFN = 512
TM_FFN1 = 1024
TF_FFN = 256
TM_PROJ = 512
T_ATTN = 256
FOX_PAIRS = 4
C_HGRN = 64
TL_HGRN = 1024
HGRN_PROMPT_HEADS = 2
HGRN_SAMPLE_STREAMS = 4
KEY_NORM_ROWS = 1024


def _cparams(*sem):
    return pltpu.CompilerParams(dimension_semantics=sem, vmem_limit_bytes=VMEM_LIMIT_BYTES)


def _rms(x, g):
    ms = jnp.mean(x * x, axis=-1, keepdims=True)
    return x * lax.rsqrt(ms + RMS_EPS) * g


def _silu(x):
    return x * jax.nn.sigmoid(x)


def _log1p_exp_neg(d):
    return jnp.log(1.0 + jnp.exp(-d))


def _log_sigmoid(z):
    return jnp.minimum(z, 0.0) - _log1p_exp_neg(jnp.abs(z))


def _gate_terms(z):
    e = jnp.exp(-jnp.abs(z))
    t = 1.0 + e
    return jnp.minimum(z, 0.0) - jnp.log(t), jnp.where(z >= 0.0, e, 1.0) / t


def _dot(a, b):
    return jnp.dot(a, b, preferred_element_type=F32)


def _dot_nt(a, b):
    return lax.dot_general(a, b, (((1,), (1,)), ((), ())), preferred_element_type=F32)


def _dot_tn(a, b):
    return lax.dot_general(a, b, (((0,), (0,)), ((), ())), preferred_element_type=F32)


def _scan(x, axis, period=None):
    n = x.shape[axis] if period is None else period
    idx = lax.broadcasted_iota(jnp.int32, x.shape, axis)
    if period is not None:
        idx = idx & (period - 1)
    s = 1
    while s < n:
        x = x + jnp.where(idx >= s, pltpu.roll(x, s, axis), 0.0)
        s *= 2
    return x


def _drop_alias_refs(body, n_in, n_alias):
    def wrapped(*refs):
        return body(*refs[:n_in], *refs[n_in + n_alias:])
    return wrapped


_ANY = pl.BlockSpec(memory_space=pl.ANY)


def _mix(fo_ref, ho_ref, hg_ref, gn_ref):
    gn = gn_ref[0]
    d = HGRN_HEAD_DIM
    parts = [fo_ref[...]]
    for h in range(HGRN_HEADS):
        ho = ho_ref[:, h * d:(h + 1) * d]
        parts.append((_rms(ho, gn) * _silu(hg_ref[:, h * d:(h + 1) * d])).astype(BF16))
    return jnp.concatenate(parts, axis=1)


def _ffn_body(*refs, final_norm, with_outproj, tf):
    if with_outproj:
        (x_ref, fo_ref, ho_ref, hg_ref, gn_ref, wout_ref,
         g_ref, wa_ref, wb_ref, wo_ref, gf_ref, o_ref, acc_ref) = refs
        x = x_ref[...] + _dot(_mix(fo_ref, ho_ref, hg_ref, gn_ref), wout_ref[...])
    else:
        x_ref, g_ref, wa_ref, wb_ref, wo_ref, gf_ref, o_ref, acc_ref = refs
        x = x_ref[...]
    f = wo_ref.shape[0]
    xn = _rms(x, g_ref[0]).astype(BF16)
    for c in range(f // tf):
        a = _dot(xn, wa_ref[:, c * tf:(c + 1) * tf])
        b = _dot(xn, wb_ref[:, c * tf:(c + 1) * tf])
        part = _dot((_silu(a) * b).astype(BF16), wo_ref[c * tf:(c + 1) * tf, :])
        if c == 0:
            acc_ref[...] = part
        else:
            acc_ref[...] += part
    y = x + 0.5 * acc_ref[...]
    if final_norm:
        y = _rms(y, gf_ref[...])
    o_ref[...] = y


def _ffn(x, g, wa, wb, wo, gf, outproj=None, *, layer, final_norm, tm, tf):
    m, d = x.shape
    tok = lambda width: pl.BlockSpec((tm, width), lambda i: (i, 0))
    held = lambda a: pl.BlockSpec(a.shape, lambda i: (0,) * a.ndim, pipeline_mode=pl.Buffered(1))
    args, in_specs = [x], [tok(d)]
    if outproj is not None:
        fo, ho, hg, gn, wout = outproj
        args += [fo, ho, hg, gn, wout]
        in_specs += [tok(fo.shape[1]), tok(ho.shape[1]), tok(hg.shape[1]),
                     pl.BlockSpec((1, 1, gn.shape[2]), lambda i: (layer, 0, 0)), held(wout)]
    args += [g, wa, wb, wo, gf]
    in_specs += [pl.BlockSpec((1, 1, d), lambda i: (layer, 0, 0)), held(wa), held(wb), held(wo),
                 pl.BlockSpec((1, d), lambda i: (0, 0))]
    return pl.pallas_call(
        functools.partial(_ffn_body, final_norm=final_norm, with_outproj=outproj is not None, tf=tf),
        grid=(m // tm,),
        in_specs=in_specs,
        out_specs=tok(d),
        out_shape=jax.ShapeDtypeStruct((m, d), F32),
        scratch_shapes=[pltpu.VMEM((tm, d), F32)],
        compiler_params=_cparams("parallel"),
        name="ffn",
    )(*args)


def _ffn_stream_body(*refs, final_norm, with_outproj):
    if with_outproj:
        (x_ref, fo_ref, ho_ref, hg_ref, gn_ref, wout_ref, g_ref, wa_ref, wb_ref, wo_ref, gf_ref,
         o_ref, wab_ref, wbb_ref, wob_ref, woutb_ref, xr_ref, xn_ref, acc_ref) = refs
    else:
        (x_ref, g_ref, wa_ref, wb_ref, wo_ref, gf_ref,
         o_ref, wab_ref, wbb_ref, wob_ref, xr_ref, xn_ref, acc_ref) = refs
    j = pl.program_id(0)

    @pl.when(j == 0)
    def _():
        x = x_ref[...]
        if with_outproj:
            woutb_ref[...] = wout_ref[0].astype(BF16)
            x = x + _dot(_mix(fo_ref, ho_ref, hg_ref, gn_ref), woutb_ref[...])
        xr_ref[...] = x
        xn_ref[...] = _rms(x, g_ref[0]).astype(BF16)
        acc_ref[...] = jnp.zeros_like(acc_ref)

    wab_ref[...] = wa_ref[0].astype(BF16)
    wbb_ref[...] = wb_ref[0].astype(BF16)
    wob_ref[...] = wo_ref[0].astype(BF16)
    xn = xn_ref[...]
    h = _silu(_dot(xn, wab_ref[...])) * _dot(xn, wbb_ref[...])
    acc_ref[...] += _dot(h.astype(BF16), wob_ref[...])

    @pl.when(j == pl.num_programs(0) - 1)
    def _():
        y = xr_ref[...] + 0.5 * acc_ref[...]
        if final_norm:
            y = _rms(y, gf_ref[...])
        o_ref[...] = y


def _ffn_stream(x, g, wi, wo, gf, outproj=None, *, layer, final_norm, tf):
    m, d = x.shape
    f = wo.shape[1]
    nf = f // tf
    whole = lambda rows, width: pl.BlockSpec((rows, width), lambda j: (0, 0))
    args, in_specs = [x], [whole(m, d)]
    out_specs = [whole(m, d), pl.BlockSpec((d, tf), lambda j: (0, j)), pl.BlockSpec((d, tf), lambda j: (0, j)),
                 pl.BlockSpec((tf, d), lambda j: (j, 0))]
    out_shape = [jax.ShapeDtypeStruct((m, d), F32), jax.ShapeDtypeStruct((d, f), BF16),
                 jax.ShapeDtypeStruct((d, f), BF16), jax.ShapeDtypeStruct((f, d), BF16)]
    if outproj is not None:
        fo, ho, hg, gn, wout = outproj
        dm = wout.shape[1]
        args += [fo, ho, hg, gn, wout]
        in_specs += [whole(m, fo.shape[1]), whole(m, ho.shape[1]), whole(m, hg.shape[1]),
                     pl.BlockSpec((1, 1, gn.shape[2]), lambda j: (layer, 0, 0)),
                     pl.BlockSpec((1, dm, d), lambda j: (layer, 0, 0))]
        out_specs.append(whole(dm, d))
        out_shape.append(jax.ShapeDtypeStruct((dm, d), BF16))
    args += [g, wi, wi, wo, gf]
    in_specs += [pl.BlockSpec((1, 1, d), lambda j: (layer, 0, 0)),
                 pl.BlockSpec((1, d, tf), lambda j: (layer, 0, j)),
                 pl.BlockSpec((1, d, tf), lambda j: (layer, 0, j + nf)),
                 pl.BlockSpec((1, tf, d), lambda j: (layer, j, 0)),
                 whole(1, d)]
    return pl.pallas_call(
        functools.partial(_ffn_stream_body, final_norm=final_norm, with_outproj=outproj is not None),
        grid=(nf,),
        in_specs=in_specs,
        out_specs=out_specs,
        out_shape=out_shape,
        scratch_shapes=[pltpu.VMEM((m, d), F32), pltpu.VMEM((m, d), BF16), pltpu.VMEM((m, d), F32)],
        compiler_params=_cparams("arbitrary"),
        name="ffn_stream",
    )(*args)


def _inproj_body(x_ref, g_ref, wm_ref, wf_ref, bf_ref, lbp_ref,
                 q_ref, k_ref, v_ref, kh_ref, vh_ref, lfr_ref,
                 hq_ref, hlf_ref, hk_ref, hv_ref, hg_ref, *, layer, sample):
    h = _rms(x_ref[...], g_ref[0]).astype(BF16)
    w, hd = FOX_WIDTH, FOX_HEAD_DIM
    wc = 2 * LANES
    heads_per_chunk = wc // hd

    def chunks(i):
        for n in range(w // wc):
            yield n, slice(n * wc, (n + 1) * wc), _dot(h, wm_ref[:, i * w + n * wc:i * w + (n + 1) * wc])

    def store_heads(ref, n, val):
        h0 = n * heads_per_chunk
        if sample:
            for hh in range(heads_per_chunk):
                piece = val[:, hh * hd:(hh + 1) * hd]
                ref[0, :, h0 + hh] = piece.reshape(ref.shape[1], ref.shape[3], hd)
        else:
            ref[0, 0, h0:h0 + heads_per_chunk] = val.T.reshape(heads_per_chunk, hd, val.shape[0])

    p = lbp_ref[...]
    e = jnp.exp(p - jnp.max(p, axis=0, keepdims=True))
    sm = e / jnp.sum(e, axis=0, keepdims=True)
    cs = sm[0:1]
    for r in range(1, layer + 1):
        cs = cs + sm[r:r + 1]
    lb = cs - sm[0:1]
    log_lb = jnp.log(lb)
    log_1m_lb = jnp.log1p(-lb)

    for n, cols, z in chunks(4):
        log_sig, sig_neg = _gate_terms(z)
        a = log_lb[:, cols]
        b = log_1m_lb[:, cols] + log_sig
        hlf_ref[:, cols] = jnp.maximum(a, b) + _log1p_exp_neg(jnp.abs(a - b))
        hk_ref[:, cols] = (1.0 - lb[:, cols]) * sig_neg
    for n, cols, hq in chunks(3):
        hq_ref[:, cols] = _silu(hq)

    lf = _log_sigmoid(_dot(h, wf_ref[...]) + bf_ref[...])
    lfr = lf.T[:FOX_HEADS, :]
    if sample:
        lfr_ref[...] = lfr
    else:
        lfr_ref[0, 0] = lfr

    for n, cols, q in chunks(0):
        q_ref[:, cols] = (q * (FOX_SCALE * LOG2E)).astype(BF16)
    for n, cols, k in chunks(1):
        store_heads(kh_ref, n, k)
        k_ref[:, cols] = k.astype(BF16)
    for n, cols, v in chunks(2):
        store_heads(vh_ref, n, v)
        v_ref[:, cols] = v.astype(BF16)
    for n, cols, hv in chunks(5):
        hv_ref[:, cols] = hv.astype(BF16)
    for n, cols, hg in chunks(6):
        hg_ref[:, cols] = hg


def _inproj(x, g, wm, wf, bfp, lbp, carried, *, layer, sample, batch, tm):
    m, d = x.shape
    w, hd, nh = FOX_WIDTH, FOX_HEAD_DIM, FOX_HEADS
    seq = m // batch
    tiles_per_seq = seq // tm if not sample else 1
    tok = lambda width: pl.BlockSpec((tm, width), lambda i: (i, 0))
    full = lambda a: pl.BlockSpec(a.shape, lambda i: (0,) * a.ndim)
    sds = jax.ShapeDtypeStruct
    if sample:
        heads_shape = (DEPTH, batch, nh, seq, hd)
        heads_spec = pl.BlockSpec((1, batch, nh, seq, hd), lambda i: (layer, 0, 0, 0, 0))
        lfr_shape, lfr_spec = (nh, m), pl.BlockSpec((nh, m), lambda i: (0, 0))
    else:
        heads_shape = (DEPTH, batch, nh, hd, seq)
        heads_spec = pl.BlockSpec((1, 1, nh, hd, tm),
                                  lambda i: (layer, i // tiles_per_seq, 0, 0, i % tiles_per_seq))
        lfr_shape = (DEPTH, batch, nh, seq)
        lfr_spec = pl.BlockSpec((1, 1, nh, tm), lambda i: (layer, i // tiles_per_seq, 0, i % tiles_per_seq))
    out_specs = [tok(w), tok(w), tok(w), heads_spec, heads_spec, lfr_spec,
                 tok(w), tok(w), tok(w), tok(w), tok(w)]
    out_shape = [sds((m, w), BF16), sds((m, w), BF16), sds((m, w), BF16),
                 sds(heads_shape, F32), sds(heads_shape, F32), sds(lfr_shape, F32),
                 sds((m, w), F32), sds((m, w), F32), sds((m, w), F32), sds((m, w), BF16), sds((m, w), F32)]
    body = functools.partial(_inproj_body, layer=layer, sample=sample)
    args = [x, g, wm, wf, bfp, lbp]
    in_specs = [tok(d), pl.BlockSpec((1, 1, d), lambda i: (layer, 0, 0)), full(wm), full(wf), full(bfp), full(lbp)]
    aliases = {}
    if carried is not None:
        out_index = (3, 4, 5)
        for n, buf in enumerate(carried):
            aliases[len(args)] = out_index[n]
            args.append(buf)
            in_specs.append(_ANY)
        body = _drop_alias_refs(body, 6, len(carried))
    return pl.pallas_call(
        body, grid=(m // tm,), in_specs=in_specs, out_specs=out_specs, out_shape=out_shape,
        input_output_aliases=aliases, compiler_params=_cparams("arbitrary"), name="inproj",
    )(*args)


def _cumsum_body(lfr_ref, cr_ref):
    cr_ref[0] = _scan(lfr_ref[0, 0], 1) * LOG2E


def _fox_cumsum(lf_rows, *, layer):
    _, b, h, l = lf_rows.shape
    return pl.pallas_call(
        _cumsum_body,
        grid=(b,),
        in_specs=[pl.BlockSpec((1, 1, h, l), lambda i: (layer, i, 0, 0))],
        out_specs=pl.BlockSpec((1, h, l), lambda i: (i, 0, 0)),
        out_shape=jax.ShapeDtypeStruct((b, h, l), F32),
        compiler_params=_cparams("parallel"),
        name="fox_cumsum",
    )(lf_rows)


def _fox_prompt_body(q_ref, k_ref, v_ref, cr_ref, o_ref, m_ref, l_ref, acc_ref, kn_ref, trips_ref, *, t, pairs):
    i = pl.program_id(2)
    nk = cr_ref.shape[2]
    hd = FOX_HEAD_DIM
    lane = lax.broadcasted_iota(jnp.int32, (t, LANES), 1)
    row = lax.broadcasted_iota(jnp.int32, (t, t), 0)
    col = lax.broadcasted_iota(jnp.int32, (t, t), 1)
    head_lanes = [lane < hd, lane >= hd]
    lanes_of = [slice(p * LANES, (p + 1) * LANES) for p in range(pairs)]

    @pl.when(i == 0)
    def _():
        step = KEY_NORM_ROWS
        lane_s = lax.broadcasted_iota(jnp.int32, (step, LANES), 1)
        for p in range(pairs):
            def chunk(ci, mx, p=p):
                kf = k_ref[0, pl.ds(pl.multiple_of(ci * step, step), step), lanes_of[p]].astype(F32)
                k2 = kf * kf
                n0 = jnp.max(jnp.sum(jnp.where(lane_s < hd, k2, 0.0), axis=1, keepdims=True))
                n1 = jnp.max(jnp.sum(jnp.where(lane_s >= hd, k2, 0.0), axis=1, keepdims=True))
                return jnp.maximum(mx[0], n0), jnp.maximum(mx[1], n1)

            n0, n1 = lax.fori_loop(0, k_ref.shape[1] // step, chunk, (jnp.float32(0.0), jnp.float32(0.0)))
            kn_ref[2 * p] = jnp.sqrt(n0)
            kn_ref[2 * p + 1] = jnp.sqrt(n1)

    causal = jnp.concatenate([col <= row, col <= row], axis=0)
    qs, qn = [], []
    for p in range(pairs):
        qq = q_ref[0, :, lanes_of[p]]
        zero = jnp.zeros_like(qq)
        qs.append(jnp.concatenate([jnp.where(head_lanes[h], qq, zero) for h in range(2)], axis=0))
        q2 = qq.astype(F32) * qq.astype(F32)
        qn.append(jnp.concatenate(
            [jnp.sqrt(jnp.sum(jnp.where(head_lanes[h], q2, 0.0), axis=1, keepdims=True)) for h in range(2)],
            axis=0))

    def scores(p, j, diag):
        kk = k_ref[0, pl.ds(pl.multiple_of(j * t, t), t), lanes_of[p]]
        s = _dot_nt(qs[p], kk)
        s = jnp.concatenate(
            [s[h * t:(h + 1) * t] - cr_ref[0, 2 * p + h, pl.ds(j, 1), :] for h in range(2)], axis=0)
        return jnp.where(causal, s, -jnp.inf) if diag else s

    def values(p, j):
        return v_ref[0, pl.ds(pl.multiple_of(j * t, t), t), lanes_of[p]]

    def write_out(p, acc, l):
        o = acc / l
        o_ref[0, :, lanes_of[p]] = jnp.where(head_lanes[0], o[:t], o[t:]).astype(BF16)

    def first_round(js, j_next):
        def lane_groups(pr):
            return functools.reduce(jnp.add, [pr[:, g * LANES:(g + 1) * LANES] for g in range(t // LANES)])

        state = []
        for p in range(pairs):
            m = l = acc = None
            for n, j in enumerate(js):
                s = scores(p, j, n == len(js) - 1)
                m_blk = jnp.max(s, axis=1, keepdims=True)
                if m is None:
                    m_next = m_blk
                    pr = jnp.exp2(s - m_next)
                    l = lane_groups(pr)
                    acc = _dot(pr.astype(BF16), values(p, j))
                else:
                    m_next = jnp.maximum(m, m_blk)
                    alpha = jnp.exp2(m - m_next)
                    pr = jnp.exp2(s - m_next)
                    l = alpha * l + lane_groups(pr)
                    acc = alpha * acc + _dot(pr.astype(BF16), values(p, j))
                m = m_next
            state.append((m, jnp.sum(l, axis=1, keepdims=True), acc))

        jidx = lax.broadcasted_iota(jnp.int32, (nk, 1), 0)
        first_needed = jnp.full((nk, 1), nk, jnp.int32)
        for p in range(pairs):
            m = state[p][0]
            for h in range(2):
                rows = slice(h * t, (h + 1) * t)
                gap = qn[p][rows] * (kn_ref[2 * p + h] * FOX_NORM_SLACK) - m[rows]
                reach = jnp.max(gap, axis=0, keepdims=True)
                c_min = jnp.min(cr_ref[0, 2 * p + h], axis=1, keepdims=True)
                first_needed = jnp.minimum(first_needed, jnp.where(reach - c_min >= -FOX_SKIP_GAP, jidx, nk))
        trips = j_next + 1 - jnp.minimum(jnp.min(first_needed), j_next + 1)
        trips_ref[0] = trips

        @pl.when(trips == 0)
        def _():
            for p in range(pairs):
                write_out(p, state[p][2], state[p][1])

        @pl.when(trips > 0)
        def _():
            for p in range(pairs):
                m, l, acc = state[p]
                m_ref[p] = jnp.broadcast_to(m, m_ref.shape[1:])
                l_ref[p] = jnp.broadcast_to(l, l_ref.shape[1:])
                acc_ref[p] = acc

    def block(j):
        for p in range(pairs):
            s = scores(p, j, False)
            m_prev = m_ref[p]
            m_next = jnp.maximum(m_prev, jnp.max(s, axis=1, keepdims=True))
            pr = jnp.exp2(s - jnp.concatenate([m_next] * (t // LANES), axis=1))
            alpha = jnp.exp2(m_prev - m_next)
            l_ref[p] = alpha * l_ref[p] + jnp.sum(pr, axis=1, keepdims=True)
            m_ref[p] = m_next
            acc_ref[p] = alpha * acc_ref[p] + _dot(pr.astype(BF16), values(p, j))

    window = FOX_FIRST_ROUND_BLOCKS

    @pl.when(i >= window - 1)
    def _():
        first_round([i - (window - 1) + n for n in range(window)], i - window)

    @pl.when(i < window - 1)
    def _():
        first_round([i], i - 1)

    trips = trips_ref[0]

    @pl.when(trips > 0)
    def _():
        j_next = jnp.where(i >= window - 1, i - window, i - 1)

        def earlier(n, carry):
            block(j_next - n)
            return carry

        lax.fori_loop(0, trips, earlier, 0)
        for p in range(pairs):
            write_out(p, acc_ref[p], l_ref[p])


def _fox_prompt(q, k, v, c_rows, *, t, pairs):
    b, l, w = q.shape
    ngrp = w // (pairs * LANES)
    nk = l // t
    wide = pairs * LANES
    return pl.pallas_call(
        functools.partial(_fox_prompt_body, t=t, pairs=pairs),
        grid=(b, ngrp, l // t),
        in_specs=[
            pl.BlockSpec((1, t, wide), lambda bi, g, i: (bi, i, g)),
            pl.BlockSpec((1, l, wide), lambda bi, g, i: (bi, 0, g), pipeline_mode=pl.Buffered(1)),
            pl.BlockSpec((1, l, wide), lambda bi, g, i: (bi, 0, g), pipeline_mode=pl.Buffered(1)),
            pl.BlockSpec((1, 2 * pairs, nk, t), lambda bi, g, i: (bi, g, 0, 0)),
        ],
        out_specs=pl.BlockSpec((1, t, wide), lambda bi, g, i: (bi, i, g)),
        out_shape=jax.ShapeDtypeStruct((b, l, w), BF16),
        scratch_shapes=[pltpu.VMEM((pairs, 2 * t, LANES), F32), pltpu.VMEM((pairs, 2 * t, LANES), F32),
                        pltpu.VMEM((pairs, 2 * t, LANES), F32), pltpu.SMEM((2 * pairs,), F32),
                        pltpu.SMEM((1,), jnp.int32)],
        compiler_params=_cparams("parallel", "parallel", "arbitrary"),
        name="fox_prompt",
    )(q, k, v, c_rows)


def _sample_scan_body(clf_ref, lfr_ref, cc_ref, cnr_ref):
    cc_ref[...] = _scan(clf_ref[0], 1) * LOG2E
    cnr_ref[...] = _scan(lfr_ref[...], 1) * LOG2E


def _fox_sample_scan(clf, lf_row, *, layer):
    _, r, p_len = clf.shape
    full = lambda a: pl.BlockSpec(a.shape, lambda i: (0,) * a.ndim)
    return pl.pallas_call(
        _sample_scan_body,
        grid=(1,),
        in_specs=[pl.BlockSpec((1, r, p_len), lambda i: (layer, 0, 0)), full(lf_row)],
        out_specs=[pl.BlockSpec((r, p_len), lambda i: (0, 0)), full(lf_row)],
        out_shape=[jax.ShapeDtypeStruct((r, p_len), F32), jax.ShapeDtypeStruct(lf_row.shape, F32)],
        compiler_params=_cparams("arbitrary"),
        name="fox_sample_scan",
    )(clf, lf_row)


def _fox_sample_body(q_ref, k_ref, v_ref, ck_ref, cv_ref, cc_ref, cnr_ref, o_ref):
    t = q_ref.shape[1]
    nh, p_len = ck_ref.shape[2], ck_ref.shape[4]
    hd = FOX_HEAD_DIM
    lane = lax.broadcasted_iota(jnp.int32, (t, LANES), 1)
    row = lax.broadcasted_iota(jnp.int32, (t, t), 0)
    col = lax.broadcasted_iota(jnp.int32, (t, t), 1)
    c_cache = cc_ref[...]
    cn_row = cnr_ref[...]
    outs = []
    for h in range(nh):
        grp = slice((h // 2) * LANES, (h // 2 + 1) * LANES)
        lo = (h % 2) * hd
        qq = q_ref[0, :, grp]
        qm = jnp.where((lane >= lo) & (lane < lo + hd), qq, jnp.zeros_like(qq))
        q_h = qq[:, lo:lo + hd]
        kc_t = ck_ref[0, 0, h].astype(BF16)
        vc_t = cv_ref[0, 0, h].astype(BF16)
        c_h = c_cache[h:h + 1, :]
        tot = c_h[:, p_len - 1:p_len]
        s_c = _dot(q_h, kc_t) + (tot - c_h)
        s_s = _dot_nt(qm, k_ref[0, :, grp]) - cn_row[h:h + 1, :t]
        s_s = jnp.where(col <= row, s_s, -jnp.inf)
        m = jnp.maximum(jnp.max(s_c, axis=1, keepdims=True), jnp.max(s_s, axis=1, keepdims=True))
        p_c = jnp.exp2(s_c - m)
        p_s = jnp.exp2(s_s - m)
        den = jnp.sum(p_c, axis=1, keepdims=True) + jnp.sum(p_s, axis=1, keepdims=True)
        o = _dot_nt(p_c.astype(BF16), vc_t) + _dot(p_s.astype(BF16), v_ref[0, :, grp])[:, lo:lo + hd]
        outs.append(o / den)
    o_ref[0] = jnp.concatenate(outs, axis=1).astype(BF16)


def _fox_sample(q, k, v, cache_k, cache_v, c_cache, cn_row, *, layer):
    b, t, w = q.shape
    nh, hd, p_len = cache_k.shape[2:]
    tokq = pl.BlockSpec((1, t, w), lambda bi: (bi, 0, 0))
    cache = pl.BlockSpec((1, 1, nh, hd, p_len), lambda bi: (layer, bi, 0, 0, 0))
    return pl.pallas_call(
        _fox_sample_body,
        grid=(b,),
        in_specs=[tokq, tokq, tokq, cache, cache,
                  pl.BlockSpec((nh, p_len), lambda bi: (bi, 0)),
                  pl.BlockSpec((nh, LANES), lambda bi: (bi, 0))],
        out_specs=tokq,
        out_shape=jax.ShapeDtypeStruct((b, t, w), BF16),
        compiler_params=_cparams("parallel"),
        name="fox_sample",
    )(q, k, v, cache_k, cache_v, c_cache, cn_row)


def _hgrn_body(q_ref, lf_ref, k_ref, v_ref, s0_ref, o_ref, so_ref, st_ref, tmp_ref, *,
               c, n_chunks, sps, hps, zero_init):
    ti = pl.program_id(2)
    d = HGRN_HEAD_DIM
    chains = [(s, hh) for s in range(sps) for hh in range(hps)]
    slot = {ch: n for n, ch in enumerate(chains)}
    lanes_of = [slice(hh * d, (hh + 1) * d) for hh in range(hps)]

    @pl.when(ti == 0)
    def _():
        for s, hh in chains:
            st_ref[slot[(s, hh)]] = jnp.zeros((d, d), F32) if zero_init else s0_ref[0, s, hh].T

    row = lax.broadcasted_iota(jnp.int32, (c, c), 0)
    col = lax.broadcasted_iota(jnp.int32, (c, c), 1)
    rowc = lax.broadcasted_iota(jnp.int32, (c, 1), 0)
    half = c // 2
    b_all = [_scan(lf_ref[s], 0, period=c) for s in range(sps)]

    span = jnp.zeros((1, hps * d), F32)
    for s in range(sps):
        for ci in range(n_chunks):
            b = b_all[s][ci * c:(ci + 1) * c]
            b_mid = b[half - 1:half]
            span = jnp.maximum(span, jnp.maximum(b[0:1] - b_mid, b_mid - b[c - 1:c]))
    factorable = jnp.max(span) <= HGRN_FACTOR_LIMIT

    def carry_state(st, k, v, b):
        b_last = b[c - 1:c]
        kh = (k * jnp.exp(b_last - b)).astype(BF16)
        return st * jnp.exp(b_last) + _dot_tn(v, kh)

    @pl.when(factorable)
    def _():
        units = [(ch, ci) for ch in chains for ci in range(n_chunks)]
        tile_of = {u: (u[0][0], slice(u[1] * c, (u[1] + 1) * c), lanes_of[u[0][1]]) for u in units}
        sc, ds, qe, grow = {}, {}, {}, {}
        for u in units:
            s, rs, hs = tile_of[u]
            q = q_ref[s, rs, hs]
            k = k_ref[s, rs, hs]
            b = b_all[s][rs, hs]
            b_mid = b[half - 1:half]
            b_last = b[c - 1:c]
            qe[u] = (q * jnp.exp(b)).astype(BF16)
            grow[u] = jnp.exp(b_last)
            qt = (q * jnp.exp(b - b_mid)).astype(BF16)
            kt = (k * jnp.exp(b_mid - b)).astype(BF16)
            kh = (k * jnp.exp(b_last - b)).astype(BF16)
            sc[u] = _dot_nt(qt, kt)
            ds[u] = _dot_tn(v_ref[s, rs, hs], kh)
        st_in = {}
        for ch in chains:
            st = st_ref[slot[ch]]
            for ci in range(n_chunks):
                st_in[(ch, ci)] = st.astype(BF16)
                st = st * grow[(ch, ci)] + ds[(ch, ci)]
            st_ref[slot[ch]] = st
        for u in units:
            s, rs, hs = tile_of[u]
            intra = _dot(jnp.where(col <= row, sc[u], 0.0).astype(BF16), v_ref[s, rs, hs])
            o_ref[s, rs, hs] = _dot_nt(qe[u], st_in[u]) + intra

    @pl.when(jnp.logical_not(factorable))
    def _():
        for s, hh in chains:
            hs = lanes_of[hh]

            def chunk(ci, carry, s=s, hs=hs, sl=slot[(s, hh)]):
                rs = pl.ds(pl.multiple_of(ci * c, c), c)
                q = q_ref[s, rs, hs]
                k = k_ref[s, rs, hs]
                v = v_ref[s, rs, hs]
                b = _scan(lf_ref[s, rs, hs], 0)
                st = st_ref[sl]
                tmp_ref[0] = b
                tmp_ref[1] = k
                tmp_ref[2] = v.astype(F32)

                def key_row(r, acc):
                    bs = tmp_ref[0, pl.ds(r, 1), :]
                    ks = tmp_ref[1, pl.ds(r, 1), :]
                    vs = tmp_ref[2, pl.ds(r, 1), :]
                    w = jnp.exp(jnp.minimum(b - bs, 0.0))
                    a = jnp.sum(q * ks * w, axis=1, keepdims=True)
                    return acc + jnp.where(rowc >= r, a, 0.0) * vs

                o = _dot_nt((q * jnp.exp(b)).astype(BF16), st.astype(BF16))
                o_ref[s, rs, hs] = lax.fori_loop(0, c, key_row, o)
                st_ref[sl] = carry_state(st, k, v, b)
                return carry

            lax.fori_loop(0, n_chunks, chunk, 0)

    @pl.when(ti == pl.num_programs(2) - 1)
    def _():
        for s, hh in chains:
            so_ref[0, s, hh] = st_ref[slot[(s, hh)]].T


def _hgrn(hq, hlf, hk, hv, s0, carried, *, layer, c, tl, sps, hps):
    b, l, w = hq.shape
    d = HGRN_HEAD_DIM
    nh = w // d
    tok = pl.BlockSpec((sps, tl, hps * d), lambda bi, h, ti: (bi, ti, h))
    st = pl.BlockSpec((1, sps, hps, d, d), lambda bi, h, ti: (layer, bi, h, 0, 0))
    zero_init = s0 is None
    body = functools.partial(_hgrn_body, c=c, n_chunks=tl // c, sps=sps, hps=hps, zero_init=zero_init)
    args, in_specs = [hq, hlf, hk, hv], [tok, tok, tok, tok]
    if zero_init:
        core = body
        body = lambda q, lf, k, v, *rest: core(q, lf, k, v, None, *rest)
    else:
        args.append(s0)
        in_specs.append(st)
    aliases = {}
    if carried is not None:
        aliases[len(args)] = 1
        body = _drop_alias_refs(body, len(args), 1)
        args.append(carried)
        in_specs.append(_ANY)
    return pl.pallas_call(
        body,
        grid=(b // sps, nh // hps, l // tl),
        in_specs=in_specs,
        out_specs=[tok, st],
        out_shape=[jax.ShapeDtypeStruct((b, l, w), F32), jax.ShapeDtypeStruct((DEPTH, b, nh, d, d), F32)],
        scratch_shapes=[pltpu.VMEM((sps * hps, d, d), F32), pltpu.VMEM((3, c, d), F32)],
        input_output_aliases=aliases,
        compiler_params=_cparams("parallel", "parallel", "arbitrary"),
        name="hgrn",
    )(*args)


def kernel(x_prompt, x_sample, cache_k, cache_v, cache_logf, state_hgrn, norm_ffn1, ffn1_wi, ffn1_wo,
           norm_mix, w_in, b_fgate, hgrn_lb, hgrn_gnorm, w_out, norm_ffn2, ffn2_wi, ffn2_wo, norm_final):
    bp, lp, d = x_prompt.shape
    bs, ls, _ = x_sample.shape
    mp, ms = bp * lp, bs * ls
    p_len = cache_k.shape[3]
    f, w, h8 = FOX_WIDTH, HGRN_WIDTH, FOX_HEADS

    xp = x_prompt.reshape(mp, d)
    xs = x_sample.reshape(ms, d)
    gfin = norm_final.reshape(1, d)
    g1, g2, gm = (a.reshape(DEPTH, 1, d) for a in (norm_ffn1, norm_ffn2, norm_mix))
    gn = hgrn_gnorm.reshape(DEPTH, 1, HGRN_HEAD_DIM)
    clf = cache_logf.reshape(DEPTH, bs * h8, p_len)
    cache_kt = jnp.swapaxes(cache_k, 3, 4)
    cache_vt = jnp.swapaxes(cache_v, 3, 4)

    carry_p = carry_s = None
    st_p = st_s = None
    lfs = []
    for l in range(DEPTH):
        last = l == DEPTH - 1
        wl = w_in[l]
        wm = jnp.concatenate([wl[:, :3 * f], wl[:, 3 * f + h8:]], axis=1).astype(BF16)
        wf = jnp.pad(wl[:, 3 * f:3 * f + h8], ((0, 0), (0, LANES - h8))).astype(BF16)
        bfp = jnp.pad(b_fgate[l].reshape(1, h8), ((0, 0), (0, LANES - h8)))

        xs, wa, wb, wo = _ffn_stream(xs, g1, ffn1_wi, ffn1_wo, gfin, layer=l, final_norm=False, tf=TF_FFN)
        xp = _ffn(xp, g1, wa, wb, wo, gfin, layer=l, final_norm=False, tm=TM_FFN1, tf=TF_FFN)

        (q, k, v, kh, vh, lfr, hq, hlf, hk, hv, hg) = _inproj(
            xp, gm, wm, wf, bfp, hgrn_lb, carry_p, layer=l, sample=False, batch=bp, tm=TM_PROJ)
        carry_p = (kh, vh, lfr)
        (q_s, k_s, v_s, kh_s, vh_s, lfr_s, hq_s, hlf_s, hk_s, hv_s, hg_s) = _inproj(
            xs, gm, wm, wf, bfp, hgrn_lb, carry_s, layer=l, sample=True, batch=bs, tm=ms)
        carry_s = (kh_s, vh_s)

        c_row = _fox_cumsum(lfr, layer=l)
        fo = _fox_prompt(q.reshape(bp, lp, f), k.reshape(bp, lp, f), v.reshape(bp, lp, f),
                         c_row.reshape(bp, h8, lp // T_ATTN, T_ATTN), t=T_ATTN, pairs=FOX_PAIRS)
        lfs_l = lfr_s.reshape(h8, bs, ls).transpose(1, 0, 2)
        lfs.append(lfs_l)
        lfs_pad = jnp.pad(lfs_l, ((0, 0), (0, 0), (0, LANES - ls))).reshape(bs * h8, LANES)
        c_cache, cn_row = _fox_sample_scan(clf, lfs_pad, layer=l)
        fo_s = _fox_sample(q_s.reshape(bs, ls, f), k_s.reshape(bs, ls, f), v_s.reshape(bs, ls, f),
                           cache_kt, cache_vt, c_cache, cn_row, layer=l)
        to3 = lambda a, b_, l_: a.reshape(b_, l_, w)
        ho, st_p = _hgrn(to3(hq, bp, lp), to3(hlf, bp, lp), to3(hk, bp, lp), to3(hv, bp, lp), None, st_p,
                         layer=l, c=C_HGRN, tl=TL_HGRN, sps=1, hps=HGRN_PROMPT_HEADS)
        ho_s, st_s = _hgrn(to3(hq_s, bs, ls), to3(hlf_s, bs, ls), to3(hk_s, bs, ls), to3(hv_s, bs, ls),
                           state_hgrn, st_s, layer=l, c=ls, tl=ls, sps=HGRN_SAMPLE_STREAMS, hps=HGRN_HEADS)

        xs, wa, wb, wo, wout = _ffn_stream(
            xs, g2, ffn2_wi, ffn2_wo, gfin, (fo_s.reshape(ms, f), ho_s.reshape(ms, w), hg_s, gn, w_out),
            layer=l, final_norm=last, tf=TF_FFN)
        xp = _ffn(xp, g2, wa, wb, wo, gfin, (fo.reshape(mp, f), ho.reshape(mp, w), hg, gn, wout),
                  layer=l, final_norm=last, tm=TM_FFN, tf=TF_FFN)

    k_prompt = jnp.swapaxes(carry_p[0], 3, 4)
    v_prompt = jnp.swapaxes(carry_p[1], 3, 4)
    return (xp.reshape(bp, lp, d), xs.reshape(bs, ls, d), k_prompt, v_prompt, carry_p[2], st_p,
            carry_s[0], carry_s[1], jnp.stack(lfs), st_s)
```

```python
import functools

import jax
import jax.numpy as jnp
from jax import lax
from jax.experimental import pallas as pl
from jax.experimental.pallas import tpu as pltpu

F32 = jnp.float32
BF16 = jnp.bfloat16

DEPTH = 2
FOX_HEADS = 8
FOX_HEAD_DIM = 64
FOX_WIDTH = FOX_HEADS * FOX_HEAD_DIM
HGRN_HEADS = 4
HGRN_HEAD_DIM = 128
HGRN_WIDTH = HGRN_HEADS * HGRN_HEAD_DIM
RMS_EPS = 1e-6
FOX_SCALE = FOX_HEAD_DIM ** -0.5
LOG2E = 1.4426950408889634

LANES = 128
VMEM_LIMIT_BYTES = 56 * 1024 * 1024
HGRN_FACTOR_LIMIT = 50.0
FOX_SKIP_GAP = 153.0
FOX_NORM_SLACK = 1.001
FOX_FIRST_ROUND_BLOCKS = 3

TM_FFN = 512
TM_FFN1 = 1024
TF_FFN = 256
TM_PROJ = 512
T_ATTN = 256
FOX_PAIRS = 4
C_HGRN = 64
TL_HGRN = 1024
HGRN_PROMPT_HEADS = 4
HGRN_SAMPLE_STREAMS = 4
KEY_NORM_ROWS = 1024


def _cparams(*sem):
    return pltpu.CompilerParams(dimension_semantics=sem, vmem_limit_bytes=VMEM_LIMIT_BYTES)


def _rms(x, g):
    ms = jnp.mean(x * x, axis=-1, keepdims=True)
    return x * lax.rsqrt(ms + RMS_EPS) * g


def _silu(x):
    return x * jax.nn.sigmoid(x)


def _log1p_exp_neg(d):
    return jnp.log(1.0 + jnp.exp(-d))


def _log_sigmoid(z):
    return jnp.minimum(z, 0.0) - _log1p_exp_neg(jnp.abs(z))


def _gate_terms(z):
    e = jnp.exp(-jnp.abs(z))
    t = 1.0 + e
    return jnp.minimum(z, 0.0) - jnp.log(t), jnp.where(z >= 0.0, e, 1.0) / t


def _dot(a, b):
    return jnp.dot(a, b, preferred_element_type=F32)


def _dot_nt(a, b):
    return lax.dot_general(a, b, (((1,), (1,)), ((), ())), preferred_element_type=F32)


def _dot_tn(a, b):
    return lax.dot_general(a, b, (((0,), (0,)), ((), ())), preferred_element_type=F32)


def _scan(x, axis, period=None):
    n = x.shape[axis] if period is None else period
    idx = lax.broadcasted_iota(jnp.int32, x.shape, axis)
    if period is not None:
        idx = idx & (period - 1)
    s = 1
    while s < n:
        x = x + jnp.where(idx >= s, pltpu.roll(x, s, axis), 0.0)
        s *= 2
    return x


def _drop_alias_refs(body, n_in, n_alias):
    def wrapped(*refs):
        return body(*refs[:n_in], *refs[n_in + n_alias:])
    return wrapped


_ANY = pl.BlockSpec(memory_space=pl.ANY)


def _mix(fo_ref, ho_ref, hg_ref, gn_ref):
    gn = gn_ref[0]
    d = HGRN_HEAD_DIM
    parts = [fo_ref[...]]
    for h in range(HGRN_HEADS):
        ho = ho_ref[:, h * d:(h + 1) * d]
        parts.append((_rms(ho, gn) * _silu(hg_ref[:, h * d:(h + 1) * d])).astype(BF16))
    return jnp.concatenate(parts, axis=1)


def _ffn_body(*refs, final_norm, with_outproj, tf):
    if with_outproj:
        (x_ref, fo_ref, ho_ref, hg_ref, gn_ref, wout_ref,
         g_ref, wa_ref, wb_ref, wo_ref, gf_ref, o_ref, acc_ref) = refs
        x = x_ref[...] + _dot(_mix(fo_ref, ho_ref, hg_ref, gn_ref), wout_ref[...])
    else:
        x_ref, g_ref, wa_ref, wb_ref, wo_ref, gf_ref, o_ref, acc_ref = refs
        x = x_ref[...]
    f = wo_ref.shape[0]
    xn = _rms(x, g_ref[0]).astype(BF16)
    for c in range(f // tf):
        a = _dot(xn, wa_ref[:, c * tf:(c + 1) * tf])
        b = _dot(xn, wb_ref[:, c * tf:(c + 1) * tf])
        part = _dot((_silu(a) * b).astype(BF16), wo_ref[c * tf:(c + 1) * tf, :])
        if c == 0:
            acc_ref[...] = part
        else:
            acc_ref[...] += part
    y = x + 0.5 * acc_ref[...]
    if final_norm:
        y = _rms(y, gf_ref[...])
    o_ref[...] = y


def _ffn(x, g, wa, wb, wo, gf, outproj=None, *, layer, final_norm, tm, tf):
    m, d = x.shape
    tok = lambda width: pl.BlockSpec((tm, width), lambda i: (i, 0))
    held = lambda a: pl.BlockSpec(a.shape, lambda i: (0,) * a.ndim, pipeline_mode=pl.Buffered(1))
    args, in_specs = [x], [tok(d)]
    if outproj is not None:
        fo, ho, hg, gn, wout = outproj
        args += [fo, ho, hg, gn, wout]
        in_specs += [tok(fo.shape[1]), tok(ho.shape[1]), tok(hg.shape[1]),
                     pl.BlockSpec((1, 1, gn.shape[2]), lambda i: (layer, 0, 0)), held(wout)]
    args += [g, wa, wb, wo, gf]
    in_specs += [pl.BlockSpec((1, 1, d), lambda i: (layer, 0, 0)), held(wa), held(wb), held(wo),
                 pl.BlockSpec((1, d), lambda i: (0, 0))]
    return pl.pallas_call(
        functools.partial(_ffn_body, final_norm=final_norm, with_outproj=outproj is not None, tf=tf),
        grid=(m // tm,),
        in_specs=in_specs,
        out_specs=tok(d),
        out_shape=jax.ShapeDtypeStruct((m, d), F32),
        scratch_shapes=[pltpu.VMEM((tm, d), F32)],
        compiler_params=_cparams("parallel"),
        name="ffn",
    )(*args)


def _ffn_stream_body(*refs, final_norm, with_outproj):
    if with_outproj:
        (x_ref, fo_ref, ho_ref, hg_ref, gn_ref, wout_ref, g_ref, wa_ref, wb_ref, wo_ref, gf_ref,
         o_ref, wab_ref, wbb_ref, wob_ref, woutb_ref, xr_ref, xn_ref, acc_ref) = refs
    else:
        (x_ref, g_ref, wa_ref, wb_ref, wo_ref, gf_ref,
         o_ref, wab_ref, wbb_ref, wob_ref, xr_ref, xn_ref, acc_ref) = refs
    j = pl.program_id(0)

    @pl.when(j == 0)
    def _():
        x = x_ref[...]
        if with_outproj:
            woutb_ref[...] = wout_ref[0].astype(BF16)
            x = x + _dot(_mix(fo_ref, ho_ref, hg_ref, gn_ref), woutb_ref[...])
        xr_ref[...] = x
        xn_ref[...] = _rms(x, g_ref[0]).astype(BF16)
        acc_ref[...] = jnp.zeros_like(acc_ref)

    wab_ref[...] = wa_ref[0].astype(BF16)
    wbb_ref[...] = wb_ref[0].astype(BF16)
    wob_ref[...] = wo_ref[0].astype(BF16)
    xn = xn_ref[...]
    h = _silu(_dot(xn, wab_ref[...])) * _dot(xn, wbb_ref[...])
    acc_ref[...] += _dot(h.astype(BF16), wob_ref[...])

    @pl.when(j == pl.num_programs(0) - 1)
    def _():
        y = xr_ref[...] + 0.5 * acc_ref[...]
        if final_norm:
            y = _rms(y, gf_ref[...])
        o_ref[...] = y


def _ffn_stream(x, g, wi, wo, gf, outproj=None, *, layer, final_norm, tf):
    m, d = x.shape
    f = wo.shape[1]
    nf = f // tf
    whole = lambda rows, width: pl.BlockSpec((rows, width), lambda j: (0, 0))
    args, in_specs = [x], [whole(m, d)]
    out_specs = [whole(m, d), pl.BlockSpec((d, tf), lambda j: (0, j)), pl.BlockSpec((d, tf), lambda j: (0, j)),
                 pl.BlockSpec((tf, d), lambda j: (j, 0))]
    out_shape = [jax.ShapeDtypeStruct((m, d), F32), jax.ShapeDtypeStruct((d, f), BF16),
                 jax.ShapeDtypeStruct((d, f), BF16), jax.ShapeDtypeStruct((f, d), BF16)]
    if outproj is not None:
        fo, ho, hg, gn, wout = outproj
        dm = wout.shape[1]
        args += [fo, ho, hg, gn, wout]
        in_specs += [whole(m, fo.shape[1]), whole(m, ho.shape[1]), whole(m, hg.shape[1]),
                     pl.BlockSpec((1, 1, gn.shape[2]), lambda j: (layer, 0, 0)),
                     pl.BlockSpec((1, dm, d), lambda j: (layer, 0, 0))]
        out_specs.append(whole(dm, d))
        out_shape.append(jax.ShapeDtypeStruct((dm, d), BF16))
    args += [g, wi, wi, wo, gf]
    in_specs += [pl.BlockSpec((1, 1, d), lambda j: (layer, 0, 0)),
                 pl.BlockSpec((1, d, tf), lambda j: (layer, 0, j)),
                 pl.BlockSpec((1, d, tf), lambda j: (layer, 0, j + nf)),
                 pl.BlockSpec((1, tf, d), lambda j: (layer, j, 0)),
                 whole(1, d)]
    return pl.pallas_call(
        functools.partial(_ffn_stream_body, final_norm=final_norm, with_outproj=outproj is not None),
        grid=(nf,),
        in_specs=in_specs,
        out_specs=out_specs,
        out_shape=out_shape,
        scratch_shapes=[pltpu.VMEM((m, d), F32), pltpu.VMEM((m, d), BF16), pltpu.VMEM((m, d), F32)],
        compiler_params=_cparams("arbitrary"),
        name="ffn_stream",
    )(*args)


def _inproj_body(x_ref, g_ref, wm_ref, wf_ref, bf_ref, lbp_ref,
                 q_ref, k_ref, v_ref, kh_ref, vh_ref, lfr_ref,
                 hq_ref, hlf_ref, hk_ref, hv_ref, hg_ref, *, layer, sample):
    h = _rms(x_ref[...], g_ref[0]).astype(BF16)
    w, hd = FOX_WIDTH, FOX_HEAD_DIM
    wc = 2 * LANES
    heads_per_chunk = wc // hd

    def chunks(i):
        for n in range(w // wc):
            yield n, slice(n * wc, (n + 1) * wc), _dot(h, wm_ref[:, i * w + n * wc:i * w + (n + 1) * wc])

    def store_heads(ref, n, val):
        h0 = n * heads_per_chunk
        if sample:
            for hh in range(heads_per_chunk):
                piece = val[:, hh * hd:(hh + 1) * hd]
                ref[0, :, h0 + hh] = piece.reshape(ref.shape[1], ref.shape[3], hd)
        else:
            ref[0, 0, h0:h0 + heads_per_chunk] = val.T.reshape(heads_per_chunk, hd, val.shape[0])

    p = lbp_ref[...]
    e = jnp.exp(p - jnp.max(p, axis=0, keepdims=True))
    sm = e / jnp.sum(e, axis=0, keepdims=True)
    cs = sm[0:1]
    for r in range(1, layer + 1):
        cs = cs + sm[r:r + 1]
    lb = cs - sm[0:1]
    log_lb = jnp.log(lb)
    log_1m_lb = jnp.log1p(-lb)

    for n, cols, z in chunks(4):
        log_sig, sig_neg = _gate_terms(z)
        a = log_lb[:, cols]
        b = log_1m_lb[:, cols] + log_sig
        hlf_ref[:, cols] = jnp.maximum(a, b) + _log1p_exp_neg(jnp.abs(a - b))
        hk_ref[:, cols] = (1.0 - lb[:, cols]) * sig_neg
    for n, cols, hq in chunks(3):
        hq_ref[:, cols] = _silu(hq)

    lf = _log_sigmoid(_dot(h, wf_ref[...]) + bf_ref[...])
    lfr = lf.T[:FOX_HEADS, :]
    if sample:
        lfr_ref[...] = lfr
    else:
        lfr_ref[0, 0] = lfr

    for n, cols, q in chunks(0):
        q_ref[:, cols] = (q * (FOX_SCALE * LOG2E)).astype(BF16)
    for n, cols, k in chunks(1):
        store_heads(kh_ref, n, k)
        k_ref[:, cols] = k.astype(BF16)
    for n, cols, v in chunks(2):
        store_heads(vh_ref, n, v)
        v_ref[:, cols] = v.astype(BF16)
    for n, cols, hv in chunks(5):
        hv_ref[:, cols] = hv.astype(BF16)
    for n, cols, hg in chunks(6):
        hg_ref[:, cols] = hg


def _inproj(x, g, wm, wf, bfp, lbp, carried, *, layer, sample, batch, tm):
    m, d = x.shape
    w, hd, nh = FOX_WIDTH, FOX_HEAD_DIM, FOX_HEADS
    seq = m // batch
    tiles_per_seq = seq // tm if not sample else 1
    tok = lambda width: pl.BlockSpec((tm, width), lambda i: (i, 0))
    full = lambda a: pl.BlockSpec(a.shape, lambda i: (0,) * a.ndim)
    sds = jax.ShapeDtypeStruct
    if sample:
        heads_shape = (DEPTH, batch, nh, seq, hd)
        heads_spec = pl.BlockSpec((1, batch, nh, seq, hd), lambda i: (layer, 0, 0, 0, 0))
        lfr_shape, lfr_spec = (nh, m), pl.BlockSpec((nh, m), lambda i: (0, 0))
    else:
        heads_shape = (DEPTH, batch, nh, hd, seq)
        heads_spec = pl.BlockSpec((1, 1, nh, hd, tm),
                                  lambda i: (layer, i // tiles_per_seq, 0, 0, i % tiles_per_seq))
        lfr_shape = (DEPTH, batch, nh, seq)
        lfr_spec = pl.BlockSpec((1, 1, nh, tm), lambda i: (layer, i // tiles_per_seq, 0, i % tiles_per_seq))
    out_specs = [tok(w), tok(w), tok(w), heads_spec, heads_spec, lfr_spec,
                 tok(w), tok(w), tok(w), tok(w), tok(w)]
    out_shape = [sds((m, w), BF16), sds((m, w), BF16), sds((m, w), BF16),
                 sds(heads_shape, F32), sds(heads_shape, F32), sds(lfr_shape, F32),
                 sds((m, w), F32), sds((m, w), F32), sds((m, w), F32), sds((m, w), BF16), sds((m, w), F32)]
    body = functools.partial(_inproj_body, layer=layer, sample=sample)
    args = [x, g, wm, wf, bfp, lbp]
    in_specs = [tok(d), pl.BlockSpec((1, 1, d), lambda i: (layer, 0, 0)), full(wm), full(wf), full(bfp), full(lbp)]
    aliases = {}
    if carried is not None:
        out_index = (3, 4, 5)
        for n, buf in enumerate(carried):
            aliases[len(args)] = out_index[n]
            args.append(buf)
            in_specs.append(_ANY)
        body = _drop_alias_refs(body, 6, len(carried))
    return pl.pallas_call(
        body, grid=(m // tm,), in_specs=in_specs, out_specs=out_specs, out_shape=out_shape,
        input_output_aliases=aliases, compiler_params=_cparams("arbitrary"), name="inproj",
    )(*args)


def _cumsum_body(lfr_ref, cr_ref):
    cr_ref[0] = _scan(lfr_ref[0, 0], 1) * LOG2E


def _fox_cumsum(lf_rows, *, layer):
    _, b, h, l = lf_rows.shape
    return pl.pallas_call(
        _cumsum_body,
        grid=(b,),
        in_specs=[pl.BlockSpec((1, 1, h, l), lambda i: (layer, i, 0, 0))],
        out_specs=pl.BlockSpec((1, h, l), lambda i: (i, 0, 0)),
        out_shape=jax.ShapeDtypeStruct((b, h, l), F32),
        compiler_params=_cparams("parallel"),
        name="fox_cumsum",
    )(lf_rows)


def _fox_prompt_body(q_ref, k_ref, v_ref, cr_ref, o_ref, m_ref, l_ref, acc_ref, kn_ref, trips_ref, *, t, pairs):
    i = pl.program_id(2)
    nk = cr_ref.shape[2]
    hd = FOX_HEAD_DIM
    lane = lax.broadcasted_iota(jnp.int32, (t, LANES), 1)
    row = lax.broadcasted_iota(jnp.int32, (t, t), 0)
    col = lax.broadcasted_iota(jnp.int32, (t, t), 1)
    head_lanes = [lane < hd, lane >= hd]
    lanes_of = [slice(p * LANES, (p + 1) * LANES) for p in range(pairs)]

    @pl.when(i == 0)
    def _():
        step = KEY_NORM_ROWS
        lane_s = lax.broadcasted_iota(jnp.int32, (step, LANES), 1)
        for p in range(pairs):
            def chunk(ci, mx, p=p):
                kf = k_ref[0, pl.ds(pl.multiple_of(ci * step, step), step), lanes_of[p]].astype(F32)
                k2 = kf * kf
                n0 = jnp.max(jnp.sum(jnp.where(lane_s < hd, k2, 0.0), axis=1, keepdims=True))
                n1 = jnp.max(jnp.sum(jnp.where(lane_s >= hd, k2, 0.0), axis=1, keepdims=True))
                return jnp.maximum(mx[0], n0), jnp.maximum(mx[1], n1)

            n0, n1 = lax.fori_loop(0, k_ref.shape[1] // step, chunk, (jnp.float32(0.0), jnp.float32(0.0)))
            kn_ref[2 * p] = jnp.sqrt(n0)
            kn_ref[2 * p + 1] = jnp.sqrt(n1)

    causal = jnp.concatenate([col <= row, col <= row], axis=0)
    qs, qn = [], []
    for p in range(pairs):
        qq = q_ref[0, :, lanes_of[p]]
        zero = jnp.zeros_like(qq)
        qs.append(jnp.concatenate([jnp.where(head_lanes[h], qq, zero) for h in range(2)], axis=0))
        q2 = qq.astype(F32) * qq.astype(F32)
        pair_norm = jnp.sqrt(jnp.sum(q2, axis=1, keepdims=True))
        qn.append(jnp.concatenate([pair_norm, pair_norm], axis=0))

    def scores(p, j, diag):
        kk = k_ref[0, pl.ds(pl.multiple_of(j * t, t), t), lanes_of[p]]
        s = _dot_nt(qs[p], kk)
        s = jnp.concatenate(
            [s[h * t:(h + 1) * t] - cr_ref[0, 2 * p + h, pl.ds(j, 1), :] for h in range(2)], axis=0)
        return jnp.where(causal, s, -jnp.inf) if diag else s

    def values(p, j):
        return v_ref[0, pl.ds(pl.multiple_of(j * t, t), t), lanes_of[p]]

    def write_out(p, acc, l):
        o = acc / l
        o_ref[0, :, lanes_of[p]] = jnp.where(head_lanes[0], o[:t], o[t:]).astype(BF16)

    def first_round(js, j_next):
        def lane_groups(pr):
            return functools.reduce(jnp.add, [pr[:, g * LANES:(g + 1) * LANES] for g in range(t // LANES)])

        state = []
        for p in range(pairs):
            m = l = acc = None
            for n, j in enumerate(js):
                s = scores(p, j, n == len(js) - 1)
                m_blk = jnp.max(s, axis=1, keepdims=True)
                if m is None:
                    m_next = m_blk
                    pr = jnp.exp2(s - m_next)
                    l = lane_groups(pr)
                    acc = _dot(pr.astype(BF16), values(p, j))
                else:
                    m_next = jnp.maximum(m, m_blk)
                    alpha = jnp.exp2(m - m_next)
                    pr = jnp.exp2(s - m_next)
                    l = alpha * l + lane_groups(pr)
                    acc = alpha * acc + _dot(pr.astype(BF16), values(p, j))
                m = m_next
            state.append((m, jnp.sum(l, axis=1, keepdims=True), acc))

        jidx = lax.broadcasted_iota(jnp.int32, (nk, 1), 0)
        first_needed = jnp.full((nk, 1), nk, jnp.int32)
        for p in range(pairs):
            m = state[p][0]
            for h in range(2):
                rows = slice(h * t, (h + 1) * t)
                gap = qn[p][rows] * (kn_ref[2 * p + h] * FOX_NORM_SLACK) - m[rows]
                reach = jnp.max(gap, axis=0, keepdims=True)
                c_min = jnp.min(cr_ref[0, 2 * p + h], axis=1, keepdims=True)
                first_needed = jnp.minimum(first_needed, jnp.where(reach - c_min >= -FOX_SKIP_GAP, jidx, nk))
        trips = j_next + 1 - jnp.minimum(jnp.min(first_needed), j_next + 1)
        trips_ref[0] = trips

        @pl.when(trips == 0)
        def _():
            for p in range(pairs):
                write_out(p, state[p][2], state[p][1])

        @pl.when(trips > 0)
        def _():
            for p in range(pairs):
                m, l, acc = state[p]
                m_ref[p] = jnp.broadcast_to(m, m_ref.shape[1:])
                l_ref[p] = jnp.broadcast_to(l, l_ref.shape[1:])
                acc_ref[p] = acc

    def block(j):
        for p in range(pairs):
            s = scores(p, j, False)
            m_prev = m_ref[p]
            m_next = jnp.maximum(m_prev, jnp.max(s, axis=1, keepdims=True))
            pr = jnp.exp2(s - jnp.concatenate([m_next] * (t // LANES), axis=1))
            alpha = jnp.exp2(m_prev - m_next)
            l_ref[p] = alpha * l_ref[p] + jnp.sum(pr, axis=1, keepdims=True)
            m_ref[p] = m_next
            acc_ref[p] = alpha * acc_ref[p] + _dot(pr.astype(BF16), values(p, j))

    window = FOX_FIRST_ROUND_BLOCKS

    @pl.when(i >= window - 1)
    def _():
        first_round([i - (window - 1) + n for n in range(window)], i - window)

    @pl.when(i < window - 1)
    def _():
        first_round([i], i - 1)

    trips = trips_ref[0]

    @pl.when(trips > 0)
    def _():
        j_next = jnp.where(i >= window - 1, i - window, i - 1)

        def earlier(n, carry):
            block(j_next - n)
            return carry

        lax.fori_loop(0, trips, earlier, 0)
        for p in range(pairs):
            write_out(p, acc_ref[p], l_ref[p])


def _fox_prompt(q, k, v, c_rows, *, t, pairs):
    b, l, w = q.shape
    ngrp = w // (pairs * LANES)
    nk = l // t
    wide = pairs * LANES
    return pl.pallas_call(
        functools.partial(_fox_prompt_body, t=t, pairs=pairs),
        grid=(b, ngrp, l // t),
        in_specs=[
            pl.BlockSpec((1, t, wide), lambda bi, g, i: (bi, i, g)),
            pl.BlockSpec((1, l, wide), lambda bi, g, i: (bi, 0, g), pipeline_mode=pl.Buffered(1)),
            pl.BlockSpec((1, l, wide), lambda bi, g, i: (bi, 0, g), pipeline_mode=pl.Buffered(1)),
            pl.BlockSpec((1, 2 * pairs, nk, t), lambda bi, g, i: (bi, g, 0, 0)),
        ],
        out_specs=pl.BlockSpec((1, t, wide), lambda bi, g, i: (bi, i, g)),
        out_shape=jax.ShapeDtypeStruct((b, l, w), BF16),
        scratch_shapes=[pltpu.VMEM((pairs, 2 * t, LANES), F32), pltpu.VMEM((pairs, 2 * t, LANES), F32),
                        pltpu.VMEM((pairs, 2 * t, LANES), F32), pltpu.SMEM((2 * pairs,), F32),
                        pltpu.SMEM((1,), jnp.int32)],
        compiler_params=_cparams("parallel", "parallel", "arbitrary"),
        name="fox_prompt",
    )(q, k, v, c_rows)


def _sample_scan_body(clf_ref, lfr_ref, cc_ref, cnr_ref):
    cc_ref[...] = _scan(clf_ref[0], 1) * LOG2E
    cnr_ref[...] = _scan(lfr_ref[...], 1) * LOG2E


def _fox_sample_scan(clf, lf_row, *, layer):
    _, r, p_len = clf.shape
    full = lambda a: pl.BlockSpec(a.shape, lambda i: (0,) * a.ndim)
    return pl.pallas_call(
        _sample_scan_body,
        grid=(1,),
        in_specs=[pl.BlockSpec((1, r, p_len), lambda i: (layer, 0, 0)), full(lf_row)],
        out_specs=[pl.BlockSpec((r, p_len), lambda i: (0, 0)), full(lf_row)],
        out_shape=[jax.ShapeDtypeStruct((r, p_len), F32), jax.ShapeDtypeStruct(lf_row.shape, F32)],
        compiler_params=_cparams("arbitrary"),
        name="fox_sample_scan",
    )(clf, lf_row)


def _fox_sample_body(q_ref, k_ref, v_ref, ck_ref, cv_ref, cc_ref, cnr_ref, o_ref):
    t = q_ref.shape[1]
    nh, p_len = ck_ref.shape[2], ck_ref.shape[4]
    hd = FOX_HEAD_DIM
    lane = lax.broadcasted_iota(jnp.int32, (t, LANES), 1)
    row = lax.broadcasted_iota(jnp.int32, (t, t), 0)
    col = lax.broadcasted_iota(jnp.int32, (t, t), 1)
    c_cache = cc_ref[...]
    cn_row = cnr_ref[...]
    outs = []
    for h in range(nh):
        grp = slice((h // 2) * LANES, (h // 2 + 1) * LANES)
        lo = (h % 2) * hd
        qq = q_ref[0, :, grp]
        qm = jnp.where((lane >= lo) & (lane < lo + hd), qq, jnp.zeros_like(qq))
        q_h = qq[:, lo:lo + hd]
        kc_t = ck_ref[0, 0, h].astype(BF16)
        vc_t = cv_ref[0, 0, h].astype(BF16)
        c_h = c_cache[h:h + 1, :]
        tot = c_h[:, p_len - 1:p_len]
        s_c = _dot(q_h, kc_t) + (tot - c_h)
        s_s = _dot_nt(qm, k_ref[0, :, grp]) - cn_row[h:h + 1, :t]
        s_s = jnp.where(col <= row, s_s, -jnp.inf)
        m = jnp.maximum(jnp.max(s_c, axis=1, keepdims=True), jnp.max(s_s, axis=1, keepdims=True))
        p_c = jnp.exp2(s_c - m)
        p_s = jnp.exp2(s_s - m)
        den = jnp.sum(p_c, axis=1, keepdims=True) + jnp.sum(p_s, axis=1, keepdims=True)
        o = _dot_nt(p_c.astype(BF16), vc_t) + _dot(p_s.astype(BF16), v_ref[0, :, grp])[:, lo:lo + hd]
        outs.append(o / den)
    o_ref[0] = jnp.concatenate(outs, axis=1).astype(BF16)


def _fox_sample(q, k, v, cache_k, cache_v, c_cache, cn_row, *, layer):
    b, t, w = q.shape
    nh, hd, p_len = cache_k.shape[2:]
    tokq = pl.BlockSpec((1, t, w), lambda bi: (bi, 0, 0))
    cache = pl.BlockSpec((1, 1, nh, hd, p_len), lambda bi: (layer, bi, 0, 0, 0))
    return pl.pallas_call(
        _fox_sample_body,
        grid=(b,),
        in_specs=[tokq, tokq, tokq, cache, cache,
                  pl.BlockSpec((nh, p_len), lambda bi: (bi, 0)),
                  pl.BlockSpec((nh, LANES), lambda bi: (bi, 0))],
        out_specs=tokq,
        out_shape=jax.ShapeDtypeStruct((b, t, w), BF16),
        compiler_params=_cparams("parallel"),
        name="fox_sample",
    )(q, k, v, cache_k, cache_v, c_cache, cn_row)


def _hgrn_body(q_ref, lf_ref, k_ref, v_ref, s0_ref, o_ref, so_ref, st_ref, tmp_ref, *,
               c, n_chunks, sps, hps, zero_init):
    ti = pl.program_id(2)
    d = HGRN_HEAD_DIM
    chains = [(s, hh) for s in range(sps) for hh in range(hps)]
    slot = {ch: n for n, ch in enumerate(chains)}
    lanes_of = [slice(hh * d, (hh + 1) * d) for hh in range(hps)]

    @pl.when(ti == 0)
    def _():
        for s, hh in chains:
            st_ref[slot[(s, hh)]] = jnp.zeros((d, d), F32) if zero_init else s0_ref[0, s, hh].T

    row = lax.broadcasted_iota(jnp.int32, (c, c), 0)
    col = lax.broadcasted_iota(jnp.int32, (c, c), 1)
    rowc = lax.broadcasted_iota(jnp.int32, (c, 1), 0)
    half = c // 2
    b_all = [_scan(lf_ref[s], 0, period=c) for s in range(sps)]

    span = jnp.zeros((1, hps * d), F32)
    for s in range(sps):
        for ci in range(n_chunks):
            b = b_all[s][ci * c:(ci + 1) * c]
            b_mid = b[half - 1:half]
            span = jnp.maximum(span, jnp.maximum(b[0:1] - b_mid, b_mid - b[c - 1:c]))
    factorable = jnp.max(span) <= HGRN_FACTOR_LIMIT

    def carry_state(st, k, v, b):
        b_last = b[c - 1:c]
        kh = (k * jnp.exp(b_last - b)).astype(BF16)
        return st * jnp.exp(b_last) + _dot_tn(v, kh)

    @pl.when(factorable)
    def _():
        units = [(ch, ci) for ch in chains for ci in range(n_chunks)]
        tile_of = {u: (u[0][0], slice(u[1] * c, (u[1] + 1) * c), lanes_of[u[0][1]]) for u in units}
        sc, ds, qe, grow = {}, {}, {}, {}
        for u in units:
            s, rs, hs = tile_of[u]
            q = q_ref[s, rs, hs]
            k = k_ref[s, rs, hs]
            b = b_all[s][rs, hs]
            b_mid = b[half - 1:half]
            b_last = b[c - 1:c]
            qe[u] = (q * jnp.exp(b)).astype(BF16)
            grow[u] = jnp.exp(b_last)
            qt = (q * jnp.exp(b - b_mid)).astype(BF16)
            kt = (k * jnp.exp(b_mid - b)).astype(BF16)
            kh = (k * jnp.exp(b_last - b)).astype(BF16)
            sc[u] = _dot_nt(qt, kt)
            ds[u] = _dot_tn(v_ref[s, rs, hs], kh)
        st_in = {}
        for ch in chains:
            st = st_ref[slot[ch]]
            for ci in range(n_chunks):
                st_in[(ch, ci)] = st.astype(BF16)
                st = st * grow[(ch, ci)] + ds[(ch, ci)]
            st_ref[slot[ch]] = st
        for u in units:
            s, rs, hs = tile_of[u]
            intra = _dot(jnp.where(col <= row, sc[u], 0.0).astype(BF16), v_ref[s, rs, hs])
            o_ref[s, rs, hs] = _dot_nt(qe[u], st_in[u]) + intra

    @pl.when(jnp.logical_not(factorable))
    def _():
        for s, hh in chains:
            hs = lanes_of[hh]

            def chunk(ci, carry, s=s, hs=hs, sl=slot[(s, hh)]):
                rs = pl.ds(pl.multiple_of(ci * c, c), c)
                q = q_ref[s, rs, hs]
                k = k_ref[s, rs, hs]
                v = v_ref[s, rs, hs]
                b = _scan(lf_ref[s, rs, hs], 0)
                st = st_ref[sl]
                tmp_ref[0] = b
                tmp_ref[1] = k
                tmp_ref[2] = v.astype(F32)

                def key_row(r, acc):
                    bs = tmp_ref[0, pl.ds(r, 1), :]
                    ks = tmp_ref[1, pl.ds(r, 1), :]
                    vs = tmp_ref[2, pl.ds(r, 1), :]
                    w = jnp.exp(jnp.minimum(b - bs, 0.0))
                    a = jnp.sum(q * ks * w, axis=1, keepdims=True)
                    return acc + jnp.where(rowc >= r, a, 0.0) * vs

                o = _dot_nt((q * jnp.exp(b)).astype(BF16), st.astype(BF16))
                o_ref[s, rs, hs] = lax.fori_loop(0, c, key_row, o)
                st_ref[sl] = carry_state(st, k, v, b)
                return carry

            lax.fori_loop(0, n_chunks, chunk, 0)

    @pl.when(ti == pl.num_programs(2) - 1)
    def _():
        for s, hh in chains:
            so_ref[0, s, hh] = st_ref[slot[(s, hh)]].T


def _hgrn(hq, hlf, hk, hv, s0, carried, *, layer, c, tl, sps, hps):
    b, l, w = hq.shape
    d = HGRN_HEAD_DIM
    nh = w // d
    tok = pl.BlockSpec((sps, tl, hps * d), lambda bi, h, ti: (bi, ti, h))
    st = pl.BlockSpec((1, sps, hps, d, d), lambda bi, h, ti: (layer, bi, h, 0, 0))
    zero_init = s0 is None
    body = functools.partial(_hgrn_body, c=c, n_chunks=tl // c, sps=sps, hps=hps, zero_init=zero_init)
    args, in_specs = [hq, hlf, hk, hv], [tok, tok, tok, tok]
    if zero_init:
        core = body
        body = lambda q, lf, k, v, *rest: core(q, lf, k, v, None, *rest)
    else:
        args.append(s0)
        in_specs.append(st)
    aliases = {}
    if carried is not None:
        aliases[len(args)] = 1
        body = _drop_alias_refs(body, len(args), 1)
        args.append(carried)
        in_specs.append(_ANY)
    return pl.pallas_call(
        body,
        grid=(b // sps, nh // hps, l // tl),
        in_specs=in_specs,
        out_specs=[tok, st],
        out_shape=[jax.ShapeDtypeStruct((b, l, w), F32), jax.ShapeDtypeStruct((DEPTH, b, nh, d, d), F32)],
        scratch_shapes=[pltpu.VMEM((sps * hps, d, d), F32), pltpu.VMEM((3, c, d), F32)],
        input_output_aliases=aliases,
        compiler_params=_cparams("parallel", "parallel", "arbitrary"),
        name="hgrn",
    )(*args)


def kernel(x_prompt, x_sample, cache_k, cache_v, cache_logf, state_hgrn, norm_ffn1, ffn1_wi, ffn1_wo,
           norm_mix, w_in, b_fgate, hgrn_lb, hgrn_gnorm, w_out, norm_ffn2, ffn2_wi, ffn2_wo, norm_final):
    bp, lp, d = x_prompt.shape
    bs, ls, _ = x_sample.shape
    mp, ms = bp * lp, bs * ls
    p_len = cache_k.shape[3]
    f, w, h8 = FOX_WIDTH, HGRN_WIDTH, FOX_HEADS

    xp = x_prompt.reshape(mp, d)
    xs = x_sample.reshape(ms, d)
    gfin = norm_final.reshape(1, d)
    g1, g2, gm = (a.reshape(DEPTH, 1, d) for a in (norm_ffn1, norm_ffn2, norm_mix))
    gn = hgrn_gnorm.reshape(DEPTH, 1, HGRN_HEAD_DIM)
    clf = cache_logf.reshape(DEPTH, bs * h8, p_len)
    cache_kt = jnp.swapaxes(cache_k, 3, 4)
    cache_vt = jnp.swapaxes(cache_v, 3, 4)

    carry_p = carry_s = None
    st_p = st_s = None
    lfs = []
    for l in range(DEPTH):
        last = l == DEPTH - 1
        wl = w_in[l]
        wm = jnp.concatenate([wl[:, :3 * f], wl[:, 3 * f + h8:]], axis=1).astype(BF16)
        wf = jnp.pad(wl[:, 3 * f:3 * f + h8], ((0, 0), (0, LANES - h8))).astype(BF16)
        bfp = jnp.pad(b_fgate[l].reshape(1, h8), ((0, 0), (0, LANES - h8)))

        xs, wa, wb, wo = _ffn_stream(xs, g1, ffn1_wi, ffn1_wo, gfin, layer=l, final_norm=False, tf=TF_FFN)
        xp = _ffn(xp, g1, wa, wb, wo, gfin, layer=l, final_norm=False, tm=TM_FFN1, tf=TF_FFN)

        (q, k, v, kh, vh, lfr, hq, hlf, hk, hv, hg) = _inproj(
            xp, gm, wm, wf, bfp, hgrn_lb, carry_p, layer=l, sample=False, batch=bp, tm=TM_PROJ)
        carry_p = (kh, vh, lfr)
        (q_s, k_s, v_s, kh_s, vh_s, lfr_s, hq_s, hlf_s, hk_s, hv_s, hg_s) = _inproj(
            xs, gm, wm, wf, bfp, hgrn_lb, carry_s, layer=l, sample=True, batch=bs, tm=ms)
        carry_s = (kh_s, vh_s)

        c_row = _fox_cumsum(lfr, layer=l)
        fo = _fox_prompt(q.reshape(bp, lp, f), k.reshape(bp, lp, f), v.reshape(bp, lp, f),
                         c_row.reshape(bp, h8, lp // T_ATTN, T_ATTN), t=T_ATTN, pairs=FOX_PAIRS)
        lfs_l = lfr_s.reshape(h8, bs, ls).transpose(1, 0, 2)
        lfs.append(lfs_l)
        lfs_pad = jnp.pad(lfs_l, ((0, 0), (0, 0), (0, LANES - ls))).reshape(bs * h8, LANES)
        c_cache, cn_row = _fox_sample_scan(clf, lfs_pad, layer=l)
        fo_s = _fox_sample(q_s.reshape(bs, ls, f), k_s.reshape(bs, ls, f), v_s.reshape(bs, ls, f),
                           cache_kt, cache_vt, c_cache, cn_row, layer=l)
        to3 = lambda a, b_, l_: a.reshape(b_, l_, w)
        ho, st_p = _hgrn(to3(hq, bp, lp), to3(hlf, bp, lp), to3(hk, bp, lp), to3(hv, bp, lp), None, st_p,
                         layer=l, c=C_HGRN, tl=TL_HGRN, sps=1, hps=HGRN_PROMPT_HEADS)
        ho_s, st_s = _hgrn(to3(hq_s, bs, ls), to3(hlf_s, bs, ls), to3(hk_s, bs, ls), to3(hv_s, bs, ls),
                           state_hgrn, st_s, layer=l, c=ls, tl=ls, sps=HGRN_SAMPLE_STREAMS, hps=HGRN_HEADS)

        xs, wa, wb, wo, wout = _ffn_stream(
            xs, g2, ffn2_wi, ffn2_wo, gfin, (fo_s.reshape(ms, f), ho_s.reshape(ms, w), hg_s, gn, w_out),
            layer=l, final_norm=last, tf=TF_FFN)
        xp = _ffn(xp, g2, wa, wb, wo, gfin, (fo.reshape(mp, f), ho.reshape(mp, w), hg, gn, wout),
                  layer=l, final_norm=last, tm=TM_FFN, tf=TF_FFN)

    k_prompt = jnp.swapaxes(carry_p[0], 3, 4)
    v_prompt = jnp.swapaxes(carry_p[1], 3, 4)
    return (xp.reshape(bp, lp, d), xs.reshape(bs, ls, d), k_prompt, v_prompt, carry_p[2], st_p,
            carry_s[0], carry_s[1], jnp.stack(lfs), st_s)
```
